```python
import jax, jax.numpy as jnp
from jax import lax
import numpy as np

D_MODEL = 1024
BATCH = 16
SEQ = 2048
DEPTH = 2
DEC_BATCH = 128
DEC_SEQ = 1
PAST_LEN = 8192
PAGE_SIZE = 128

N_A = DEPTH // 2
N_B = DEPTH - N_A
RWKV_HEAD = 64
RWKV_H = D_MODEL // RWKV_HEAD
DECAY_LORA = 64
AAA_LORA = 64
N_MIX = 6
GN_EPS = 64e-5
MLA_H = 8
QK_NOPE = 128
QK_ROPE = 64
V_HEAD = 128
Q_LORA = 384
KV_LORA = 256
ROPE_THETA = 10000.0
ATTN_SCALE = (QK_NOPE + QK_ROPE) ** -0.5
Q_BLOCK = 128
EPS = 1e-6

kernel_name = "yoco_rwkv7_mla_step"


def rmsnorm(x, g):
    xf = x.astype(jnp.float32)
    y = xf * lax.rsqrt(jnp.mean(xf * xf, axis=-1, keepdims=True) + EPS)
    return (y * g.astype(jnp.float32)).astype(x.dtype)


def rope(x, pos):
    half = x.shape[-1] // 2
    inv = ROPE_THETA ** (-jnp.arange(half, dtype=jnp.float32) / half)
    ang = pos.astype(jnp.float32)[:, None] * inv[None, :]
    cos, sin = jnp.cos(ang)[:, None, :], jnp.sin(ang)[:, None, :]
    x1 = x[..., :half].astype(jnp.float32)
    x2 = x[..., half:].astype(jnp.float32)
    return jnp.concatenate([x1 * cos - x2 * sin, x1 * sin + x2 * cos], axis=-1).astype(x.dtype)


def rwkv7_time_mix(h, h_prev, S0, mu, w_in, w0, w1, w2, a0, a1, a2, k_k, k_a, r_k, gn_w, gn_b, w_out):
    B, T, D = h.shape
    f32 = jnp.float32
    xx = jnp.concatenate([h_prev[:, None, :].astype(h.dtype), h[:, :-1]], axis=1) - h
    xm = h[:, :, None, :] + xx[:, :, None, :] * mu
    proj = jnp.einsum('btmd,mde->btme', xm[:, :, :4], w_in)
    r, k, v, g = proj[:, :, 0], proj[:, :, 1], proj[:, :, 2], proj[:, :, 3]
    xw, xa = xm[:, :, 4], xm[:, :, 5]
    w_log = -jax.nn.softplus(-(w0 + jnp.tanh(xw @ w1) @ w2).astype(f32)) - 0.5
    decay = jnp.exp(-jnp.exp(w_log))
    a = jax.nn.sigmoid((a0 + (xa @ a1) @ a2).astype(f32))
    heads = lambda t: t.astype(f32).reshape(B, T, RWKV_H, RWKV_HEAD)
    kk = heads(k * k_k)
    kk = kk / jnp.maximum(jnp.sqrt(jnp.sum(kk * kk, axis=-1, keepdims=True)), 1e-12)
    k_mod = k.astype(f32) * (1.0 + (a - 1.0) * k_a.astype(f32))
    r_h, k_h, v_h, w_h, a_h = heads(r), heads(k_mod), heads(v), heads(decay), heads(a)
    seqs = tuple(jnp.moveaxis(t, 1, 0) for t in (r_h, w_h, k_h, v_h, kk, a_h))

    def step(S, inp):
        r_t, w_t, k_t, v_t, kk_t, a_t = inp
        sa = jnp.einsum('bhij,bhj->bhi', S, -kk_t)
        S = (S * w_t[:, :, None, :] + sa[..., None] * (kk_t * a_t)[:, :, None, :]
             + v_t[..., None] * k_t[:, :, None, :])
        return S, jnp.einsum('bhij,bhj->bhi', S, r_t)

    S_fin, ys = lax.scan(step, S0.astype(f32), seqs)
    y = jnp.moveaxis(ys, 0, 1)
    mean = jnp.mean(y, axis=-1, keepdims=True)
    var = jnp.mean(jnp.square(y - mean), axis=-1, keepdims=True)
    y = ((y - mean) * lax.rsqrt(var + GN_EPS)).reshape(B, T, D) * gn_w.astype(f32) + gn_b.astype(f32)
    bonus = jnp.sum(r_h * k_h * r_k.astype(f32), axis=-1, keepdims=True) * v_h
    y = (y + bonus.reshape(B, T, D)) * jax.nn.silu(g.astype(f32))
    return y.astype(h.dtype) @ w_out, S_fin.astype(S0.dtype), h[:, -1]


def shared_kv(x, c, pos, kv_ada_w, kv_ada_b, kv_norm, kv_w_a, kv_a_norm):
    shift, scale = jnp.split(c @ kv_ada_w + kv_ada_b, 2, axis=-1)
    h = rmsnorm(x, kv_norm) * (1.0 + scale[:, None, :]) + shift[:, None, :]
    kv = h @ kv_w_a
    c_kv = rmsnorm(kv[..., :KV_LORA], kv_a_norm)
    k_pe = rope(kv[..., None, KV_LORA:], pos)[:, :, 0]
    return c_kv, k_pe


def _attend_block(q_lat, q_pe, c_kv, k_pe, q_pos, k_pos):
    s = (jnp.einsum('bqhc,bkc->bhqk', q_lat, c_kv).astype(jnp.float32)
         + jnp.einsum('bqhr,bkr->bhqk', q_pe, k_pe).astype(jnp.float32)) * ATTN_SCALE
    s = jnp.where(k_pos[None, None, None, :] <= q_pos[None, None, :, None], s, -jnp.inf)
    p = jax.nn.softmax(s, axis=-1)
    return jnp.einsum('bhqk,bkc->bqhc', p.astype(c_kv.dtype), c_kv)


def mla_attend(q_lat, q_pe, c_kv, k_pe, q_pos, k_pos):
    B, T = q_lat.shape[0], q_lat.shape[1]
    if T <= Q_BLOCK or T % Q_BLOCK:
        return _attend_block(q_lat, q_pe, c_kv, k_pe, q_pos, k_pos)
    nb = T // Q_BLOCK
    split = lambda t: jnp.moveaxis(t.reshape(B, nb, Q_BLOCK, *t.shape[2:]), 1, 0)

    def blk(args):
        ql, qp, qpos = args
        return _attend_block(ql, qp, c_kv, k_pe, qpos, k_pos)

    out = lax.map(blk, (split(q_lat), split(q_pe), q_pos.reshape(nb, Q_BLOCK)))
    return jnp.moveaxis(out, 0, 1).reshape(B, T, MLA_H, KV_LORA)


def mla_mix(h, pos, c_kv, k_pe, k_pos, kv_w_b, w_in, q_norm, w_q, w_out):
    B, T, _ = h.shape
    qg = h @ w_in
    q_a, gate = qg[..., :Q_LORA], qg[..., Q_LORA:]
    q = jnp.einsum('btr,rhe->bthe', rmsnorm(q_a, q_norm), w_q)
    q_nope, q_pe = q[..., :QK_NOPE], rope(q[..., QK_NOPE:], pos)
    w_uk, w_uv = kv_w_b[..., :QK_NOPE], kv_w_b[..., QK_NOPE:]
    q_lat = jnp.einsum('bthn,chn->bthc', q_nope, w_uk)
    o_lat = mla_attend(q_lat, q_pe, c_kv, k_pe, pos, k_pos)
    o = jnp.einsum('bthc,chv->bthv', o_lat, w_uv).reshape(B, T, MLA_H * V_HEAD)
    return (o * jax.nn.silu(gate)) @ w_out


def trunk(x, c, pos, shift0, S0, past_ckv, past_kpe, layer_p, a_p, kv_p, kv_w_b, b_p):
    ada_w, ada_b, norm_pre, norm_post = layer_p
    new_S, new_shift = [], []
    c_kv = k_pe = keys = None
    for i in range(DEPTH):
        shift, scale, gate = jnp.split(c @ ada_w[i] + ada_b[i], 3, axis=-1)
        h = rmsnorm(x, norm_pre[i]) * (1.0 + scale[:, None, :]) + shift[:, None, :]
        if i < N_A:
            y, S_i, sh_i = rwkv7_time_mix(h, shift0[i], S0[i], *(p[i] for p in a_p))
            new_S.append(S_i)
            new_shift.append(sh_i)
        else:
            if keys is None:
                c_kv, k_pe = shared_kv(x, c, pos, *kv_p)
                if past_ckv is None:
                    keys = (c_kv, k_pe, pos)
                else:
                    k_pos = jnp.arange(past_ckv.shape[1] + x.shape[1], dtype=jnp.int32)
                    keys = (jnp.concatenate([past_ckv, c_kv], axis=1),
                            jnp.concatenate([past_kpe, k_pe], axis=1), k_pos)
            y = mla_mix(h, pos, keys[0], keys[1], keys[2], kv_w_b, *(p[i - N_A] for p in b_p))
        x = x + gate[:, None, :] * rmsnorm(y, norm_post[i])
    return x, jnp.stack(new_S), jnp.stack(new_shift), c_kv, k_pe


def setup_inputs(seed: int = 0) -> dict:
    key = jax.random.key(seed)
    ks = iter(jax.random.split(key, 64))

    def nrm(shape, scale=1.0):
        return jax.random.normal(next(ks), shape, jnp.float32) * scale

    D = D_MODEL
    n_pages = PAST_LEN // PAGE_SIZE
    n_used = DEC_BATCH * n_pages
    n_phys = n_used + n_used // 4
    page_table = jax.random.permutation(next(ks), n_phys)[:n_used].reshape(DEC_BATCH, n_pages).astype(jnp.int32)
    H, K = RWKV_H, RWKV_HEAD
    return {
        "x_prompt": nrm((BATCH, SEQ, D)),
        "x_sample": nrm((DEC_BATCH, DEC_SEQ, D)),
        "c_prompt": nrm((BATCH, D)),
        "c_sample": nrm((DEC_BATCH, D)),
        "state_wkv": nrm((N_A, DEC_BATCH, H, K, K), 0.3),
        "state_shift": nrm((N_A, DEC_BATCH, D)),
        "cache_kv_latent": nrm((n_phys, PAGE_SIZE, KV_LORA)),
        "cache_k_rope": nrm((n_phys, PAGE_SIZE, QK_ROPE)),
        "page_table": page_table,
        "ada_w": nrm((DEPTH, D, 3 * D), 0.5 * D ** -0.5),
        "ada_b": nrm((DEPTH, 3 * D), 0.02),
        "norm_pre": 1.0 + nrm((DEPTH, D), 0.02),
        "norm_post": 1.0 + nrm((DEPTH, D), 0.02),
        "a_mu": jax.random.uniform(next(ks), (N_A, N_MIX, D), jnp.float32),
        "a_w_in": nrm((N_A, 4, D, D), D ** -0.5),
        "a_w0": -2.0 + nrm((N_A, D), 0.5),
        "a_w1": nrm((N_A, D, DECAY_LORA), D ** -0.5),
        "a_w2": nrm((N_A, DECAY_LORA, D), 0.5 * DECAY_LORA ** -0.5),
        "a_a0": nrm((N_A, D), 0.3),
        "a_a1": nrm((N_A, D, AAA_LORA), D ** -0.5),
        "a_a2": nrm((N_A, AAA_LORA, D), 0.5 * AAA_LORA ** -0.5),
        "a_k_k": 0.85 + nrm((N_A, D), 0.02),
        "a_k_a": 1.0 + nrm((N_A, D), 0.02),
        "a_r_k": nrm((N_A, H, K), 0.1),
        "a_gn_w": 1.0 + nrm((N_A, D), 0.02),
        "a_gn_b": nrm((N_A, D), 0.02),
        "a_w_out": nrm((N_A, D, D), D ** -0.5),
        "kv_ada_w": nrm((D, 2 * D), 0.5 * D ** -0.5),
        "kv_ada_b": nrm((2 * D,), 0.02),
        "kv_norm": 1.0 + nrm((D,), 0.02),
        "kv_w_a": nrm((D, KV_LORA + QK_ROPE), D ** -0.5),
        "kv_a_norm": 1.0 + nrm((KV_LORA,), 0.02),
        "kv_w_b": nrm((KV_LORA, MLA_H, QK_NOPE + V_HEAD), KV_LORA ** -0.5),
        "b_w_in": nrm((N_B, D, Q_LORA + MLA_H * V_HEAD), D ** -0.5),
        "b_q_norm": 1.0 + nrm((N_B, Q_LORA), 0.02),
        "b_w_q": nrm((N_B, Q_LORA, MLA_H, QK_NOPE + QK_ROPE), Q_LORA ** -0.5),
        "b_w_out": nrm((N_B, MLA_H * V_HEAD, D), (MLA_H * V_HEAD) ** -0.5),
    }


def reference(x_prompt, x_sample, c_prompt, c_sample, state_wkv, state_shift, cache_kv_latent,
              cache_k_rope, page_table, ada_w, ada_b, norm_pre, norm_post, a_mu, a_w_in, a_w0,
              a_w1, a_w2, a_a0, a_a1, a_a2, a_k_k, a_k_a, a_r_k, a_gn_w, a_gn_b, a_w_out,
              kv_ada_w, kv_ada_b, kv_norm, kv_w_a, kv_a_norm, kv_w_b, b_w_in, b_q_norm, b_w_q,
              b_w_out):
    layer_p = (ada_w, ada_b, norm_pre, norm_post)
    a_p = (a_mu, a_w_in, a_w0, a_w1, a_w2, a_a0, a_a1, a_a2, a_k_k, a_k_a, a_r_k, a_gn_w, a_gn_b, a_w_out)
    kv_p = (kv_ada_w, kv_ada_b, kv_norm, kv_w_a, kv_a_norm)
    b_p = (b_w_in, b_q_norm, b_w_q, b_w_out)

    B, T, D = x_prompt.shape
    pos_p = jnp.arange(T, dtype=jnp.int32)
    shift0_p = jnp.zeros((N_A, B, D), x_prompt.dtype)
    S0_p = jnp.zeros((N_A, B, RWKV_H, RWKV_HEAD, RWKV_HEAD), x_prompt.dtype)
    y_prompt, wkv_p, shift_p, ckv_p, kpe_p = trunk(
        x_prompt, c_prompt, pos_p, shift0_p, S0_p, None, None, layer_p, a_p, kv_p, kv_w_b, b_p)

    DB, n_pages = page_table.shape
    past_len = n_pages * cache_kv_latent.shape[1]
    past_ckv = cache_kv_latent[page_table].reshape(DB, past_len, KV_LORA)
    past_kpe = cache_k_rope[page_table].reshape(DB, past_len, QK_ROPE)
    pos_s = past_len + jnp.arange(x_sample.shape[1], dtype=jnp.int32)
    y_sample, wkv_s, shift_s, ckv_s, kpe_s = trunk(
        x_sample, c_sample, pos_s, state_shift, state_wkv, past_ckv, past_kpe,
        layer_p, a_p, kv_p, kv_w_b, b_p)

    return (y_prompt, y_sample, wkv_p, shift_p, ckv_p, kpe_p, wkv_s, shift_s, ckv_s, kpe_s)
```

```python
import functools
import math

import jax
import jax.numpy as jnp
from jax import lax
from jax.experimental import pallas as pl
from jax.experimental.pallas import tpu as pltpu

F32 = jnp.float32
BF16 = jnp.bfloat16
HIGHEST = lax.Precision.HIGHEST

RWKV_HEAD = 64
GN_EPS = 64e-5
EPS = 1e-6
MLA_H = 8
QK_NOPE = 128
QK_ROPE = 64
V_HEAD = 128
Q_LORA = 384
KV_LORA = 256
ROPE_THETA = 10000.0
ATTN_SCALE = (QK_NOPE + QK_ROPE) ** -0.5
KCAT = KV_LORA + 128

V7X_LANES = 128
V7X_SUBLANES = 8
V7X_VMEM_BYTES = 64 * 1024 * 1024
V7X_VMEM_REQUEST_CAP = V7X_VMEM_BYTES - 8 * 1024 * 1024

WKV_CHUNK = 64
ROW_TILE = 256


def _vmem_limit(pipelined_bytes, resident_bytes=0):
    est = 2 * pipelined_bytes + resident_bytes + 4 * 1024 * 1024
    return int(min(max(est, 16 * 1024 * 1024), V7X_VMEM_REQUEST_CAP))


def _nbytes(shape, dtype):
    return math.prod(shape) * jnp.dtype(dtype).itemsize


def _params(sem, pipelined_bytes, resident_bytes=0):
    return pltpu.CompilerParams(
        dimension_semantics=sem,
        vmem_limit_bytes=_vmem_limit(pipelined_bytes, resident_bytes))


def _dot(a, b):
    return jnp.dot(a.astype(BF16), b.astype(BF16), preferred_element_type=F32)


def _dot_hi(a, b, dims=(((1,), (0,)), ((), ()))):
    return lax.dot_general(a, b, dims, precision=HIGHEST, preferred_element_type=F32)


_NT = (((1,), (1,)), ((), ()))
_TN = (((0,), (0,)), ((), ()))


def _sigmoid(x):
    return 1.0 / (1.0 + jnp.exp(-x))


def _ada_kernel(c_ref, w_ref, b_ref, o_ref):
    o_ref[0] = _dot_hi(c_ref[...], w_ref[0]) + b_ref[0]


def _ada_call(c, w, b):
    G, D, N = w.shape
    M = c.shape[0]
    tn = 1024
    pipelined = _nbytes((D, tn), F32) + _nbytes((M, tn), F32) + _nbytes((M, D), F32)
    return pl.pallas_call(
        _ada_kernel,
        grid=(G, N // tn),
        in_specs=[
            pl.BlockSpec((M, D), lambda g, j: (0, 0)),
            pl.BlockSpec((1, D, tn), lambda g, j: (g, 0, j)),
            pl.BlockSpec((1, 1, tn), lambda g, j: (g, 0, j)),
        ],
        out_specs=pl.BlockSpec((1, M, tn), lambda g, j: (g, 0, j)),
        out_shape=jax.ShapeDtypeStruct((G, M, N), F32),
        compiler_params=_params(("parallel", "parallel"), pipelined),
        name="ada_modulation",
    )(c, w, b.reshape(G, 1, N))


def _mod_spec(per_row, tm, D, col):
    if per_row:
        return pl.BlockSpec((1, tm, D), lambda b, t: (b, t, col))
    return pl.BlockSpec((1, 1, D), lambda b, t: (b, 0, col))


def _modnorm(x, gain, scale, shift):
    ms = jnp.mean(x * x, axis=-1, keepdims=True)
    return x * lax.rsqrt(ms + EPS) * gain * (1.0 + scale) + shift


def _rmsnorm(x, gain):
    ms = jnp.mean(x * x, axis=-1, keepdims=True)
    return x * lax.rsqrt(ms + EPS) * gain


def _rwkv_proj_kernel(x_ref, prev_ref, shift_ref, scale_ref, gain_ref, mu_ref, win_ref,
                      w0_ref, w1_ref, w2_ref, a0_ref, a1_ref, a2_ref,
                      r_ref, lw_ref, k_ref, v_ref, a_ref, g_ref, hlast_ref, *, seq_shift):
    x = x_ref[0]
    gain = gain_ref[...]
    scale = scale_ref[0]
    shift = shift_ref[0]
    h = _modnorm(x, gain, scale, shift)
    tm = h.shape[0]
    if seq_shift:
        hp = _modnorm(prev_ref[0][V7X_SUBLANES - 1:V7X_SUBLANES, :], gain, scale, shift)
        hp = jnp.where(pl.program_id(1) == 0, 0.0, hp)
        row = lax.broadcasted_iota(jnp.int32, (tm, 1), 0)
        hs = jnp.where(row == 0, hp, pltpu.roll(h, 1, axis=0))
        hlast_ref[0] = h[tm - 1:tm, :]
    else:
        hs = prev_ref[0]
        hlast_ref[0] = h
    xx = hs - h
    mu = mu_ref[...]
    outs = (r_ref, k_ref, v_ref, g_ref)
    for m in range(4):
        xm = h + xx * mu[m:m + 1, :]
        outs[m][0] = _dot(xm, win_ref[m])
    xw = h + xx * mu[4:5, :]
    xa = h + xx * mu[5:6, :]
    wl = w0_ref[...] + _dot_hi(jnp.tanh(_dot_hi(xw, w1_ref[...])), w2_ref[...])
    z = -wl
    softplus = jnp.maximum(z, 0.0) + jnp.log(1.0 + jnp.exp(-jnp.abs(z)))
    lw_ref[0] = -jnp.exp(-softplus - 0.5)
    al = a0_ref[...] + _dot_hi(_dot_hi(xa, a1_ref[...]), a2_ref[...])
    a_ref[0] = _sigmoid(al)


def _rwkv_proj_call(x, prev, mod, gain, mu, w_in, w0, w1, w2, a0, a1, a2, *, seq_shift, tm):
    B, T, D = x.shape
    per_row = not seq_shift
    nt = T // tm
    tile = pl.BlockSpec((1, tm, D), lambda b, t: (b, t, 0))
    if seq_shift:
        sub = tm // V7X_SUBLANES
        prev_spec = pl.BlockSpec((1, V7X_SUBLANES, D),
                                 lambda b, t: (b, jnp.maximum(t * sub - 1, 0), 0))
        hlast_shape = jax.ShapeDtypeStruct((B, 1, D), F32)
        hlast_spec = pl.BlockSpec((1, 1, D), lambda b, t: (b, 0, 0))
        sem = ("parallel", "arbitrary")
    else:
        prev_spec = tile
        hlast_shape = jax.ShapeDtypeStruct((B, T, D), F32)
        hlast_spec = tile
        sem = ("parallel", "parallel")
    full = lambda shape: pl.BlockSpec(shape, lambda b, t: (0,) * len(shape))
    lora = w1.shape[1]
    pipelined = (9 * _nbytes((tm, D), F32) + _nbytes((4, D, D), BF16)
                 + 4 * _nbytes((D, V7X_LANES), F32))
    resident = 8 * _nbytes((tm, D), F32)
    out_sds = jax.ShapeDtypeStruct((B, T, D), F32)
    return pl.pallas_call(
        functools.partial(_rwkv_proj_kernel, seq_shift=seq_shift),
        grid=(B, nt),
        in_specs=[tile, prev_spec, _mod_spec(per_row, tm, D, 0), _mod_spec(per_row, tm, D, 1),
                  full((1, D)), full((6, D)), full((4, D, D)),
                  full((1, D)), full((D, lora)), full((lora, D)),
                  full((1, D)), full((D, lora)), full((lora, D))],
        out_specs=[tile] * 6 + [hlast_spec],
        out_shape=[out_sds] * 6 + [hlast_shape],
        compiler_params=_params(sem, pipelined, resident),
        name="rwkv_proj",
    )(x, prev, mod, mod, gain, mu, w_in, w0, w1, w2, a0, a1, a2)


def _wkv_chunk_kernel(r_ref, lw_ref, k_ref, v_ref, a_ref, kk_ref, ka_ref, rk_ref, gw_ref, gb_ref,
                      y_ref, sfin_ref, s_scr, *, n_chunks, n_pairs):
    L = WKV_CHUNK
    L2 = 2 * L
    tc = pl.program_id(2)

    @pl.when(tc == 0)
    def _():
        s_scr[...] = jnp.zeros_like(s_scr)

    lane = lax.broadcasted_iota(jnp.int32, (L2, V7X_LANES), 1)
    srow = lax.broadcasted_iota(jnp.int32, (L2, V7X_LANES), 0)
    own = (srow < L) == (lane < RWKV_HEAD)
    ti = lax.broadcasted_iota(jnp.int32, (L, L), 0)
    tj = lax.broadcasted_iota(jnp.int32, (L, L), 1)
    cum = (ti >= tj).astype(F32)
    ri = lax.broadcasted_iota(jnp.int32, (L2, L2), 0)
    rj = lax.broadcasted_iota(jnp.int32, (L2, L2), 1)
    strict = ri > rj
    incl = ri >= rj
    n_rounds = int(math.log2(L))

    def stack(x):
        return jnp.where(own, jnp.concatenate([x, x], axis=0), 0.0)

    def fold(x2):
        return x2[:L] + x2[L:]

    def chunk(c, carry):
        t0 = pl.multiple_of(c * L, L)
        for p in range(n_pairs):
            cols = slice(p * V7X_LANES, (p + 1) * V7X_LANES)
            r = r_ref[0, pl.ds(t0, L), cols]
            lw = lw_ref[0, pl.ds(t0, L), cols]
            k = k_ref[0, pl.ds(t0, L), cols]
            v = v_ref[0, pl.ds(t0, L), cols]
            a = a_ref[0, pl.ds(t0, L), cols]
            k_k = kk_ref[:, cols]
            k_a = ka_ref[:, cols]
            r_k = rk_ref[:, cols]

            r2, v2, a2 = stack(r), stack(v), stack(a)
            kkr = stack(k * k_k)
            nrm = jnp.sqrt(jnp.sum(kkr * kkr, axis=-1, keepdims=True))
            kk2 = kkr / jnp.maximum(nrm, 1e-12)
            b2 = kk2 * a2
            km2 = stack(k * (1.0 + (a - 1.0) * k_a))

            cs = _dot_hi(cum, lw)
            cs_end = cs[L - 1:L, :]
            cs2 = jnp.concatenate([cs, cs], axis=0)
            lw2 = jnp.concatenate([lw, lw], axis=0)
            wt = kk2 * jnp.exp(cs2 - lw2)
            rt = r2 * jnp.exp(cs2)
            e_neg = jnp.exp(-cs2)
            bh = b2 * e_neg
            kh = km2 * e_neg
            e_end = jnp.exp(cs_end - cs2)

            wr = jnp.concatenate([wt, rt], axis=0)
            bk = jnp.concatenate([bh, kh], axis=0)
            aa = _dot_hi(wr, bk, _NT)
            s0 = s_scr[p]
            g = _dot_hi(wr, s0, _NT)
            a_wb = jnp.where(strict, aa[:L2, :L2], 0.0)
            a_wk = jnp.where(strict, aa[:L2, L2:], 0.0)
            a_rb = jnp.where(incl, aa[L2:, :L2], 0.0)
            a_rk = jnp.where(incl, aa[L2:, L2:], 0.0)

            x = -(g[:L2] + _dot_hi(a_wk, v2))
            m = -a_wb
            for it in range(n_rounds):
                if it + 1 < n_rounds:
                    mx = _dot_hi(m, jnp.concatenate([m, x], axis=1))
                    x = x + mx[:, L2:]
                    m = mx[:, :L2]
                else:
                    x = x + _dot_hi(m, x)
            uv = jnp.concatenate([x, v2], axis=0)
            y2 = g[L2:] + _dot_hi(jnp.concatenate([a_rb, a_rk], axis=1), uv)
            bkp = jnp.concatenate([b2 * e_end, km2 * e_end], axis=0)
            s_scr[p] = s0 * jnp.exp(cs_end) + _dot_hi(uv, bkp, _TN)

            mean = jnp.sum(y2, axis=-1, keepdims=True) * (1.0 / RWKV_HEAD)
            cen = jnp.where(own, y2 - mean, 0.0)
            var = jnp.sum(cen * cen, axis=-1, keepdims=True) * (1.0 / RWKV_HEAD)
            yn = fold(cen * lax.rsqrt(var + GN_EPS))
            bonus = fold(jnp.sum(r2 * km2 * r_k, axis=-1, keepdims=True) * v2)
            y_ref[0, pl.ds(t0, L), cols] = yn * gw_ref[:, cols] + gb_ref[:, cols] + bonus
        return carry

    lax.fori_loop(0, n_chunks, chunk, 0)

    @pl.when(tc == pl.num_programs(2) - 1)
    def _():
        sfin_ref[0] = s_scr[...]


def _wkv_chunk_call(r, lw, k, v, a, k_k, k_a, r_k, gn_w, gn_b, *, t_block, n_pairs):
    B, T, D = r.shape
    wcol = n_pairs * V7X_LANES
    n_col = D // wcol
    tile = pl.BlockSpec((1, t_block, wcol), lambda b, p, t: (b, t, p))
    vec = pl.BlockSpec((1, wcol), lambda b, p, t: (0, p))
    pipelined = 6 * _nbytes((t_block, wcol), F32) + _nbytes((n_pairs, 128, 128), F32)
    resident = _nbytes((n_pairs, 128, 128), F32) + 24 * _nbytes((256, 256), F32)
    return pl.pallas_call(
        functools.partial(_wkv_chunk_kernel, n_chunks=t_block // WKV_CHUNK, n_pairs=n_pairs),
        grid=(B, n_col, T // t_block),
        in_specs=[tile] * 5 + [vec] * 5,
        out_specs=[tile, pl.BlockSpec((1, n_pairs, V7X_LANES, V7X_LANES), lambda b, p, t: (b, p, 0, 0))],
        out_shape=[jax.ShapeDtypeStruct((B, T, D), F32),
                   jax.ShapeDtypeStruct((B, D // V7X_LANES, V7X_LANES, V7X_LANES), F32)],
        scratch_shapes=[pltpu.VMEM((n_pairs, V7X_LANES, V7X_LANES), F32)],
        compiler_params=_params(("parallel", "parallel", "arbitrary"), pipelined, resident),
        name="wkv_chunked",
    )(r, lw, k, v, a, k_k, k_a, r_k, gn_w, gn_b)


def _wkv_step_kernel(s_ref, r_ref, lw_ref, k_ref, v_ref, a_ref, kk_ref, ka_ref, rk_ref, gw_ref, gb_ref,
                     snew_ref, y_ref):
    K = RWKV_HEAD
    S = s_ref[0]
    r, lw, k, v, a = r_ref[0], lw_ref[0], k_ref[0], v_ref[0], a_ref[0]
    kkr = k * kk_ref[...]
    nrm = jnp.sqrt(jnp.sum(kkr * kkr, axis=-1, keepdims=True))
    kk = kkr / jnp.maximum(nrm, 1e-12)
    b = kk * a
    km = k * (1.0 + (a - 1.0) * ka_ref[...])
    w = jnp.exp(lw)
    eye = (lax.broadcasted_iota(jnp.int32, (K, K), 0)
           == lax.broadcasted_iota(jnp.int32, (K, K), 1)).astype(F32)
    row = lambda t: t[:, None, :]
    col = lambda t: jnp.sum(eye[None] * t[:, None, :], axis=-1, keepdims=True)
    sa = -jnp.sum(S * row(kk), axis=-1, keepdims=True)
    s_new = S * row(w) + sa * row(b) + col(v) * row(km)
    snew_ref[0] = s_new
    y_col = jnp.sum(s_new * row(r), axis=-1, keepdims=True)
    y = jnp.sum(y_col * eye[None], axis=1)
    mean = jnp.mean(y, axis=-1, keepdims=True)
    cen = y - mean
    var = jnp.mean(cen * cen, axis=-1, keepdims=True)
    yn = cen * lax.rsqrt(var + GN_EPS) * gw_ref[...] + gb_ref[...]
    bonus = jnp.sum(r * km * rk_ref[...], axis=-1, keepdims=True) * v
    y_ref[0] = yn + bonus


def _wkv_step_call(s0, r, lw, k, v, a, k_k, k_a, r_k, gn_w, gn_b):
    B, H, K, _ = s0.shape
    st = pl.BlockSpec((1, H, K, K), lambda b: (b, 0, 0, 0))
    vec = pl.BlockSpec((1, H, K), lambda b: (b, 0, 0))
    par = pl.BlockSpec((H, K), lambda b: (0, 0))
    pipelined = 2 * _nbytes((H, K, V7X_LANES), F32) + 6 * _nbytes((H, V7X_LANES), F32)
    resident = 6 * _nbytes((H, K, V7X_LANES), F32)
    return pl.pallas_call(
        _wkv_step_kernel,
        grid=(B,),
        in_specs=[st] + [vec] * 5 + [par] * 5,
        out_specs=[st, vec],
        out_shape=[jax.ShapeDtypeStruct(s0.shape, F32), jax.ShapeDtypeStruct((B, H, K), F32)],
        compiler_params=_params(("parallel",), pipelined, resident),
        name="wkv_step",
    )(s0, r, lw, k, v, a, k_k, k_a, r_k, gn_w, gn_b)


def _rwkv_out_kernel(y_ref, g_ref, x_ref, gate_ref, gain_ref, wout_ref, o_ref):
    g = g_ref[0]
    z = _dot(y_ref[0] * (g * _sigmoid(g)), wout_ref[...])
    o_ref[0] = x_ref[0] + gate_ref[0] * _rmsnorm(z, gain_ref[...])


def _rwkv_out_call(y, g, x, mod, gain, w_out, *, per_row, tm):
    B, T, D = x.shape
    tile = pl.BlockSpec((1, tm, D), lambda b, t: (b, t, 0))
    full = lambda shape: pl.BlockSpec(shape, lambda b, t: (0,) * len(shape))
    pipelined = 5 * _nbytes((tm, D), F32) + _nbytes((D, D), BF16)
    return pl.pallas_call(
        _rwkv_out_kernel,
        grid=(B, T // tm),
        in_specs=[tile, tile, tile, _mod_spec(per_row, tm, D, 2), full((1, D)), full((D, D))],
        out_specs=tile,
        out_shape=jax.ShapeDtypeStruct((B, T, D), F32),
        compiler_params=_params(("parallel", "parallel"), pipelined, 4 * _nbytes((tm, D), F32)),
        name="rwkv_out",
    )(y, g, x, mod, gain, w_out)


def _mla_proj_kernel(x_ref, shift_ref, scale_ref, kshift_ref, kscale_ref, gain_ref, kgain_ref,
                     cos_ref, sin_ref, wqa_ref, wgate_ref, wkv_ref, qnorm_ref, kvnorm_ref,
                     wqn_ref, wqp_ref, wqs_ref, wuk_ref,
                     q_ref, kcat_ref, ckv_ref, kpe_ref, sg_ref):
    x = x_ref[0]
    ms = jnp.mean(x * x, axis=-1, keepdims=True)
    xn = x * lax.rsqrt(ms + EPS)
    h = xn * gain_ref[...] * (1.0 + scale_ref[0]) + shift_ref[0]
    hk = xn * kgain_ref[...] * (1.0 + kscale_ref[0]) + kshift_ref[0]
    cos = cos_ref[...]
    sin = sin_ref[...]

    kv = _dot(hk, wkv_ref[...])
    ckv = _rmsnorm(kv[:, :KV_LORA], kvnorm_ref[...])
    kpe = kv[:, KV_LORA:KV_LORA + 128] * cos + kv[:, KV_LORA + 128:] * sin
    ckv_ref[0] = ckv
    kpe_ref[0] = kpe[:, :QK_ROPE]
    kcat_ref[0, :, :KV_LORA] = ckv.astype(BF16)
    kcat_ref[0, :, KV_LORA:] = kpe.astype(BF16)

    g = _dot(h, wgate_ref[...])
    sg_ref[0] = g * _sigmoid(g)

    qn = _rmsnorm(_dot(h, wqa_ref[...]), qnorm_ref[...]).astype(BF16)
    q_nope = _dot(qn, wqn_ref[...])
    q_pe = _dot(qn, wqp_ref[...])
    q_ps = _dot(qn, wqs_ref[...])
    for hd in range(MLA_H):
        cols = slice(hd * 128, (hd + 1) * 128)
        q_lat = _dot(q_nope[:, cols], wuk_ref[hd])
        q_ref[0, hd, :, :KV_LORA] = (q_lat * ATTN_SCALE).astype(BF16)
        q_ref[0, hd, :, KV_LORA:] = ((q_pe[:, cols] * cos + q_ps[:, cols] * sin) * ATTN_SCALE).astype(BF16)


def _mla_proj_call(x, mod, kvmod, gain, kgain, cos, sin, wqa, wgate, wkv, qnorm, kvnorm,
                   wqn, wqp, wqs, wuk, *, per_row, tm):
    B, T, D = x.shape
    tile = lambda w: pl.BlockSpec((1, tm, w), lambda b, t: (b, t, 0))
    full = lambda shape: pl.BlockSpec(shape, lambda b, t: (0,) * len(shape))
    tab = pl.BlockSpec((tm, 128), lambda b, t: (t, 0))
    weights = (wqa, wgate, wkv, wqn, wqp, wqs, wuk)
    pipelined = (sum(_nbytes(w.shape, BF16) for w in weights) + 3 * _nbytes((tm, D), F32)
                 + _nbytes((MLA_H, tm, KCAT), BF16) + 2 * _nbytes((tm, KCAT), F32))
    return pl.pallas_call(
        _mla_proj_kernel,
        grid=(B, T // tm),
        in_specs=[tile(D), _mod_spec(per_row, tm, D, 0), _mod_spec(per_row, tm, D, 1),
                  _mod_spec(per_row, tm, D, 0), _mod_spec(per_row, tm, D, 1),
                  full((1, D)), full((1, D)), tab, tab,
                  full(wqa.shape), full(wgate.shape), full(wkv.shape), full((1, Q_LORA)), full((1, KV_LORA)),
                  full(wqn.shape), full(wqp.shape), full(wqs.shape), full(wuk.shape)],
        out_specs=[pl.BlockSpec((1, MLA_H, tm, KCAT), lambda b, t: (b, 0, t, 0)),
                   tile(KCAT), tile(KV_LORA), tile(QK_ROPE), tile(MLA_H * V_HEAD)],
        out_shape=[jax.ShapeDtypeStruct((B, MLA_H, T, KCAT), BF16),
                   jax.ShapeDtypeStruct((B, T, KCAT), BF16),
                   jax.ShapeDtypeStruct((B, T, KV_LORA), F32),
                   jax.ShapeDtypeStruct((B, T, QK_ROPE), F32),
                   jax.ShapeDtypeStruct((B, T, MLA_H * V_HEAD), F32)],
        compiler_params=_params(("parallel", "parallel"), pipelined, 12 * _nbytes((tm, D), F32)),
        name="mla_proj",
    )(x, mod, mod, kvmod, kvmod, gain, kgain, cos, sin, wqa, wgate, wkv, qnorm, kvnorm,
      wqn, wqp, wqs, wuk)


def _flash_kernel(q_ref, k_ref, o_ref, m_scr, l_scr, acc_scr, *, tq):
    qi = pl.program_id(1)
    ki = pl.program_id(2)
    rows = MLA_H * tq

    @pl.when(ki == 0)
    def _():
        m_scr[...] = jnp.full_like(m_scr, -jnp.inf)
        l_scr[...] = jnp.zeros_like(l_scr)
        acc_scr[...] = jnp.zeros_like(acc_scr)

    def update(masked):
        q = q_ref[0].reshape(rows, KCAT)
        kc = k_ref[0]
        s = lax.dot_general(q, kc, _NT, preferred_element_type=F32)
        if masked:
            qpos = lax.broadcasted_iota(jnp.int32, (MLA_H, tq, tq), 1).reshape(rows, tq)
            kpos = lax.broadcasted_iota(jnp.int32, (rows, tq), 1)
            s = jnp.where(kpos <= qpos, s, -jnp.inf)
        m_prev = m_scr[...]
        m_new = jnp.maximum(m_prev, jnp.max(s, axis=-1, keepdims=True))
        alpha = jnp.exp(m_prev - m_new)
        p = jnp.exp(s - m_new)
        l_scr[...] = alpha * l_scr[...] + jnp.sum(p, axis=-1, keepdims=True)
        acc_scr[...] = alpha * acc_scr[...] + jnp.dot(p.astype(BF16), kc[:, :KV_LORA],
                                                      preferred_element_type=F32)
        m_scr[...] = m_new

    @pl.when(ki < qi)
    def _():
        update(False)

    @pl.when(ki == qi)
    def _():
        update(True)
        o = acc_scr[...] / l_scr[...]
        o_ref[0] = o.reshape(MLA_H, tq, KV_LORA).astype(BF16)


def _flash_call(q, kcat, *, tq):
    B, H, T, _ = q.shape
    nq = T // tq
    rows = H * tq
    pipelined = (_nbytes((H, tq, KCAT), BF16) + _nbytes((tq, KCAT), BF16)
                 + _nbytes((H, tq, KV_LORA), BF16))
    resident = _nbytes((rows, KV_LORA + 2 * V7X_LANES), F32) + 4 * _nbytes((rows, tq), F32)
    return pl.pallas_call(
        functools.partial(_flash_kernel, tq=tq),
        grid=(B, nq, nq),
        in_specs=[pl.BlockSpec((1, H, tq, KCAT), lambda b, i, j: (b, 0, i, 0)),
                  pl.BlockSpec((1, tq, KCAT), lambda b, i, j: (b, jnp.minimum(i, j), 0))],
        out_specs=pl.BlockSpec((1, H, tq, KV_LORA), lambda b, i, j: (b, 0, i, 0)),
        out_shape=jax.ShapeDtypeStruct((B, H, T, KV_LORA), BF16),
        scratch_shapes=[pltpu.VMEM((rows, 1), F32), pltpu.VMEM((rows, 1), F32),
                        pltpu.VMEM((rows, KV_LORA), F32)],
        compiler_params=_params(("parallel", "parallel", "arbitrary"), pipelined, resident),
        name="mla_flash",
    )(q, kcat)


DECODE_PAGES = 8


def _decode_kernel(pt_ref, q_ref, cnew_ref, pnew_ref, *refs):
    del pt_ref
    ck_refs = refs[:DECODE_PAGES]
    kp_refs = refs[DECODE_PAGES:2 * DECODE_PAGES]
    o_ref, m_scr, l_scr, acc_scr = refs[2 * DECODE_PAGES:]
    j = pl.program_id(1)
    q = q_ref[0]
    q_lat = q[:, :KV_LORA]
    q_pe = q[:, KV_LORA:KV_LORA + QK_ROPE]

    @pl.when(j == 0)
    def _():
        cn = cnew_ref[0]
        s_new = (jnp.sum(q_lat.astype(F32) * cn, axis=-1, keepdims=True)
                 + jnp.sum(q_pe.astype(F32) * pnew_ref[0], axis=-1, keepdims=True))
        m_scr[...] = s_new
        l_scr[...] = jnp.ones_like(l_scr)
        acc_scr[...] = jnp.broadcast_to(cn, acc_scr.shape)

    cks = [r[0].astype(BF16) for r in ck_refs]
    s = jnp.concatenate(
        [lax.dot_general(q_lat, ck, _NT, preferred_element_type=F32)
         + lax.dot_general(q_pe, kp[0].astype(BF16), _NT, preferred_element_type=F32)
         for ck, kp in zip(cks, kp_refs)], axis=1)
    m_prev = m_scr[...]
    m_new = jnp.maximum(m_prev, jnp.max(s, axis=-1, keepdims=True))
    alpha = jnp.exp(m_prev - m_new)
    p = jnp.exp(s - m_new)
    l_scr[...] = alpha * l_scr[...] + jnp.sum(p, axis=-1, keepdims=True)
    ps = cks[0].shape[0]
    pv = sum(jnp.dot(p[:, i * ps:(i + 1) * ps].astype(BF16), cks[i], preferred_element_type=F32)
             for i in range(DECODE_PAGES))
    acc_scr[...] = alpha * acc_scr[...] + pv
    m_scr[...] = m_new

    @pl.when(j == pl.num_programs(1) - 1)
    def _():
        o_ref[0] = acc_scr[...] / l_scr[...]


def _decode_call(page_table, q, c_new, p_new, cache_ckv, cache_kpe):
    B, H, _ = q.shape
    n_pages = page_table.shape[1]
    ps = cache_ckv.shape[1]
    steps = n_pages // DECODE_PAGES

    def page_spec(width, i):
        return pl.BlockSpec((1, ps, width), lambda b, j, pt: (pt[b, j * DECODE_PAGES + i], 0, 0))

    in_specs = ([pl.BlockSpec((1, H, KCAT), lambda b, j, pt: (b, 0, 0)),
                 pl.BlockSpec((1, 1, KV_LORA), lambda b, j, pt: (b, 0, 0)),
                 pl.BlockSpec((1, 1, QK_ROPE), lambda b, j, pt: (b, 0, 0))]
                + [page_spec(KV_LORA, i) for i in range(DECODE_PAGES)]
                + [page_spec(QK_ROPE, i) for i in range(DECODE_PAGES)])
    pipelined = DECODE_PAGES * (_nbytes((ps, KV_LORA), F32) + _nbytes((ps, V7X_LANES), F32))
    resident = DECODE_PAGES * _nbytes((ps, KV_LORA), F32)
    return pl.pallas_call(
        _decode_kernel,
        grid_spec=pltpu.PrefetchScalarGridSpec(
            num_scalar_prefetch=1,
            grid=(B, steps),
            in_specs=in_specs,
            out_specs=pl.BlockSpec((1, H, KV_LORA), lambda b, j, pt: (b, 0, 0)),
            scratch_shapes=[pltpu.VMEM((H, 1), F32), pltpu.VMEM((H, 1), F32),
                            pltpu.VMEM((H, KV_LORA), F32)]),
        out_shape=jax.ShapeDtypeStruct((B, H, KV_LORA), F32),
        compiler_params=_params(("parallel", "arbitrary"), pipelined, resident),
        name="mla_decode",
    )(page_table, q, c_new, p_new, *([cache_ckv] * DECODE_PAGES), *([cache_kpe] * DECODE_PAGES))


def _mla_out_kernel(o_ref, sg_ref, x_ref, gate_ref, gain_ref, wuv_ref, wout_ref, y_ref, og_scr):
    for hd in range(MLA_H):
        cols = slice(hd * V_HEAD, (hd + 1) * V_HEAD)
        o = _dot(o_ref[0, hd], wuv_ref[hd])
        og_scr[:, cols] = (o * sg_ref[0, :, cols]).astype(BF16)
    z = _dot(og_scr[...], wout_ref[...])
    y_ref[0] = x_ref[0] + gate_ref[0] * _rmsnorm(z, gain_ref[...])


def _mla_out_call(o_lat, sg, x, mod, gain, wuv, wout, *, per_row, tm):
    B, T, D = x.shape
    tile = lambda w: pl.BlockSpec((1, tm, w), lambda b, t: (b, t, 0))
    full = lambda shape: pl.BlockSpec(shape, lambda b, t: (0,) * len(shape))
    pipelined = (_nbytes((MLA_H, tm, KV_LORA), BF16) + 4 * _nbytes((tm, D), F32)
                 + _nbytes(wuv.shape, BF16) + _nbytes(wout.shape, BF16))
    return pl.pallas_call(
        _mla_out_kernel,
        grid=(B, T // tm),
        in_specs=[pl.BlockSpec((1, MLA_H, tm, KV_LORA), lambda b, t: (b, 0, t, 0)),
                  tile(MLA_H * V_HEAD), tile(D), _mod_spec(per_row, tm, D, 2), full((1, D)),
                  full(wuv.shape), full(wout.shape)],
        out_specs=tile(D),
        out_shape=jax.ShapeDtypeStruct((B, T, D), F32),
        scratch_shapes=[pltpu.VMEM((tm, MLA_H * V_HEAD), BF16)],
        compiler_params=_params(("parallel", "parallel"), pipelined, 4 * _nbytes((tm, D), F32)),
        name="mla_out",
    )(o_lat, sg, x, mod, gain, wuv, wout)


def _rope_tables(pos):
    half = QK_ROPE // 2
    inv = ROPE_THETA ** (-jnp.arange(half, dtype=F32) / half)
    ang = pos.astype(F32)[:, None] * inv[None, :]
    c, s = jnp.cos(ang), jnp.sin(ang)
    z = jnp.zeros((pos.shape[0], 128 - QK_ROPE), F32)
    return jnp.concatenate([c, c, z], axis=1), jnp.concatenate([-s, s, z], axis=1)


def _swap_halves(w):
    half = w.shape[-1] // 2
    return jnp.concatenate([w[..., half:], w[..., :half]], axis=-1)


def _pad_lanes(w):
    return jnp.concatenate([w, jnp.zeros(w.shape[:-1] + (128 - w.shape[-1],), w.dtype)], axis=-1)


def kernel(x_prompt, x_sample, c_prompt, c_sample, state_wkv, state_shift, cache_kv_latent, cache_k_rope, page_table, ada_w, ada_b, norm_pre, norm_post, a_mu, a_w_in, a_w0, a_w1, a_w2, a_a0, a_a1, a_a2, a_k_k, a_k_a, a_r_k, a_gn_w, a_gn_b, a_w_out, kv_ada_w, kv_ada_b, kv_norm, kv_w_a, kv_a_norm, kv_w_b, b_w_in, b_q_norm, b_w_q, b_w_out):
    B, T, D = x_prompt.shape
    DB = x_sample.shape[0]
    H = D // RWKV_HEAD
    assert ada_w.shape[0] == 2 and a_mu.shape[0] == 1 and b_w_in.shape[0] == 1
    assert x_sample.shape[1] == 1 and T % ROW_TILE == 0 and DB % V7X_SUBLANES == 0

    c_all = jnp.concatenate([c_prompt, c_sample], axis=0)
    mods = _ada_call(c_all, ada_w, ada_b)
    kvmods = _ada_call(c_all, kv_ada_w[None], kv_ada_b[None])
    mod_p = [mods[i, :B].reshape(B, 1, 3 * D) for i in range(2)]
    mod_s = [mods[i, B:].reshape(1, DB, 3 * D) for i in range(2)]
    kvmod_p = kvmods[0, :B].reshape(B, 1, 2 * D)
    kvmod_s = kvmods[0, B:].reshape(1, DB, 2 * D)

    row = lambda v: v.reshape(1, -1)
    w_in = a_w_in[0].astype(BF16)
    a_args = (row(norm_pre[0]), a_mu[0], w_in, row(a_w0[0]), a_w1[0], a_w2[0],
              row(a_a0[0]), a_a1[0], a_a2[0])
    w_out_a = a_w_out[0].astype(BF16)
    k_k, k_a, r_k = row(a_k_k[0]), row(a_k_a[0]), row(a_r_k[0])
    gn_w, gn_b = row(a_gn_w[0]), row(a_gn_b[0])

    w_bin = b_w_in[0]
    wqa = w_bin[:, :Q_LORA].astype(BF16)
    wgate = w_bin[:, Q_LORA:].astype(BF16)
    kv_pe = kv_w_a[:, KV_LORA:]
    wkv = jnp.concatenate([kv_w_a[:, :KV_LORA], _pad_lanes(kv_pe), _pad_lanes(_swap_halves(kv_pe))],
                          axis=1).astype(BF16)
    w_q = b_w_q[0]
    wqn = w_q[:, :, :QK_NOPE].reshape(Q_LORA, MLA_H * QK_NOPE).astype(BF16)
    wq_pe = w_q[:, :, QK_NOPE:]
    wqp = _pad_lanes(wq_pe).reshape(Q_LORA, MLA_H * 128).astype(BF16)
    wqs = _pad_lanes(_swap_halves(wq_pe)).reshape(Q_LORA, MLA_H * 128).astype(BF16)
    wuk = jnp.transpose(kv_w_b[:, :, :QK_NOPE], (1, 2, 0)).astype(BF16)
    wuv = jnp.transpose(kv_w_b[:, :, QK_NOPE:], (1, 0, 2)).astype(BF16)
    wout_b = b_w_out[0].astype(BF16)
    b_args = (wqa, wgate, wkv, row(b_q_norm[0]), row(kv_a_norm), wqn, wqp, wqs, wuk)

    tm = ROW_TILE
    r, lw, k, v, a, g, h_last = _rwkv_proj_call(x_prompt, x_prompt, mod_p[0], *a_args,
                                                seq_shift=True, tm=tm)
    yw, s_packed = _wkv_chunk_call(r, lw, k, v, a, k_k, k_a, r_k, gn_w, gn_b,
                                   t_block=min(T, 512), n_pairs=2)
    x1 = _rwkv_out_call(yw, g, x_prompt, mod_p[0], row(norm_post[0]), w_out_a, per_row=False, tm=tm)
    s_packed = s_packed.reshape(B, H // 2, 2, RWKV_HEAD, 2, RWKV_HEAD)
    wkv_p = jnp.stack([s_packed[:, :, 0, :, 0, :], s_packed[:, :, 1, :, 1, :]], axis=2)
    wkv_p = wkv_p.reshape(1, B, H, RWKV_HEAD, RWKV_HEAD)
    shift_p = h_last.reshape(1, B, D)

    cos_p, sin_p = _rope_tables(jnp.arange(T, dtype=jnp.int32))
    q_p, kcat_p, ckv_p, kpe_p, sg_p = _mla_proj_call(
        x1, mod_p[1], kvmod_p, row(norm_pre[1]), row(kv_norm), cos_p, sin_p, *b_args,
        per_row=False, tm=tm)
    o_p = _flash_call(q_p, kcat_p, tq=min(T, 256))
    y_prompt = _mla_out_call(o_p, sg_p, x1, mod_p[1], row(norm_post[1]), wuv, wout_b,
                             per_row=False, tm=tm)

    xs = x_sample.reshape(1, DB, D)
    rs, lws, ks, vs, as_, gs, hs = _rwkv_proj_call(xs, state_shift[0].reshape(1, DB, D), mod_s[0],
                                                   *a_args, seq_shift=False, tm=DB)
    hk = lambda t: t.reshape(DB, H, RWKV_HEAD)
    pk = lambda t: t.reshape(H, RWKV_HEAD)
    s_new, yws = _wkv_step_call(state_wkv[0], hk(rs), hk(lws), hk(ks), hk(vs), hk(as_),
                                pk(k_k), pk(k_a), pk(r_k), pk(gn_w), pk(gn_b))
    x1s = _rwkv_out_call(yws.reshape(1, DB, D), gs, xs, mod_s[0], row(norm_post[0]), w_out_a,
                         per_row=True, tm=DB)
    n_pages = page_table.shape[1]
    past_len = n_pages * cache_kv_latent.shape[1]
    cos_s, sin_s = _rope_tables(jnp.full((DB,), past_len, dtype=jnp.int32))
    q_s, _, ckv_s, kpe_s, sg_s = _mla_proj_call(
        x1s, mod_s[1], kvmod_s, row(norm_pre[1]), row(kv_norm), cos_s, sin_s, *b_args,
        per_row=True, tm=DB)
    o_s = _decode_call(page_table, jnp.transpose(q_s[0], (1, 0, 2)),
                       ckv_s.reshape(DB, 1, KV_LORA), kpe_s.reshape(DB, 1, QK_ROPE),
                       cache_kv_latent, cache_k_rope)
    o_s = jnp.transpose(o_s, (1, 0, 2)).astype(BF16)[None]
    y_s = _mla_out_call(o_s, sg_s, x1s, mod_s[1], row(norm_post[1]), wuv, wout_b,
                        per_row=True, tm=DB)

    return (y_prompt, y_s.reshape(DB, 1, D), wkv_p, shift_p, ckv_p, kpe_p,
            s_new[None], hs.reshape(1, DB, D), ckv_s.reshape(DB, 1, KV_LORA),
            kpe_s.reshape(DB, 1, QK_ROPE))
```

```python
import functools
import math

import jax
import jax.numpy as jnp
from jax import lax
from jax.experimental import pallas as pl
from jax.experimental.pallas import tpu as pltpu

F32 = jnp.float32
BF16 = jnp.bfloat16
HIGHEST = lax.Precision.HIGHEST

RWKV_HEAD = 64
GN_EPS = 64e-5
EPS = 1e-6
MLA_H = 8
QK_NOPE = 128
QK_ROPE = 64
V_HEAD = 128
Q_LORA = 384
KV_LORA = 256
ROPE_THETA = 10000.0
ATTN_SCALE = (QK_NOPE + QK_ROPE) ** -0.5
KCAT = KV_LORA + 128

V7X_LANES = 128
V7X_SUBLANES = 8
V7X_VMEM_BYTES = 64 * 1024 * 1024
V7X_VMEM_REQUEST_CAP = V7X_VMEM_BYTES - 8 * 1024 * 1024

WKV_CHUNK = 64
ROW_TILE = 256


def _vmem_limit(pipelined_bytes, resident_bytes=0):
    est = 2 * pipelined_bytes + resident_bytes + 4 * 1024 * 1024
    return int(min(max(est, 16 * 1024 * 1024), V7X_VMEM_REQUEST_CAP))


def _nbytes(shape, dtype):
    return math.prod(shape) * jnp.dtype(dtype).itemsize


def _params(sem, pipelined_bytes, resident_bytes=0):
    return pltpu.CompilerParams(
        dimension_semantics=sem,
        vmem_limit_bytes=_vmem_limit(pipelined_bytes, resident_bytes))


def _dot(a, b):
    return jnp.dot(a.astype(BF16), b.astype(BF16), preferred_element_type=F32)


def _dot_hi(a, b, dims=(((1,), (0,)), ((), ()))):
    return lax.dot_general(a, b, dims, precision=HIGHEST, preferred_element_type=F32)


_NT = (((1,), (1,)), ((), ()))
_TN = (((0,), (0,)), ((), ()))


def _sigmoid(x):
    return 1.0 / (1.0 + jnp.exp(-x))


def _ada_kernel(c_ref, w_ref, b_ref, o_ref):
    o_ref[0] = _dot_hi(c_ref[...], w_ref[0]) + b_ref[0]


def _ada_call(c, w, b):
    G, D, N = w.shape
    M = c.shape[0]
    tn = 1024
    pipelined = _nbytes((D, tn), F32) + _nbytes((M, tn), F32) + _nbytes((M, D), F32)
    return pl.pallas_call(
        _ada_kernel,
        grid=(G, N // tn),
        in_specs=[
            pl.BlockSpec((M, D), lambda g, j: (0, 0)),
            pl.BlockSpec((1, D, tn), lambda g, j: (g, 0, j)),
            pl.BlockSpec((1, 1, tn), lambda g, j: (g, 0, j)),
        ],
        out_specs=pl.BlockSpec((1, M, tn), lambda g, j: (g, 0, j)),
        out_shape=jax.ShapeDtypeStruct((G, M, N), F32),
        compiler_params=_params(("parallel", "parallel"), pipelined),
        name="ada_modulation",
    )(c, w, b.reshape(G, 1, N))


def _mod_spec(per_row, tm, D, col):
    if per_row:
        return pl.BlockSpec((1, tm, D), lambda b, t: (b, t, col))
    return pl.BlockSpec((1, 1, D), lambda b, t: (b, 0, col))


def _modnorm(x, gain, scale, shift):
    ms = jnp.mean(x * x, axis=-1, keepdims=True)
    return x * lax.rsqrt(ms + EPS) * gain * (1.0 + scale) + shift


def _rmsnorm(x, gain):
    ms = jnp.mean(x * x, axis=-1, keepdims=True)
    return x * lax.rsqrt(ms + EPS) * gain


def _rwkv_proj_kernel(x_ref, prev_ref, shift_ref, scale_ref, gain_ref, mu_ref, win_ref,
                      w0_ref, w1_ref, w2_ref, a0_ref, a1_ref, a2_ref,
                      r_ref, lw_ref, k_ref, v_ref, a_ref, g_ref, hlast_ref, *, seq_shift):
    x = x_ref[0]
    gain = gain_ref[...]
    scale = scale_ref[0]
    shift = shift_ref[0]
    h = _modnorm(x, gain, scale, shift)
    tm = h.shape[0]
    if seq_shift:
        hp = _modnorm(prev_ref[0][V7X_SUBLANES - 1:V7X_SUBLANES, :], gain, scale, shift)
        hp = jnp.where(pl.program_id(1) == 0, 0.0, hp)
        row = lax.broadcasted_iota(jnp.int32, (tm, 1), 0)
        hs = jnp.where(row == 0, hp, pltpu.roll(h, 1, axis=0))
        hlast_ref[0] = h[tm - 1:tm, :]
    else:
        hs = prev_ref[0]
        hlast_ref[0] = h
    xx = hs - h
    mu = mu_ref[...]
    outs = (r_ref, k_ref, v_ref, g_ref)
    for m in range(4):
        xm = h + xx * mu[m:m + 1, :]
        outs[m][0] = _dot(xm, win_ref[m])
    xw = h + xx * mu[4:5, :]
    xa = h + xx * mu[5:6, :]
    wl = w0_ref[...] + _dot_hi(jnp.tanh(_dot_hi(xw, w1_ref[...])), w2_ref[...])
    z = -wl
    softplus = jnp.maximum(z, 0.0) + jnp.log(1.0 + jnp.exp(-jnp.abs(z)))
    lw_ref[0] = -jnp.exp(-softplus - 0.5)
    al = a0_ref[...] + _dot_hi(_dot_hi(xa, a1_ref[...]), a2_ref[...])
    a_ref[0] = _sigmoid(al)


def _rwkv_proj_call(x, prev, mod, gain, mu, w_in, w0, w1, w2, a0, a1, a2, *, seq_shift, tm):
    B, T, D = x.shape
    per_row = not seq_shift
    nt = T // tm
    tile = pl.BlockSpec((1, tm, D), lambda b, t: (b, t, 0))
    if seq_shift:
        sub = tm // V7X_SUBLANES
        prev_spec = pl.BlockSpec((1, V7X_SUBLANES, D),
                                 lambda b, t: (b, jnp.maximum(t * sub - 1, 0), 0))
        hlast_shape = jax.ShapeDtypeStruct((B, 1, D), F32)
        hlast_spec = pl.BlockSpec((1, 1, D), lambda b, t: (b, 0, 0))
        sem = ("parallel", "arbitrary")
    else:
        prev_spec = tile
        hlast_shape = jax.ShapeDtypeStruct((B, T, D), F32)
        hlast_spec = tile
        sem = ("parallel", "parallel")
    full = lambda shape: pl.BlockSpec(shape, lambda b, t: (0,) * len(shape))
    lora = w1.shape[1]
    pipelined = (9 * _nbytes((tm, D), F32) + _nbytes((4, D, D), BF16)
                 + 4 * _nbytes((D, V7X_LANES), F32))
    resident = 8 * _nbytes((tm, D), F32)
    out_sds = jax.ShapeDtypeStruct((B, T, D), F32)
    return pl.pallas_call(
        functools.partial(_rwkv_proj_kernel, seq_shift=seq_shift),
        grid=(B, nt),
        in_specs=[tile, prev_spec, _mod_spec(per_row, tm, D, 0), _mod_spec(per_row, tm, D, 1),
                  full((1, D)), full((6, D)), full((4, D, D)),
                  full((1, D)), full((D, lora)), full((lora, D)),
                  full((1, D)), full((D, lora)), full((lora, D))],
        out_specs=[tile] * 6 + [hlast_spec],
        out_shape=[out_sds] * 6 + [hlast_shape],
        compiler_params=_params(sem, pipelined, resident),
        name="rwkv_proj",
    )(x, prev, mod, mod, gain, mu, w_in, w0, w1, w2, a0, a1, a2)


def _split(x):
    hi = x.astype(BF16)
    return hi, (x - hi.astype(F32)).astype(BF16)


def _dot_split(a, b, dims=(((1,), (0,)), ((), ()))):
    (ah, al), (bh, bl) = a, b
    dot = lambda x, y: lax.dot_general(x, y, dims, preferred_element_type=F32)
    return dot(ah, bh) + dot(ah, bl) + dot(al, bh)


def _cumsum_rows(cum, x):
    hi = x.astype(BF16)
    rest = x - hi.astype(F32)
    mid = rest.astype(BF16)
    lo = (rest - mid.astype(F32)).astype(BF16)
    dot = lambda t: jnp.dot(cum, t, preferred_element_type=F32)
    return dot(hi) + dot(mid) + dot(lo)


def _wkv_chunk_kernel(r_ref, lw_ref, k_ref, v_ref, a_ref, kk_ref, ka_ref, rk_ref, gw_ref, gb_ref,
                      y_ref, sfin_ref, s_scr, wr_scr, tinv_scr, av_scr, ar_scr, bkp_scr, v2_scr, pend_scr,
                      y2_scr, *, n_chunks, n_pairs):
    L = WKV_CHUNK
    L2 = 2 * L
    tc = pl.program_id(2)

    @pl.when(tc == 0)
    def _():
        s_scr[...] = jnp.zeros_like(s_scr)

    lane = lax.broadcasted_iota(jnp.int32, (L2, V7X_LANES), 1)
    srow = lax.broadcasted_iota(jnp.int32, (L2, V7X_LANES), 0)
    own = (srow < L) == (lane < RWKV_HEAD)
    ti = lax.broadcasted_iota(jnp.int32, (L, L), 0)
    tj = lax.broadcasted_iota(jnp.int32, (L, L), 1)
    cum = (ti >= tj).astype(BF16)
    ri = lax.broadcasted_iota(jnp.int32, (L2, L2), 0)
    rj = lax.broadcasted_iota(jnp.int32, (L2, L2), 1)
    strict = ri > rj
    incl = ri >= rj
    eye = (ri == rj).astype(F32)
    n_rounds = int(math.log2(L))

    def stack(x):
        return jnp.where(own, jnp.concatenate([x, x], axis=0), 0.0)

    def fold(x2):
        return x2[:L] + x2[L:]

    def precompute(c, carry):
        t0 = pl.multiple_of(c * L, L)
        pairs = range(n_pairs)
        cols = [slice(p * V7X_LANES, (p + 1) * V7X_LANES) for p in pairs]
        load = lambda ref: [ref[0, pl.ds(t0, L), cols[p]] for p in pairs]
        r, lw, k, v, a = load(r_ref), load(lw_ref), load(k_ref), load(v_ref), load(a_ref)

        r2 = [stack(r[p]) for p in pairs]
        v2 = [stack(v[p]) for p in pairs]
        a2 = [stack(a[p]) for p in pairs]
        kkr = [stack(k[p] * kk_ref[:, cols[p]]) for p in pairs]
        nrm = [jnp.sqrt(jnp.sum(kkr[p] * kkr[p], axis=-1, keepdims=True)) for p in pairs]
        kk2 = [kkr[p] / jnp.maximum(nrm[p], 1e-12) for p in pairs]
        b2 = [kk2[p] * a2[p] for p in pairs]
        km2 = [stack(k[p] * (1.0 + (a[p] - 1.0) * ka_ref[:, cols[p]])) for p in pairs]

        cs = [_cumsum_rows(cum, lw[p]) for p in pairs]
        cs_end = [cs[p][L - 1:L, :] for p in pairs]
        cs2 = [jnp.concatenate([cs[p], cs[p]], axis=0) for p in pairs]
        lw2 = [jnp.concatenate([lw[p], lw[p]], axis=0) for p in pairs]
        wr = [jnp.concatenate([kk2[p] * jnp.exp(cs2[p] - lw2[p]), r2[p] * jnp.exp(cs2[p])],
                              axis=0).astype(BF16) for p in pairs]
        e_neg = [jnp.exp(-cs2[p]) for p in pairs]
        bk = [jnp.concatenate([b2[p] * e_neg[p], km2[p] * e_neg[p]], axis=0).astype(BF16) for p in pairs]
        aa = [lax.dot_general(wr[p], bk[p], _NT, preferred_element_type=F32) for p in pairs]

        m0 = [jnp.where(strict, -aa[p][:L2, :L2], 0.0) for p in pairs]
        m = m0
        x = [eye for p in pairs]
        for it in range(n_rounds):
            if it + 1 < n_rounds:
                mx = [_dot(m[p], jnp.concatenate([m[p], x[p]], axis=1)) for p in pairs]
                x = [x[p] + mx[p][:, L2:] for p in pairs]
                m = [mx[p][:, :L2] for p in pairs]
            else:
                x = [x[p] + _dot(m[p], x[p]) for p in pairs]
        res = [eye - x[p] + _dot_split(_split(m0[p]), _split(x[p])) for p in pairs]
        x = [x[p] + _dot(x[p], res[p]) for p in pairs]
        av = [_dot(jnp.where(strict, aa[p][:L2, L2:], 0.0), v2[p]) for p in pairs]

        for p in pairs:
            e_end = jnp.exp(cs_end[p] - cs2[p])
            bkp_hi, bkp_lo = _split(jnp.concatenate([b2[p] * e_end, km2[p] * e_end], axis=0))
            v2_hi, v2_lo = _split(v2[p])
            wr_scr[c, p] = wr[p]
            tinv_scr[c, p] = x[p].astype(BF16)
            av_scr[c, p] = av[p]
            ar_scr[c, p] = jnp.where(jnp.concatenate([incl, incl], axis=1), aa[p][L2:, :], 0.0).astype(BF16)
            bkp_scr[c, p, 0] = bkp_hi
            bkp_scr[c, p, 1] = bkp_lo
            v2_scr[c, p, 0] = v2_hi
            v2_scr[c, p, 1] = v2_lo
            pend_scr[c, p] = jnp.broadcast_to(jnp.exp(cs_end[p]), (V7X_SUBLANES, V7X_LANES))
            bonus = fold(jnp.sum(r2[p] * km2[p] * rk_ref[:, cols[p]], axis=-1, keepdims=True) * v2[p])
            y_ref[0, pl.ds(t0, L), cols[p]] = gb_ref[:, cols[p]] + bonus
        return carry

    def recur(c, carry):
        pairs = range(n_pairs)
        s0 = [s_scr[p] for p in pairs]
        g = [lax.dot_general(wr_scr[c, p], s0[p].astype(BF16), _NT, preferred_element_type=F32)
             for p in pairs]
        u = [_dot(tinv_scr[c, p], -(g[p][:L2] + av_scr[c, p])) for p in pairs]
        us = [_split(u[p]) for p in pairs]
        uv = [tuple(jnp.concatenate([us[p][i], v2_scr[c, p, i]], axis=0) for i in range(2)) for p in pairs]
        for p in pairs:
            s_scr[p] = (s0[p] * pend_scr[c, p][0:1, :]
                        + _dot_split(uv[p], (bkp_scr[c, p, 0], bkp_scr[c, p, 1]), _TN))
        for p in pairs:
            y2_scr[c, p] = g[p][L2:] + jnp.dot(ar_scr[c, p], uv[p][0], preferred_element_type=F32)
        return carry

    def normalise(c, carry):
        t0 = pl.multiple_of(c * L, L)
        pairs = range(n_pairs)
        y2 = [y2_scr[c, p] for p in pairs]
        mean = [jnp.sum(y2[p], axis=-1, keepdims=True) * (1.0 / RWKV_HEAD) for p in pairs]
        cen = [jnp.where(own, y2[p] - mean[p], 0.0) for p in pairs]
        var = [jnp.sum(cen[p] * cen[p], axis=-1, keepdims=True) * (1.0 / RWKV_HEAD) for p in pairs]
        for p in pairs:
            cols = slice(p * V7X_LANES, (p + 1) * V7X_LANES)
            yn = fold(cen[p] * lax.rsqrt(var[p] + GN_EPS))
            y_ref[0, pl.ds(t0, L), cols] += yn * gw_ref[:, cols]
        return carry

    lax.fori_loop(0, n_chunks, precompute, 0)
    lax.fori_loop(0, n_chunks, recur, 0)
    lax.fori_loop(0, n_chunks, normalise, 0)

    @pl.when(tc == pl.num_programs(2) - 1)
    def _():
        sfin_ref[0] = s_scr[...]


def _wkv_chunk_call(r, lw, k, v, a, k_k, k_a, r_k, gn_w, gn_b, *, t_block, n_pairs):
    B, T, D = r.shape
    L2 = 2 * WKV_CHUNK
    n_chunks = t_block // WKV_CHUNK
    wcol = n_pairs * V7X_LANES
    n_col = D // wcol
    tile = pl.BlockSpec((1, t_block, wcol), lambda b, p, t: (b, t, p))
    vec = pl.BlockSpec((1, wcol), lambda b, p, t: (0, p))
    per_chunk = lambda shape, dt: ((n_chunks, n_pairs) + shape, dt)
    per_chunk_scratch = [
        per_chunk((2 * L2, V7X_LANES), BF16),
        per_chunk((L2, L2), BF16),
        per_chunk((L2, V7X_LANES), F32),
        per_chunk((L2, 2 * L2), BF16),
        per_chunk((2, 2 * L2, V7X_LANES), BF16),
        per_chunk((2, L2, V7X_LANES), BF16),
        per_chunk((V7X_SUBLANES, V7X_LANES), F32),
        per_chunk((L2, V7X_LANES), F32)]
    scratch = ([pltpu.VMEM((n_pairs, V7X_LANES, V7X_LANES), F32)]
               + [pltpu.VMEM(shape, dt) for shape, dt in per_chunk_scratch])
    scratch_bytes = sum(_nbytes(shape, dt) for shape, dt in per_chunk_scratch)
    pipelined = 6 * _nbytes((t_block, wcol), F32) + _nbytes((n_pairs, 128, 128), F32)
    resident = scratch_bytes + _nbytes((n_pairs, 128, 128), F32) + 24 * _nbytes((256, 256), F32)
    return pl.pallas_call(
        functools.partial(_wkv_chunk_kernel, n_chunks=n_chunks, n_pairs=n_pairs),
        grid=(B, n_col, T // t_block),
        in_specs=[tile] * 5 + [vec] * 5,
        out_specs=[tile, pl.BlockSpec((1, n_pairs, V7X_LANES, V7X_LANES), lambda b, p, t: (b, p, 0, 0))],
        out_shape=[jax.ShapeDtypeStruct((B, T, D), F32),
                   jax.ShapeDtypeStruct((B, D // V7X_LANES, V7X_LANES, V7X_LANES), F32)],
        scratch_shapes=scratch,
        compiler_params=_params(("parallel", "parallel", "arbitrary"), pipelined, resident),
        name="wkv_chunked",
    )(r, lw, k, v, a, k_k, k_a, r_k, gn_w, gn_b)


def _wkv_step_kernel(s_ref, r_ref, lw_ref, k_ref, v_ref, a_ref, kk_ref, ka_ref, rk_ref, gw_ref, gb_ref,
                     snew_ref, y_ref):
    K = RWKV_HEAD
    S = s_ref[0]
    r, lw, k, v, a = r_ref[0], lw_ref[0], k_ref[0], v_ref[0], a_ref[0]
    kkr = k * kk_ref[...]
    nrm = jnp.sqrt(jnp.sum(kkr * kkr, axis=-1, keepdims=True))
    kk = kkr / jnp.maximum(nrm, 1e-12)
    b = kk * a
    km = k * (1.0 + (a - 1.0) * ka_ref[...])
    w = jnp.exp(lw)
    eye = (lax.broadcasted_iota(jnp.int32, (K, K), 0)
           == lax.broadcasted_iota(jnp.int32, (K, K), 1)).astype(F32)
    row = lambda t: t[:, None, :]
    col = lambda t: jnp.sum(eye[None] * t[:, None, :], axis=-1, keepdims=True)
    sa = -jnp.sum(S * row(kk), axis=-1, keepdims=True)
    s_new = S * row(w) + sa * row(b) + col(v) * row(km)
    snew_ref[0] = s_new
    y_col = jnp.sum(s_new * row(r), axis=-1, keepdims=True)
    y = jnp.sum(y_col * eye[None], axis=1)
    mean = jnp.mean(y, axis=-1, keepdims=True)
    cen = y - mean
    var = jnp.mean(cen * cen, axis=-1, keepdims=True)
    yn = cen * lax.rsqrt(var + GN_EPS) * gw_ref[...] + gb_ref[...]
    bonus = jnp.sum(r * km * rk_ref[...], axis=-1, keepdims=True) * v
    y_ref[0] = yn + bonus


def _wkv_step_call(s0, r, lw, k, v, a, k_k, k_a, r_k, gn_w, gn_b):
    B, H, K, _ = s0.shape
    st = pl.BlockSpec((1, H, K, K), lambda b: (b, 0, 0, 0))
    vec = pl.BlockSpec((1, H, K), lambda b: (b, 0, 0))
    par = pl.BlockSpec((H, K), lambda b: (0, 0))
    pipelined = 2 * _nbytes((H, K, V7X_LANES), F32) + 6 * _nbytes((H, V7X_LANES), F32)
    resident = 6 * _nbytes((H, K, V7X_LANES), F32)
    return pl.pallas_call(
        _wkv_step_kernel,
        grid=(B,),
        in_specs=[st] + [vec] * 5 + [par] * 5,
        out_specs=[st, vec],
        out_shape=[jax.ShapeDtypeStruct(s0.shape, F32), jax.ShapeDtypeStruct((B, H, K), F32)],
        compiler_params=_params(("parallel",), pipelined, resident),
        name="wkv_step",
    )(s0, r, lw, k, v, a, k_k, k_a, r_k, gn_w, gn_b)


def _rwkv_out_kernel(y_ref, g_ref, x_ref, gate_ref, gain_ref, wout_ref, o_ref):
    g = g_ref[0]
    z = _dot(y_ref[0] * (g * _sigmoid(g)), wout_ref[...])
    o_ref[0] = x_ref[0] + gate_ref[0] * _rmsnorm(z, gain_ref[...])


def _rwkv_out_call(y, g, x, mod, gain, w_out, *, per_row, tm):
    B, T, D = x.shape
    tile = pl.BlockSpec((1, tm, D), lambda b, t: (b, t, 0))
    full = lambda shape: pl.BlockSpec(shape, lambda b, t: (0,) * len(shape))
    pipelined = 5 * _nbytes((tm, D), F32) + _nbytes((D, D), BF16)
    return pl.pallas_call(
        _rwkv_out_kernel,
        grid=(B, T // tm),
        in_specs=[tile, tile, tile, _mod_spec(per_row, tm, D, 2), full((1, D)), full((D, D))],
        out_specs=tile,
        out_shape=jax.ShapeDtypeStruct((B, T, D), F32),
        compiler_params=_params(("parallel", "parallel"), pipelined, 4 * _nbytes((tm, D), F32)),
        name="rwkv_out",
    )(y, g, x, mod, gain, w_out)


def _mla_proj_kernel(x_ref, shift_ref, scale_ref, kshift_ref, kscale_ref, gain_ref, kgain_ref,
                     cos_ref, sin_ref, wqa_ref, wgate_ref, wkv_ref, qnorm_ref, kvnorm_ref,
                     wqn_ref, wqp_ref, wqs_ref, wuk_ref,
                     q_ref, kcat_ref, ckv_ref, kpe_ref, sg_ref):
    x = x_ref[0]
    ms = jnp.mean(x * x, axis=-1, keepdims=True)
    xn = x * lax.rsqrt(ms + EPS)
    h = xn * gain_ref[...] * (1.0 + scale_ref[0]) + shift_ref[0]
    hk = xn * kgain_ref[...] * (1.0 + kscale_ref[0]) + kshift_ref[0]
    cos = cos_ref[...]
    sin = sin_ref[...]

    kv = _dot(hk, wkv_ref[...])
    ckv = _rmsnorm(kv[:, :KV_LORA], kvnorm_ref[...])
    kpe = kv[:, KV_LORA:KV_LORA + 128] * cos + kv[:, KV_LORA + 128:] * sin
    ckv_ref[0] = ckv
    kpe_ref[0] = kpe[:, :QK_ROPE]
    kcat_ref[0, :, :KV_LORA] = ckv.astype(BF16)
    kcat_ref[0, :, KV_LORA:] = kpe.astype(BF16)

    g = _dot(h, wgate_ref[...])
    sg_ref[0] = g * _sigmoid(g)

    qn = _rmsnorm(_dot(h, wqa_ref[...]), qnorm_ref[...]).astype(BF16)
    q_nope = _dot(qn, wqn_ref[...])
    q_pe = _dot(qn, wqp_ref[...])
    q_ps = _dot(qn, wqs_ref[...])
    for hd in range(MLA_H):
        cols = slice(hd * 128, (hd + 1) * 128)
        q_lat = _dot(q_nope[:, cols], wuk_ref[hd])
        q_ref[0, hd, :, :KV_LORA] = (q_lat * ATTN_SCALE).astype(BF16)
        q_ref[0, hd, :, KV_LORA:] = ((q_pe[:, cols] * cos + q_ps[:, cols] * sin) * ATTN_SCALE).astype(BF16)


def _mla_proj_call(x, mod, kvmod, gain, kgain, cos, sin, wqa, wgate, wkv, qnorm, kvnorm,
                   wqn, wqp, wqs, wuk, *, per_row, tm):
    B, T, D = x.shape
    tile = lambda w: pl.BlockSpec((1, tm, w), lambda b, t: (b, t, 0))
    full = lambda shape: pl.BlockSpec(shape, lambda b, t: (0,) * len(shape))
    tab = pl.BlockSpec((tm, 128), lambda b, t: (t, 0))
    weights = (wqa, wgate, wkv, wqn, wqp, wqs, wuk)
    pipelined = (sum(_nbytes(w.shape, BF16) for w in weights) + 3 * _nbytes((tm, D), F32)
                 + _nbytes((MLA_H, tm, KCAT), BF16) + 2 * _nbytes((tm, KCAT), F32))
    return pl.pallas_call(
        _mla_proj_kernel,
        grid=(B, T // tm),
        in_specs=[tile(D), _mod_spec(per_row, tm, D, 0), _mod_spec(per_row, tm, D, 1),
                  _mod_spec(per_row, tm, D, 0), _mod_spec(per_row, tm, D, 1),
                  full((1, D)), full((1, D)), tab, tab,
                  full(wqa.shape), full(wgate.shape), full(wkv.shape), full((1, Q_LORA)), full((1, KV_LORA)),
                  full(wqn.shape), full(wqp.shape), full(wqs.shape), full(wuk.shape)],
        out_specs=[pl.BlockSpec((1, MLA_H, tm, KCAT), lambda b, t: (b, 0, t, 0)),
                   tile(KCAT), tile(KV_LORA), tile(QK_ROPE), tile(MLA_H * V_HEAD)],
        out_shape=[jax.ShapeDtypeStruct((B, MLA_H, T, KCAT), BF16),
                   jax.ShapeDtypeStruct((B, T, KCAT), BF16),
                   jax.ShapeDtypeStruct((B, T, KV_LORA), F32),
                   jax.ShapeDtypeStruct((B, T, QK_ROPE), F32),
                   jax.ShapeDtypeStruct((B, T, MLA_H * V_HEAD), F32)],
        compiler_params=_params(("parallel", "parallel"), pipelined, 12 * _nbytes((tm, D), F32)),
        name="mla_proj",
    )(x, mod, mod, kvmod, kvmod, gain, kgain, cos, sin, wqa, wgate, wkv, qnorm, kvnorm,
      wqn, wqp, wqs, wuk)


def _flash_kernel(q_ref, k_ref, o_ref, m_scr, l_scr, acc_scr, *, tq):
    qi = pl.program_id(1)
    ki = pl.program_id(2)
    rows = MLA_H * tq

    @pl.when(ki == 0)
    def _():
        m_scr[...] = jnp.full_like(m_scr, -jnp.inf)
        l_scr[...] = jnp.zeros_like(l_scr)
        acc_scr[...] = jnp.zeros_like(acc_scr)

    def update(masked):
        q = q_ref[0].reshape(rows, KCAT)
        kc = k_ref[0]
        s = lax.dot_general(q, kc, _NT, preferred_element_type=F32)
        if masked:
            qpos = lax.broadcasted_iota(jnp.int32, (MLA_H, tq, tq), 1).reshape(rows, tq)
            kpos = lax.broadcasted_iota(jnp.int32, (rows, tq), 1)
            s = jnp.where(kpos <= qpos, s, -jnp.inf)
        m_prev = m_scr[...]
        m_new = jnp.maximum(m_prev, jnp.max(s, axis=-1, keepdims=True))
        alpha = jnp.exp(m_prev - m_new)
        p = jnp.exp(s - m_new)
        l_scr[...] = alpha * l_scr[...] + jnp.sum(p, axis=-1, keepdims=True)
        acc_scr[...] = alpha * acc_scr[...] + jnp.dot(p.astype(BF16), kc[:, :KV_LORA],
                                                      preferred_element_type=F32)
        m_scr[...] = m_new

    @pl.when(ki < qi)
    def _():
        update(False)

    @pl.when(ki == qi)
    def _():
        update(True)
        o = acc_scr[...] / l_scr[...]
        o_ref[0] = o.reshape(MLA_H, tq, KV_LORA).astype(BF16)


def _flash_call(q, kcat, *, tq):
    B, H, T, _ = q.shape
    nq = T // tq
    rows = H * tq
    pipelined = (_nbytes((H, tq, KCAT), BF16) + _nbytes((tq, KCAT), BF16)
                 + _nbytes((H, tq, KV_LORA), BF16))
    resident = _nbytes((rows, KV_LORA + 2 * V7X_LANES), F32) + 4 * _nbytes((rows, tq), F32)
    return pl.pallas_call(
        functools.partial(_flash_kernel, tq=tq),
        grid=(B, nq, nq),
        in_specs=[pl.BlockSpec((1, H, tq, KCAT), lambda b, i, j: (b, 0, i, 0)),
                  pl.BlockSpec((1, tq, KCAT), lambda b, i, j: (b, jnp.minimum(i, j), 0))],
        out_specs=pl.BlockSpec((1, H, tq, KV_LORA), lambda b, i, j: (b, 0, i, 0)),
        out_shape=jax.ShapeDtypeStruct((B, H, T, KV_LORA), BF16),
        scratch_shapes=[pltpu.VMEM((rows, 1), F32), pltpu.VMEM((rows, 1), F32),
                        pltpu.VMEM((rows, KV_LORA), F32)],
        compiler_params=_params(("parallel", "parallel", "arbitrary"), pipelined, resident),
        name="mla_flash",
    )(q, kcat)


DECODE_PAGES = 8


def _decode_kernel(pt_ref, q_ref, cnew_ref, pnew_ref, *refs):
    del pt_ref
    ck_refs = refs[:DECODE_PAGES]
    kp_refs = refs[DECODE_PAGES:2 * DECODE_PAGES]
    o_ref, m_scr, l_scr, acc_scr = refs[2 * DECODE_PAGES:]
    j = pl.program_id(1)
    q = q_ref[0]
    q_lat = q[:, :KV_LORA]
    q_pe = q[:, KV_LORA:KV_LORA + QK_ROPE]

    @pl.when(j == 0)
    def _():
        cn = cnew_ref[0]
        s_new = (jnp.sum(q_lat.astype(F32) * cn, axis=-1, keepdims=True)
                 + jnp.sum(q_pe.astype(F32) * pnew_ref[0], axis=-1, keepdims=True))
        m_scr[...] = s_new
        l_scr[...] = jnp.ones_like(l_scr)
        acc_scr[...] = jnp.broadcast_to(cn, acc_scr.shape)

    cks = [r[0].astype(BF16) for r in ck_refs]
    s = jnp.concatenate(
        [lax.dot_general(q_lat, ck, _NT, preferred_element_type=F32)
         + lax.dot_general(q_pe, kp[0].astype(BF16), _NT, preferred_element_type=F32)
         for ck, kp in zip(cks, kp_refs)], axis=1)
    m_prev = m_scr[...]
    m_new = jnp.maximum(m_prev, jnp.max(s, axis=-1, keepdims=True))
    alpha = jnp.exp(m_prev - m_new)
    p = jnp.exp(s - m_new)
    l_scr[...] = alpha * l_scr[...] + jnp.sum(p, axis=-1, keepdims=True)
    ps = cks[0].shape[0]
    pv = sum(jnp.dot(p[:, i * ps:(i + 1) * ps].astype(BF16), cks[i], preferred_element_type=F32)
             for i in range(DECODE_PAGES))
    acc_scr[...] = alpha * acc_scr[...] + pv
    m_scr[...] = m_new

    @pl.when(j == pl.num_programs(1) - 1)
    def _():
        o_ref[0] = acc_scr[...] / l_scr[...]


def _decode_call(page_table, q, c_new, p_new, cache_ckv, cache_kpe):
    B, H, _ = q.shape
    n_pages = page_table.shape[1]
    ps = cache_ckv.shape[1]
    steps = n_pages // DECODE_PAGES

    def page_spec(width, i):
        return pl.BlockSpec((1, ps, width), lambda b, j, pt: (pt[b, j * DECODE_PAGES + i], 0, 0))

    in_specs = ([pl.BlockSpec((1, H, KCAT), lambda b, j, pt: (b, 0, 0)),
                 pl.BlockSpec((1, 1, KV_LORA), lambda b, j, pt: (b, 0, 0)),
                 pl.BlockSpec((1, 1, QK_ROPE), lambda b, j, pt: (b, 0, 0))]
                + [page_spec(KV_LORA, i) for i in range(DECODE_PAGES)]
                + [page_spec(QK_ROPE, i) for i in range(DECODE_PAGES)])
    pipelined = DECODE_PAGES * (_nbytes((ps, KV_LORA), F32) + _nbytes((ps, V7X_LANES), F32))
    resident = DECODE_PAGES * _nbytes((ps, KV_LORA), F32)
    return pl.pallas_call(
        _decode_kernel,
        grid_spec=pltpu.PrefetchScalarGridSpec(
            num_scalar_prefetch=1,
            grid=(B, steps),
            in_specs=in_specs,
            out_specs=pl.BlockSpec((1, H, KV_LORA), lambda b, j, pt: (b, 0, 0)),
            scratch_shapes=[pltpu.VMEM((H, 1), F32), pltpu.VMEM((H, 1), F32),
                            pltpu.VMEM((H, KV_LORA), F32)]),
        out_shape=jax.ShapeDtypeStruct((B, H, KV_LORA), F32),
        compiler_params=_params(("parallel", "arbitrary"), pipelined, resident),
        name="mla_decode",
    )(page_table, q, c_new, p_new, *([cache_ckv] * DECODE_PAGES), *([cache_kpe] * DECODE_PAGES))


def _mla_out_kernel(o_ref, sg_ref, x_ref, gate_ref, gain_ref, wuv_ref, wout_ref, y_ref, og_scr):
    for hd in range(MLA_H):
        cols = slice(hd * V_HEAD, (hd + 1) * V_HEAD)
        o = _dot(o_ref[0, hd], wuv_ref[hd])
        og_scr[:, cols] = (o * sg_ref[0, :, cols]).astype(BF16)
    z = _dot(og_scr[...], wout_ref[...])
    y_ref[0] = x_ref[0] + gate_ref[0] * _rmsnorm(z, gain_ref[...])


def _mla_out_call(o_lat, sg, x, mod, gain, wuv, wout, *, per_row, tm):
    B, T, D = x.shape
    tile = lambda w: pl.BlockSpec((1, tm, w), lambda b, t: (b, t, 0))
    full = lambda shape: pl.BlockSpec(shape, lambda b, t: (0,) * len(shape))
    pipelined = (_nbytes((MLA_H, tm, KV_LORA), BF16) + 4 * _nbytes((tm, D), F32)
                 + _nbytes(wuv.shape, BF16) + _nbytes(wout.shape, BF16))
    return pl.pallas_call(
        _mla_out_kernel,
        grid=(B, T // tm),
        in_specs=[pl.BlockSpec((1, MLA_H, tm, KV_LORA), lambda b, t: (b, 0, t, 0)),
                  tile(MLA_H * V_HEAD), tile(D), _mod_spec(per_row, tm, D, 2), full((1, D)),
                  full(wuv.shape), full(wout.shape)],
        out_specs=tile(D),
        out_shape=jax.ShapeDtypeStruct((B, T, D), F32),
        scratch_shapes=[pltpu.VMEM((tm, MLA_H * V_HEAD), BF16)],
        compiler_params=_params(("parallel", "parallel"), pipelined, 4 * _nbytes((tm, D), F32)),
        name="mla_out",
    )(o_lat, sg, x, mod, gain, wuv, wout)


def _rope_tables(pos):
    half = QK_ROPE // 2
    inv = ROPE_THETA ** (-jnp.arange(half, dtype=F32) / half)
    ang = pos.astype(F32)[:, None] * inv[None, :]
    c, s = jnp.cos(ang), jnp.sin(ang)
    z = jnp.zeros((pos.shape[0], 128 - QK_ROPE), F32)
    return jnp.concatenate([c, c, z], axis=1), jnp.concatenate([-s, s, z], axis=1)


def _swap_halves(w):
    half = w.shape[-1] // 2
    return jnp.concatenate([w[..., half:], w[..., :half]], axis=-1)


def _pad_lanes(w):
    return jnp.concatenate([w, jnp.zeros(w.shape[:-1] + (128 - w.shape[-1],), w.dtype)], axis=-1)


def kernel(x_prompt, x_sample, c_prompt, c_sample, state_wkv, state_shift, cache_kv_latent, cache_k_rope, page_table, ada_w, ada_b, norm_pre, norm_post, a_mu, a_w_in, a_w0, a_w1, a_w2, a_a0, a_a1, a_a2, a_k_k, a_k_a, a_r_k, a_gn_w, a_gn_b, a_w_out, kv_ada_w, kv_ada_b, kv_norm, kv_w_a, kv_a_norm, kv_w_b, b_w_in, b_q_norm, b_w_q, b_w_out):
    B, T, D = x_prompt.shape
    DB = x_sample.shape[0]
    H = D // RWKV_HEAD
    assert ada_w.shape[0] == 2 and a_mu.shape[0] == 1 and b_w_in.shape[0] == 1
    assert x_sample.shape[1] == 1 and T % ROW_TILE == 0 and DB % V7X_SUBLANES == 0

    c_all = jnp.concatenate([c_prompt, c_sample], axis=0)
    mods = _ada_call(c_all, ada_w, ada_b)
    kvmods = _ada_call(c_all, kv_ada_w[None], kv_ada_b[None])
    mod_p = [mods[i, :B].reshape(B, 1, 3 * D) for i in range(2)]
    mod_s = [mods[i, B:].reshape(1, DB, 3 * D) for i in range(2)]
    kvmod_p = kvmods[0, :B].reshape(B, 1, 2 * D)
    kvmod_s = kvmods[0, B:].reshape(1, DB, 2 * D)

    row = lambda v: v.reshape(1, -1)
    w_in = a_w_in[0].astype(BF16)
    a_args = (row(norm_pre[0]), a_mu[0], w_in, row(a_w0[0]), a_w1[0], a_w2[0],
              row(a_a0[0]), a_a1[0], a_a2[0])
    w_out_a = a_w_out[0].astype(BF16)
    k_k, k_a, r_k = row(a_k_k[0]), row(a_k_a[0]), row(a_r_k[0])
    gn_w, gn_b = row(a_gn_w[0]), row(a_gn_b[0])

    w_bin = b_w_in[0]
    wqa = w_bin[:, :Q_LORA].astype(BF16)
    wgate = w_bin[:, Q_LORA:].astype(BF16)
    kv_pe = kv_w_a[:, KV_LORA:]
    wkv = jnp.concatenate([kv_w_a[:, :KV_LORA], _pad_lanes(kv_pe), _pad_lanes(_swap_halves(kv_pe))],
                          axis=1).astype(BF16)
    w_q = b_w_q[0]
    wqn = w_q[:, :, :QK_NOPE].reshape(Q_LORA, MLA_H * QK_NOPE).astype(BF16)
    wq_pe = w_q[:, :, QK_NOPE:]
    wqp = _pad_lanes(wq_pe).reshape(Q_LORA, MLA_H * 128).astype(BF16)
    wqs = _pad_lanes(_swap_halves(wq_pe)).reshape(Q_LORA, MLA_H * 128).astype(BF16)
    wuk = jnp.transpose(kv_w_b[:, :, :QK_NOPE], (1, 2, 0)).astype(BF16)
    wuv = jnp.transpose(kv_w_b[:, :, QK_NOPE:], (1, 0, 2)).astype(BF16)
    wout_b = b_w_out[0].astype(BF16)
    b_args = (wqa, wgate, wkv, row(b_q_norm[0]), row(kv_a_norm), wqn, wqp, wqs, wuk)

    tm = ROW_TILE
    r, lw, k, v, a, g, h_last = _rwkv_proj_call(x_prompt, x_prompt, mod_p[0], *a_args,
                                                seq_shift=True, tm=tm)
    yw, s_packed = _wkv_chunk_call(r, lw, k, v, a, k_k, k_a, r_k, gn_w, gn_b,
                                   t_block=min(T, 512), n_pairs=4)
    x1 = _rwkv_out_call(yw, g, x_prompt, mod_p[0], row(norm_post[0]), w_out_a, per_row=False, tm=tm)
    s_packed = s_packed.reshape(B, H // 2, 2, RWKV_HEAD, 2, RWKV_HEAD)
    wkv_p = jnp.stack([s_packed[:, :, 0, :, 0, :], s_packed[:, :, 1, :, 1, :]], axis=2)
    wkv_p = wkv_p.reshape(1, B, H, RWKV_HEAD, RWKV_HEAD)
    shift_p = h_last.reshape(1, B, D)

    cos_p, sin_p = _rope_tables(jnp.arange(T, dtype=jnp.int32))
    q_p, kcat_p, ckv_p, kpe_p, sg_p = _mla_proj_call(
        x1, mod_p[1], kvmod_p, row(norm_pre[1]), row(kv_norm), cos_p, sin_p, *b_args,
        per_row=False, tm=tm)
    o_p = _flash_call(q_p, kcat_p, tq=min(T, 256))
    y_prompt = _mla_out_call(o_p, sg_p, x1, mod_p[1], row(norm_post[1]), wuv, wout_b,
                             per_row=False, tm=tm)

    xs = x_sample.reshape(1, DB, D)
    rs, lws, ks, vs, as_, gs, hs = _rwkv_proj_call(xs, state_shift[0].reshape(1, DB, D), mod_s[0],
                                                   *a_args, seq_shift=False, tm=DB)
    hk = lambda t: t.reshape(DB, H, RWKV_HEAD)
    pk = lambda t: t.reshape(H, RWKV_HEAD)
    s_new, yws = _wkv_step_call(state_wkv[0], hk(rs), hk(lws), hk(ks), hk(vs), hk(as_),
                                pk(k_k), pk(k_a), pk(r_k), pk(gn_w), pk(gn_b))
    x1s = _rwkv_out_call(yws.reshape(1, DB, D), gs, xs, mod_s[0], row(norm_post[0]), w_out_a,
                         per_row=True, tm=DB)
    n_pages = page_table.shape[1]
    past_len = n_pages * cache_kv_latent.shape[1]
    cos_s, sin_s = _rope_tables(jnp.full((DB,), past_len, dtype=jnp.int32))
    q_s, _, ckv_s, kpe_s, sg_s = _mla_proj_call(
        x1s, mod_s[1], kvmod_s, row(norm_pre[1]), row(kv_norm), cos_s, sin_s, *b_args,
        per_row=True, tm=DB)
    o_s = _decode_call(page_table, jnp.transpose(q_s[0], (1, 0, 2)),
                       ckv_s.reshape(DB, 1, KV_LORA), kpe_s.reshape(DB, 1, QK_ROPE),
                       cache_kv_latent, cache_k_rope)
    o_s = jnp.transpose(o_s, (1, 0, 2)).astype(BF16)[None]
    y_s = _mla_out_call(o_s, sg_s, x1s, mod_s[1], row(norm_post[1]), wuv, wout_b,
                        per_row=True, tm=DB)

    return (y_prompt, y_s.reshape(DB, 1, D), wkv_p, shift_p, ckv_p, kpe_p,
            s_new[None], hs.reshape(1, DB, D), ckv_s.reshape(DB, 1, KV_LORA),
            kpe_s.reshape(DB, 1, QK_ROPE))
```

```python
import functools
import math

import jax
import jax.numpy as jnp
from jax import lax
from jax.experimental import pallas as pl
from jax.experimental.pallas import tpu as pltpu

F32 = jnp.float32
BF16 = jnp.bfloat16
HIGHEST = lax.Precision.HIGHEST

RWKV_HEAD = 64
GN_EPS = 64e-5
EPS = 1e-6
MLA_H = 8
QK_NOPE = 128
QK_ROPE = 64
V_HEAD = 128
Q_LORA = 384
KV_LORA = 256
ROPE_THETA = 10000.0
ATTN_SCALE = (QK_NOPE + QK_ROPE) ** -0.5
Q_SCALE = ATTN_SCALE * math.log2(math.e)
KCAT = KV_LORA + 128

V7X_LANES = 128
V7X_SUBLANES = 8
V7X_VMEM_BYTES = 64 * 1024 * 1024
V7X_VMEM_REQUEST_CAP = V7X_VMEM_BYTES - 8 * 1024 * 1024

WKV_CHUNK = 64
ROW_TILE = 256


def _vmem_limit(pipelined_bytes, resident_bytes=0):
    est = 2 * pipelined_bytes + resident_bytes + 4 * 1024 * 1024
    return int(min(max(est, 16 * 1024 * 1024), V7X_VMEM_REQUEST_CAP))


def _nbytes(shape, dtype):
    return math.prod(shape) * jnp.dtype(dtype).itemsize


def _params(sem, pipelined_bytes, resident_bytes=0):
    return pltpu.CompilerParams(
        dimension_semantics=sem,
        vmem_limit_bytes=_vmem_limit(pipelined_bytes, resident_bytes))


def _dot(a, b):
    return jnp.dot(a.astype(BF16), b.astype(BF16), preferred_element_type=F32)


def _dot_hi(a, b, dims=(((1,), (0,)), ((), ()))):
    return lax.dot_general(a, b, dims, precision=HIGHEST, preferred_element_type=F32)


_NT = (((1,), (1,)), ((), ()))
_TN = (((0,), (0,)), ((), ()))


def _sigmoid(x):
    return 1.0 / (1.0 + jnp.exp(-x))


def _ada_kernel(c_ref, w_ref, b_ref, o_ref):
    o_ref[0] = _dot_hi(c_ref[...], w_ref[0]) + b_ref[0]


def _ada_call(c, w, b):
    G, D, N = w.shape
    M = c.shape[0]
    tn = 1024
    pipelined = _nbytes((D, tn), F32) + _nbytes((M, tn), F32) + _nbytes((M, D), F32)
    return pl.pallas_call(
        _ada_kernel,
        grid=(G, N // tn),
        in_specs=[
            pl.BlockSpec((M, D), lambda g, j: (0, 0)),
            pl.BlockSpec((1, D, tn), lambda g, j: (g, 0, j)),
            pl.BlockSpec((1, 1, tn), lambda g, j: (g, 0, j)),
        ],
        out_specs=pl.BlockSpec((1, M, tn), lambda g, j: (g, 0, j)),
        out_shape=jax.ShapeDtypeStruct((G, M, N), F32),
        compiler_params=_params(("parallel", "parallel"), pipelined),
        name="ada_modulation",
    )(c, w, b.reshape(G, 1, N))


def _mod_spec(per_row, tm, D, col):
    if per_row:
        return pl.BlockSpec((1, tm, D), lambda b, t: (b, t, col))
    return pl.BlockSpec((1, 1, D), lambda b, t: (b, 0, col))


def _modnorm(x, gain, scale, shift):
    ms = jnp.mean(x * x, axis=-1, keepdims=True)
    return x * lax.rsqrt(ms + EPS) * gain * (1.0 + scale) + shift


def _rmsnorm(x, gain):
    ms = jnp.mean(x * x, axis=-1, keepdims=True)
    return x * lax.rsqrt(ms + EPS) * gain


def _rwkv_proj_kernel(x_ref, prev_ref, shift_ref, scale_ref, gain_ref, mu_ref, win_ref,
                      w0_ref, w1_ref, w2_ref, a0_ref, a1_ref, a2_ref,
                      r_ref, lw_ref, k_ref, v_ref, a_ref, g_ref, hlast_ref, *, seq_shift):
    x = x_ref[0]
    gain = gain_ref[...]
    scale = scale_ref[0]
    shift = shift_ref[0]
    h = _modnorm(x, gain, scale, shift)
    tm = h.shape[0]
    if seq_shift:
        hp = _modnorm(prev_ref[0][V7X_SUBLANES - 1:V7X_SUBLANES, :], gain, scale, shift)
        hp = jnp.where(pl.program_id(1) == 0, 0.0, hp)
        row = lax.broadcasted_iota(jnp.int32, (tm, 1), 0)
        hs = jnp.where(row == 0, hp, pltpu.roll(h, 1, axis=0))
        hlast_ref[0] = h[tm - 1:tm, :]
    else:
        hs = prev_ref[0]
        hlast_ref[0] = h
    xx = hs - h
    mu = mu_ref[...]
    outs = (r_ref, k_ref, v_ref, g_ref)
    for m in range(4):
        xm = h + xx * mu[m:m + 1, :]
        outs[m][0] = _dot(xm, win_ref[m])
    xw = h + xx * mu[4:5, :]
    xa = h + xx * mu[5:6, :]
    wl = w0_ref[...] + _dot(jnp.tanh(_dot(xw, w1_ref[...])), w2_ref[...])
    z = -wl
    softplus = jnp.maximum(z, 0.0) + jnp.log(1.0 + jnp.exp(-jnp.abs(z)))
    lw_ref[0] = -jnp.exp(-softplus - 0.5)
    al = a0_ref[...] + _dot(_dot(xa, a1_ref[...]), a2_ref[...])
    a_ref[0] = _sigmoid(al)


def _rwkv_proj_call(x, prev, mod, gain, mu, w_in, w0, w1, w2, a0, a1, a2, *, seq_shift, tm):
    B, T, D = x.shape
    per_row = not seq_shift
    nt = T // tm
    tile = pl.BlockSpec((1, tm, D), lambda b, t: (b, t, 0))
    if seq_shift:
        sub = tm // V7X_SUBLANES
        prev_spec = pl.BlockSpec((1, V7X_SUBLANES, D),
                                 lambda b, t: (b, jnp.maximum(t * sub - 1, 0), 0))
        hlast_shape = jax.ShapeDtypeStruct((B, 1, D), F32)
        hlast_spec = pl.BlockSpec((1, 1, D), lambda b, t: (b, 0, 0))
        sem = ("parallel", "arbitrary")
    else:
        prev_spec = tile
        hlast_shape = jax.ShapeDtypeStruct((B, T, D), F32)
        hlast_spec = tile
        sem = ("parallel", "parallel")
    full = lambda shape: pl.BlockSpec(shape, lambda b, t: (0,) * len(shape))
    lora = w1.shape[1]
    pipelined = (9 * _nbytes((tm, D), F32) + _nbytes((4, D, D), BF16)
                 + 4 * _nbytes((D, V7X_LANES), BF16))
    resident = 8 * _nbytes((tm, D), F32)
    out_sds = jax.ShapeDtypeStruct((B, T, D), F32)
    return pl.pallas_call(
        functools.partial(_rwkv_proj_kernel, seq_shift=seq_shift),
        grid=(B, nt),
        in_specs=[tile, prev_spec, _mod_spec(per_row, tm, D, 0), _mod_spec(per_row, tm, D, 1),
                  full((1, D)), full((6, D)), full((4, D, D)),
                  full((1, D)), full((D, lora)), full((lora, D)),
                  full((1, D)), full((D, lora)), full((lora, D))],
        out_specs=[tile] * 6 + [hlast_spec],
        out_shape=[out_sds] * 6 + [hlast_shape],
        compiler_params=_params(sem, pipelined, resident),
        name="rwkv_proj",
    )(x, prev, mod, mod, gain, mu, w_in, w0, w1, w2, a0, a1, a2)


def _split(x):
    hi = x.astype(BF16)
    return hi, (x - hi.astype(F32)).astype(BF16)


def _dot_split(a, b, dims=(((1,), (0,)), ((), ()))):
    (ah, al), (bh, bl) = a, b
    dot = lambda x, y: lax.dot_general(x, y, dims, preferred_element_type=F32)
    return dot(ah, bh) + dot(ah, bl) + dot(al, bh)


def _cumsum_rows(cum, x):
    hi = x.astype(BF16)
    rest = x - hi.astype(F32)
    mid = rest.astype(BF16)
    lo = (rest - mid.astype(F32)).astype(BF16)
    dot = lambda t: jnp.dot(cum, t, preferred_element_type=F32)
    return dot(hi) + dot(mid) + dot(lo)


def _wkv_chunk_kernel(r_ref, lw_ref, k_ref, v_ref, a_ref, kk_ref, ka_ref, rk_ref, gw_ref, gb_ref,
                      y_ref, sfin_ref, s_scr, wr_scr, tinv_scr, av_scr, ar_scr, bkp_scr, v2_scr, pend_scr,
                      y2_scr, *, n_chunks, n_pairs):
    L = WKV_CHUNK
    L2 = 2 * L
    tc = pl.program_id(2)

    @pl.when(tc == 0)
    def _():
        s_scr[...] = jnp.zeros_like(s_scr)

    lane = lax.broadcasted_iota(jnp.int32, (L2, V7X_LANES), 1)
    srow = lax.broadcasted_iota(jnp.int32, (L2, V7X_LANES), 0)
    own = (srow < L) == (lane < RWKV_HEAD)
    ti = lax.broadcasted_iota(jnp.int32, (L, L), 0)
    tj = lax.broadcasted_iota(jnp.int32, (L, L), 1)
    cum = (ti >= tj).astype(BF16)
    ri = lax.broadcasted_iota(jnp.int32, (L2, L2), 0)
    rj = lax.broadcasted_iota(jnp.int32, (L2, L2), 1)
    strict = ri > rj
    incl = ri >= rj
    eye = (ri == rj).astype(F32)
    n_rounds = int(math.log2(L))

    def stack(x):
        return jnp.where(own, jnp.concatenate([x, x], axis=0), 0.0)

    def fold(x2):
        return x2[:L] + x2[L:]

    def precompute(c, carry):
        t0 = pl.multiple_of(c * L, L)
        pairs = range(n_pairs)
        cols = [slice(p * V7X_LANES, (p + 1) * V7X_LANES) for p in pairs]
        load = lambda ref: [ref[0, pl.ds(t0, L), cols[p]] for p in pairs]
        r, lw, k, v, a = load(r_ref), load(lw_ref), load(k_ref), load(v_ref), load(a_ref)

        r2 = [stack(r[p]) for p in pairs]
        v2 = [stack(v[p]) for p in pairs]
        a2 = [stack(a[p]) for p in pairs]
        kkr = [stack(k[p] * kk_ref[:, cols[p]]) for p in pairs]
        nrm = [jnp.sqrt(jnp.sum(kkr[p] * kkr[p], axis=-1, keepdims=True)) for p in pairs]
        kk2 = [kkr[p] / jnp.maximum(nrm[p], 1e-12) for p in pairs]
        b2 = [kk2[p] * a2[p] for p in pairs]
        km2 = [stack(k[p] * (1.0 + (a[p] - 1.0) * ka_ref[:, cols[p]])) for p in pairs]

        cs = [_cumsum_rows(cum, lw[p]) for p in pairs]
        cs_end = [cs[p][L - 1:L, :] for p in pairs]
        cs2 = [jnp.concatenate([cs[p], cs[p]], axis=0) for p in pairs]
        lw2 = [jnp.concatenate([lw[p], lw[p]], axis=0) for p in pairs]
        wr = [jnp.concatenate([kk2[p] * jnp.exp(cs2[p] - lw2[p]), r2[p] * jnp.exp(cs2[p])],
                              axis=0).astype(BF16) for p in pairs]
        e_neg = [jnp.exp(-cs2[p]) for p in pairs]
        bk = [jnp.concatenate([b2[p] * e_neg[p], km2[p] * e_neg[p]], axis=0).astype(BF16) for p in pairs]
        aa = [lax.dot_general(wr[p], bk[p], _NT, preferred_element_type=F32) for p in pairs]

        m0 = [jnp.where(strict, -aa[p][:L2, :L2], 0.0) for p in pairs]
        m = m0
        x = [eye for p in pairs]
        for it in range(n_rounds):
            if it + 1 < n_rounds:
                mx = [_dot(m[p], jnp.concatenate([m[p], x[p]], axis=1)) for p in pairs]
                x = [x[p] + mx[p][:, L2:] for p in pairs]
                m = [mx[p][:, :L2] for p in pairs]
            else:
                x = [x[p] + _dot(m[p], x[p]) for p in pairs]
        res = [eye - x[p] + _dot_split(_split(m0[p]), _split(x[p])) for p in pairs]
        x = [x[p] + _dot(x[p], res[p]) for p in pairs]
        av = [_dot(jnp.where(strict, aa[p][:L2, L2:], 0.0), v2[p]) for p in pairs]

        for p in pairs:
            e_end = jnp.exp(cs_end[p] - cs2[p])
            bkp_hi, bkp_lo = _split(jnp.concatenate([b2[p] * e_end, km2[p] * e_end], axis=0))
            v2_hi, v2_lo = _split(v2[p])
            wr_scr[c, p] = wr[p]
            tinv_scr[c, p] = x[p].astype(BF16)
            av_scr[c, p] = av[p]
            ar_scr[c, p] = jnp.where(jnp.concatenate([incl, incl], axis=1), aa[p][L2:, :], 0.0).astype(BF16)
            bkp_scr[c, p, 0] = bkp_hi
            bkp_scr[c, p, 1] = bkp_lo
            v2_scr[c, p, 0] = v2_hi
            v2_scr[c, p, 1] = v2_lo
            pend_scr[c, p] = jnp.broadcast_to(jnp.exp(cs_end[p]), (V7X_SUBLANES, V7X_LANES))
            bonus = fold(jnp.sum(r2[p] * km2[p] * rk_ref[:, cols[p]], axis=-1, keepdims=True) * v2[p])
            y_ref[0, pl.ds(t0, L), cols[p]] = gb_ref[:, cols[p]] + bonus
        return carry

    def recur(c, carry):
        pairs = range(n_pairs)
        s0 = [s_scr[p] for p in pairs]
        g = [lax.dot_general(wr_scr[c, p], s0[p].astype(BF16), _NT, preferred_element_type=F32)
             for p in pairs]
        u = [_dot(tinv_scr[c, p], -(g[p][:L2] + av_scr[c, p])) for p in pairs]
        us = [_split(u[p]) for p in pairs]
        uv = [tuple(jnp.concatenate([us[p][i], v2_scr[c, p, i]], axis=0) for i in range(2)) for p in pairs]
        for p in pairs:
            s_scr[p] = (s0[p] * pend_scr[c, p][0:1, :]
                        + _dot_split(uv[p], (bkp_scr[c, p, 0], bkp_scr[c, p, 1]), _TN))
        for p in pairs:
            y2_scr[c, p] = g[p][L2:] + jnp.dot(ar_scr[c, p], uv[p][0], preferred_element_type=F32)
        return carry

    def normalise(c, carry):
        t0 = pl.multiple_of(c * L, L)
        pairs = range(n_pairs)
        y2 = [y2_scr[c, p] for p in pairs]
        mean = [jnp.sum(y2[p], axis=-1, keepdims=True) * (1.0 / RWKV_HEAD) for p in pairs]
        cen = [jnp.where(own, y2[p] - mean[p], 0.0) for p in pairs]
        var = [jnp.sum(cen[p] * cen[p], axis=-1, keepdims=True) * (1.0 / RWKV_HEAD) for p in pairs]
        for p in pairs:
            cols = slice(p * V7X_LANES, (p + 1) * V7X_LANES)
            yn = fold(cen[p] * lax.rsqrt(var[p] + GN_EPS))
            y_ref[0, pl.ds(t0, L), cols] += yn * gw_ref[:, cols]
        return carry

    lax.fori_loop(0, n_chunks, precompute, 0)
    lax.fori_loop(0, n_chunks, recur, 0)
    lax.fori_loop(0, n_chunks, normalise, 0)

    @pl.when(tc == pl.num_programs(2) - 1)
    def _():
        sfin_ref[0] = s_scr[...]


def _wkv_chunk_call(r, lw, k, v, a, k_k, k_a, r_k, gn_w, gn_b, *, t_block, n_pairs):
    B, T, D = r.shape
    L2 = 2 * WKV_CHUNK
    n_chunks = t_block // WKV_CHUNK
    wcol = n_pairs * V7X_LANES
    n_col = D // wcol
    tile = pl.BlockSpec((1, t_block, wcol), lambda b, p, t: (b, t, p))
    vec = pl.BlockSpec((1, wcol), lambda b, p, t: (0, p))
    per_chunk = lambda shape, dt: ((n_chunks, n_pairs) + shape, dt)
    per_chunk_scratch = [
        per_chunk((2 * L2, V7X_LANES), BF16),
        per_chunk((L2, L2), BF16),
        per_chunk((L2, V7X_LANES), F32),
        per_chunk((L2, 2 * L2), BF16),
        per_chunk((2, 2 * L2, V7X_LANES), BF16),
        per_chunk((2, L2, V7X_LANES), BF16),
        per_chunk((V7X_SUBLANES, V7X_LANES), F32),
        per_chunk((L2, V7X_LANES), F32)]
    scratch = ([pltpu.VMEM((n_pairs, V7X_LANES, V7X_LANES), F32)]
               + [pltpu.VMEM(shape, dt) for shape, dt in per_chunk_scratch])
    scratch_bytes = sum(_nbytes(shape, dt) for shape, dt in per_chunk_scratch)
    pipelined = 6 * _nbytes((t_block, wcol), F32) + _nbytes((n_pairs, 128, 128), F32)
    resident = scratch_bytes + _nbytes((n_pairs, 128, 128), F32) + 24 * _nbytes((256, 256), F32)
    return pl.pallas_call(
        functools.partial(_wkv_chunk_kernel, n_chunks=n_chunks, n_pairs=n_pairs),
        grid=(B, n_col, T // t_block),
        in_specs=[tile] * 5 + [vec] * 5,
        out_specs=[tile, pl.BlockSpec((1, n_pairs, V7X_LANES, V7X_LANES), lambda b, p, t: (b, p, 0, 0))],
        out_shape=[jax.ShapeDtypeStruct((B, T, D), F32),
                   jax.ShapeDtypeStruct((B, D // V7X_LANES, V7X_LANES, V7X_LANES), F32)],
        scratch_shapes=scratch,
        compiler_params=_params(("parallel", "parallel", "arbitrary"), pipelined, resident),
        name="wkv_chunked",
    )(r, lw, k, v, a, k_k, k_a, r_k, gn_w, gn_b)


def _wkv_step_kernel(s_ref, r_ref, lw_ref, k_ref, v_ref, a_ref, kk_ref, ka_ref, rk_ref, gw_ref, gb_ref,
                     snew_ref, y_ref):
    K = RWKV_HEAD
    S = s_ref[0]
    r, lw, k, v, a = r_ref[0], lw_ref[0], k_ref[0], v_ref[0], a_ref[0]
    kkr = k * kk_ref[...]
    nrm = jnp.sqrt(jnp.sum(kkr * kkr, axis=-1, keepdims=True))
    kk = kkr / jnp.maximum(nrm, 1e-12)
    b = kk * a
    km = k * (1.0 + (a - 1.0) * ka_ref[...])
    w = jnp.exp(lw)
    eye = (lax.broadcasted_iota(jnp.int32, (K, K), 0)
           == lax.broadcasted_iota(jnp.int32, (K, K), 1)).astype(F32)
    row = lambda t: t[:, None, :]
    col = lambda t: jnp.sum(eye[None] * t[:, None, :], axis=-1, keepdims=True)
    sa = -jnp.sum(S * row(kk), axis=-1, keepdims=True)
    s_new = S * row(w) + sa * row(b) + col(v) * row(km)
    snew_ref[0] = s_new
    y_col = jnp.sum(s_new * row(r), axis=-1, keepdims=True)
    y = jnp.sum(y_col * eye[None], axis=1)
    mean = jnp.mean(y, axis=-1, keepdims=True)
    cen = y - mean
    var = jnp.mean(cen * cen, axis=-1, keepdims=True)
    yn = cen * lax.rsqrt(var + GN_EPS) * gw_ref[...] + gb_ref[...]
    bonus = jnp.sum(r * km * rk_ref[...], axis=-1, keepdims=True) * v
    y_ref[0] = yn + bonus


def _wkv_step_call(s0, r, lw, k, v, a, k_k, k_a, r_k, gn_w, gn_b):
    B, H, K, _ = s0.shape
    st = pl.BlockSpec((1, H, K, K), lambda b: (b, 0, 0, 0))
    vec = pl.BlockSpec((1, H, K), lambda b: (b, 0, 0))
    par = pl.BlockSpec((H, K), lambda b: (0, 0))
    pipelined = 2 * _nbytes((H, K, V7X_LANES), F32) + 6 * _nbytes((H, V7X_LANES), F32)
    resident = 6 * _nbytes((H, K, V7X_LANES), F32)
    return pl.pallas_call(
        _wkv_step_kernel,
        grid=(B,),
        in_specs=[st] + [vec] * 5 + [par] * 5,
        out_specs=[st, vec],
        out_shape=[jax.ShapeDtypeStruct(s0.shape, F32), jax.ShapeDtypeStruct((B, H, K), F32)],
        compiler_params=_params(("parallel",), pipelined, resident),
        name="wkv_step",
    )(s0, r, lw, k, v, a, k_k, k_a, r_k, gn_w, gn_b)


def _rwkv_out_kernel(y_ref, g_ref, x_ref, gate_ref, gain_ref, wout_ref, o_ref):
    g = g_ref[0]
    z = _dot(y_ref[0] * (g * _sigmoid(g)), wout_ref[...])
    o_ref[0] = x_ref[0] + gate_ref[0] * _rmsnorm(z, gain_ref[...])


def _rwkv_out_call(y, g, x, mod, gain, w_out, *, per_row, tm):
    B, T, D = x.shape
    tile = pl.BlockSpec((1, tm, D), lambda b, t: (b, t, 0))
    full = lambda shape: pl.BlockSpec(shape, lambda b, t: (0,) * len(shape))
    pipelined = 5 * _nbytes((tm, D), F32) + _nbytes((D, D), BF16)
    return pl.pallas_call(
        _rwkv_out_kernel,
        grid=(B, T // tm),
        in_specs=[tile, tile, tile, _mod_spec(per_row, tm, D, 2), full((1, D)), full((D, D))],
        out_specs=tile,
        out_shape=jax.ShapeDtypeStruct((B, T, D), F32),
        compiler_params=_params(("parallel", "parallel"), pipelined, 4 * _nbytes((tm, D), F32)),
        name="rwkv_out",
    )(y, g, x, mod, gain, w_out)


def _mla_proj_kernel(x_ref, shift_ref, scale_ref, kshift_ref, kscale_ref, gain_ref, kgain_ref,
                     cos_ref, sin_ref, wqa_ref, wgate_ref, wkv_ref, qnorm_ref, kvnorm_ref,
                     wqn_ref, wqp_ref, wqs_ref, wuk_ref,
                     q_ref, kcat_ref, ckv_ref, kpe_ref, sg_ref):
    x = x_ref[0]
    ms = jnp.mean(x * x, axis=-1, keepdims=True)
    xn = x * lax.rsqrt(ms + EPS)
    h = xn * gain_ref[...] * (1.0 + scale_ref[0]) + shift_ref[0]
    hk = xn * kgain_ref[...] * (1.0 + kscale_ref[0]) + kshift_ref[0]
    cos = cos_ref[...]
    sin = sin_ref[...]

    kv = _dot(hk, wkv_ref[...])
    ckv = _rmsnorm(kv[:, :KV_LORA], kvnorm_ref[...])
    kpe = kv[:, KV_LORA:KV_LORA + 128] * cos + kv[:, KV_LORA + 128:] * sin
    ckv_ref[0] = ckv
    kpe_ref[0] = kpe[:, :QK_ROPE]
    kcat_ref[0, :, :KV_LORA] = ckv.astype(BF16)
    kcat_ref[0, :, KV_LORA:] = kpe.astype(BF16)

    g = _dot(h, wgate_ref[...])
    sg_ref[0] = g * _sigmoid(g)

    qn = _rmsnorm(_dot(h, wqa_ref[...]), qnorm_ref[...]).astype(BF16)
    q_nope = _dot(qn, wqn_ref[...])
    q_pe = _dot(qn, wqp_ref[...])
    q_ps = _dot(qn, wqs_ref[...])
    for hd in range(MLA_H):
        cols = slice(hd * 128, (hd + 1) * 128)
        q_lat = _dot(q_nope[:, cols], wuk_ref[hd])
        q_ref[0, hd, :, :KV_LORA] = (q_lat * Q_SCALE).astype(BF16)
        q_ref[0, hd, :, KV_LORA:] = ((q_pe[:, cols] * cos + q_ps[:, cols] * sin) * Q_SCALE).astype(BF16)


def _mla_proj_call(x, mod, kvmod, gain, kgain, cos, sin, wqa, wgate, wkv, qnorm, kvnorm,
                   wqn, wqp, wqs, wuk, *, per_row, tm):
    B, T, D = x.shape
    tile = lambda w: pl.BlockSpec((1, tm, w), lambda b, t: (b, t, 0))
    full = lambda shape: pl.BlockSpec(shape, lambda b, t: (0,) * len(shape))
    tab = pl.BlockSpec((tm, 128), lambda b, t: (t, 0))
    weights = (wqa, wgate, wkv, wqn, wqp, wqs, wuk)
    pipelined = (sum(_nbytes(w.shape, BF16) for w in weights) + 3 * _nbytes((tm, D), F32)
                 + _nbytes((MLA_H, tm, KCAT), BF16) + 2 * _nbytes((tm, KCAT), F32))
    return pl.pallas_call(
        _mla_proj_kernel,
        grid=(B, T // tm),
        in_specs=[tile(D), _mod_spec(per_row, tm, D, 0), _mod_spec(per_row, tm, D, 1),
                  _mod_spec(per_row, tm, D, 0), _mod_spec(per_row, tm, D, 1),
                  full((1, D)), full((1, D)), tab, tab,
                  full(wqa.shape), full(wgate.shape), full(wkv.shape), full((1, Q_LORA)), full((1, KV_LORA)),
                  full(wqn.shape), full(wqp.shape), full(wqs.shape), full(wuk.shape)],
        out_specs=[pl.BlockSpec((1, MLA_H, tm, KCAT), lambda b, t: (b, 0, t, 0)),
                   tile(KCAT), tile(KV_LORA), tile(QK_ROPE), tile(MLA_H * V_HEAD)],
        out_shape=[jax.ShapeDtypeStruct((B, MLA_H, T, KCAT), BF16),
                   jax.ShapeDtypeStruct((B, T, KCAT), BF16),
                   jax.ShapeDtypeStruct((B, T, KV_LORA), F32),
                   jax.ShapeDtypeStruct((B, T, QK_ROPE), F32),
                   jax.ShapeDtypeStruct((B, T, MLA_H * V_HEAD), F32)],
        compiler_params=_params(("parallel", "parallel"), pipelined, 12 * _nbytes((tm, D), F32)),
        name="mla_proj",
    )(x, mod, mod, kvmod, kvmod, gain, kgain, cos, sin, wqa, wgate, wkv, qnorm, kvnorm,
      wqn, wqp, wqs, wuk)


def _lane_tile(t, width):
    return jnp.concatenate([t] * (width // V7X_LANES), axis=1)


def _flash_kernel(qi_ref, ki_ref, q_ref, k_ref, o_ref, m_scr, l_scr, acc_scr, *, tq):
    step = pl.program_id(1)
    qi = qi_ref[step]
    ki = ki_ref[step]

    @pl.when(ki == 0)
    def _():
        m_scr[...] = jnp.full_like(m_scr, -jnp.inf)
        l_scr[...] = jnp.zeros_like(l_scr)
        acc_scr[...] = jnp.zeros_like(acc_scr)

    def update(masked):
        kc = k_ref[0]
        vc = kc[:, :KV_LORA]
        scores = lambda hd: lax.dot_general(q_ref[0, hd], kc, _NT, preferred_element_type=F32)
        if masked:
            causal = (lax.broadcasted_iota(jnp.int32, (tq, tq), 1)
                      <= lax.broadcasted_iota(jnp.int32, (tq, tq), 0))
        s_next = scores(0)
        for hd in range(MLA_H):
            s = s_next
            if hd + 1 < MLA_H:
                s_next = scores(hd + 1)
            if masked:
                s = jnp.where(causal, s, -jnp.inf)
            m_prev = m_scr[hd]
            m_new = jnp.maximum(m_prev, jnp.max(s, axis=-1, keepdims=True))
            alpha = jnp.exp2(m_prev - m_new)
            p = jnp.exp2(s - _lane_tile(m_new, tq))
            l_scr[hd] = alpha * l_scr[hd] + jnp.sum(p, axis=-1, keepdims=True)
            acc_scr[hd] = (_lane_tile(alpha, KV_LORA) * acc_scr[hd]
                           + jnp.dot(p.astype(BF16), vc, preferred_element_type=F32))
            m_scr[hd] = m_new

    @pl.when(ki < qi)
    def _():
        update(False)

    @pl.when(ki == qi)
    def _():
        update(True)
        for hd in range(MLA_H):
            o_ref[0, hd] = (acc_scr[hd] * _lane_tile(1.0 / l_scr[hd], KV_LORA)).astype(BF16)


def _flash_call(q, kcat, *, tq):
    B, H, T, _ = q.shape
    nq = T // tq
    pairs = [(i, j) for i in range(nq) for j in range(i + 1)]
    qi_tab = jnp.asarray([i for i, _ in pairs], jnp.int32)
    ki_tab = jnp.asarray([j for _, j in pairs], jnp.int32)
    pipelined = (_nbytes((H, tq, KCAT), BF16) + _nbytes((tq, KCAT), BF16)
                 + _nbytes((H, tq, KV_LORA), BF16))
    resident = _nbytes((H, tq, KV_LORA + 2 * V7X_LANES), F32) + 8 * _nbytes((tq, tq), F32)
    return pl.pallas_call(
        functools.partial(_flash_kernel, tq=tq),
        grid_spec=pltpu.PrefetchScalarGridSpec(
            num_scalar_prefetch=2,
            grid=(B, len(pairs)),
            in_specs=[pl.BlockSpec((1, H, tq, KCAT), lambda b, s, qt, kt: (b, 0, qt[s], 0)),
                      pl.BlockSpec((1, tq, KCAT), lambda b, s, qt, kt: (b, kt[s], 0))],
            out_specs=pl.BlockSpec((1, H, tq, KV_LORA), lambda b, s, qt, kt: (b, 0, qt[s], 0)),
            scratch_shapes=[pltpu.VMEM((H, tq, V7X_LANES), F32), pltpu.VMEM((H, tq, V7X_LANES), F32),
                            pltpu.VMEM((H, tq, KV_LORA), F32)]),
        out_shape=jax.ShapeDtypeStruct((B, H, T, KV_LORA), BF16),
        compiler_params=_params(("parallel", "arbitrary"), pipelined, resident),
        name="mla_flash",
    )(qi_tab, ki_tab, q, kcat)


DECODE_PAGES = 32


def _decode_kernel(pt_ref, q_ref, cnew_ref, pnew_ref, *refs):
    del pt_ref
    ck_refs = refs[:DECODE_PAGES]
    kp_refs = refs[DECODE_PAGES:2 * DECODE_PAGES]
    o_ref, m_scr, l_scr, acc_scr, kbuf, pbuf = refs[2 * DECODE_PAGES:]
    j = pl.program_id(1)
    q = q_ref[0]
    q_lat = q[:, :KV_LORA]
    q_pe = q[:, KV_LORA:KV_LORA + QK_ROPE]

    @pl.when(j == 0)
    def _():
        cn = cnew_ref[0]
        s_new = (jnp.sum(q_lat.astype(F32) * cn, axis=-1, keepdims=True)
                 + jnp.sum(q_pe.astype(F32) * pnew_ref[0], axis=-1, keepdims=True))
        m_scr[...] = s_new
        l_scr[...] = jnp.ones_like(l_scr)
        acc_scr[...] = jnp.broadcast_to(cn, acc_scr.shape)

    ps = ck_refs[0].shape[1]
    for i in range(DECODE_PAGES):
        kbuf[i * ps:(i + 1) * ps, :] = ck_refs[i][0].astype(BF16)
        pbuf[:, i * ps:(i + 1) * ps] = kp_refs[i][0].astype(BF16)
    keys = kbuf[...]
    s = (lax.dot_general(q_lat, keys, _NT, preferred_element_type=F32)
         + jnp.dot(q_pe, pbuf[...], preferred_element_type=F32))
    m_prev = m_scr[...]
    m_new = jnp.maximum(m_prev, jnp.max(s, axis=-1, keepdims=True))
    alpha = jnp.exp2(m_prev - m_new)
    p = jnp.exp2(s - m_new)
    l_scr[...] = alpha * l_scr[...] + jnp.sum(p, axis=-1, keepdims=True)
    acc_scr[...] = alpha * acc_scr[...] + jnp.dot(p.astype(BF16), keys, preferred_element_type=F32)
    m_scr[...] = m_new

    @pl.when(j == pl.num_programs(1) - 1)
    def _():
        o_ref[0] = acc_scr[...] / l_scr[...]


def _decode_call(page_table, q, c_new, p_new, cache_ckv, cache_kpe):
    B, H, _ = q.shape
    n_pages = page_table.shape[1]
    ps = cache_ckv.shape[1]
    assert n_pages % DECODE_PAGES == 0
    steps = n_pages // DECODE_PAGES

    def page_spec(rows, width, i):
        return pl.BlockSpec((1, rows, width), lambda b, j, pt: (pt[b, j * DECODE_PAGES + i], 0, 0))

    in_specs = ([pl.BlockSpec((1, H, KCAT), lambda b, j, pt: (b, 0, 0)),
                 pl.BlockSpec((1, 1, KV_LORA), lambda b, j, pt: (b, 0, 0)),
                 pl.BlockSpec((1, 1, QK_ROPE), lambda b, j, pt: (b, 0, 0))]
                + [page_spec(ps, KV_LORA, i) for i in range(DECODE_PAGES)]
                + [page_spec(QK_ROPE, ps, i) for i in range(DECODE_PAGES)])
    pipelined = DECODE_PAGES * (_nbytes((ps, KV_LORA), F32) + _nbytes((QK_ROPE, ps), F32))
    resident = DECODE_PAGES * _nbytes((ps, KV_LORA), F32)
    return pl.pallas_call(
        _decode_kernel,
        grid_spec=pltpu.PrefetchScalarGridSpec(
            num_scalar_prefetch=1,
            grid=(B, steps),
            in_specs=in_specs,
            out_specs=pl.BlockSpec((1, H, KV_LORA), lambda b, j, pt: (b, 0, 0)),
            scratch_shapes=[pltpu.VMEM((H, 1), F32), pltpu.VMEM((H, 1), F32),
                            pltpu.VMEM((H, KV_LORA), F32),
                            pltpu.VMEM((DECODE_PAGES * ps, KV_LORA), BF16),
                            pltpu.VMEM((QK_ROPE, DECODE_PAGES * ps), BF16)]),
        out_shape=jax.ShapeDtypeStruct((B, H, KV_LORA), F32),
        compiler_params=_params(("parallel", "arbitrary"), pipelined, resident),
        name="mla_decode",
    )(page_table, q, c_new, p_new, *([cache_ckv] * DECODE_PAGES), *([cache_kpe] * DECODE_PAGES))


def _mla_out_kernel(o_ref, sg_ref, x_ref, gate_ref, gain_ref, wuv_ref, wout_ref, y_ref, og_scr):
    for hd in range(MLA_H):
        cols = slice(hd * V_HEAD, (hd + 1) * V_HEAD)
        o = _dot(o_ref[0, hd], wuv_ref[hd])
        og_scr[:, cols] = (o * sg_ref[0, :, cols]).astype(BF16)
    z = _dot(og_scr[...], wout_ref[...])
    y_ref[0] = x_ref[0] + gate_ref[0] * _rmsnorm(z, gain_ref[...])


def _mla_out_call(o_lat, sg, x, mod, gain, wuv, wout, *, per_row, tm):
    B, T, D = x.shape
    tile = lambda w: pl.BlockSpec((1, tm, w), lambda b, t: (b, t, 0))
    full = lambda shape: pl.BlockSpec(shape, lambda b, t: (0,) * len(shape))
    pipelined = (_nbytes((MLA_H, tm, KV_LORA), BF16) + 4 * _nbytes((tm, D), F32)
                 + _nbytes(wuv.shape, BF16) + _nbytes(wout.shape, BF16))
    return pl.pallas_call(
        _mla_out_kernel,
        grid=(B, T // tm),
        in_specs=[pl.BlockSpec((1, MLA_H, tm, KV_LORA), lambda b, t: (b, 0, t, 0)),
                  tile(MLA_H * V_HEAD), tile(D), _mod_spec(per_row, tm, D, 2), full((1, D)),
                  full(wuv.shape), full(wout.shape)],
        out_specs=tile(D),
        out_shape=jax.ShapeDtypeStruct((B, T, D), F32),
        scratch_shapes=[pltpu.VMEM((tm, MLA_H * V_HEAD), BF16)],
        compiler_params=_params(("parallel", "parallel"), pipelined, 4 * _nbytes((tm, D), F32)),
        name="mla_out",
    )(o_lat, sg, x, mod, gain, wuv, wout)


def _rope_tables(pos):
    half = QK_ROPE // 2
    inv = ROPE_THETA ** (-jnp.arange(half, dtype=F32) / half)
    ang = pos.astype(F32)[:, None] * inv[None, :]
    c, s = jnp.cos(ang), jnp.sin(ang)
    z = jnp.zeros((pos.shape[0], 128 - QK_ROPE), F32)
    return jnp.concatenate([c, c, z], axis=1), jnp.concatenate([-s, s, z], axis=1)


def _swap_halves(w):
    half = w.shape[-1] // 2
    return jnp.concatenate([w[..., half:], w[..., :half]], axis=-1)


def _pad_lanes(w):
    return jnp.concatenate([w, jnp.zeros(w.shape[:-1] + (128 - w.shape[-1],), w.dtype)], axis=-1)


def kernel(x_prompt, x_sample, c_prompt, c_sample, state_wkv, state_shift, cache_kv_latent, cache_k_rope, page_table, ada_w, ada_b, norm_pre, norm_post, a_mu, a_w_in, a_w0, a_w1, a_w2, a_a0, a_a1, a_a2, a_k_k, a_k_a, a_r_k, a_gn_w, a_gn_b, a_w_out, kv_ada_w, kv_ada_b, kv_norm, kv_w_a, kv_a_norm, kv_w_b, b_w_in, b_q_norm, b_w_q, b_w_out):
    B, T, D = x_prompt.shape
    DB = x_sample.shape[0]
    H = D // RWKV_HEAD
    assert ada_w.shape[0] == 2 and a_mu.shape[0] == 1 and b_w_in.shape[0] == 1
    assert x_sample.shape[1] == 1 and T % ROW_TILE == 0 and DB % V7X_SUBLANES == 0

    c_all = jnp.concatenate([c_prompt, c_sample], axis=0)
    mods = _ada_call(c_all, ada_w, ada_b)
    kvmods = _ada_call(c_all, kv_ada_w[None], kv_ada_b[None])
    mod_p = [mods[i, :B].reshape(B, 1, 3 * D) for i in range(2)]
    mod_s = [mods[i, B:].reshape(1, DB, 3 * D) for i in range(2)]
    kvmod_p = kvmods[0, :B].reshape(B, 1, 2 * D)
    kvmod_s = kvmods[0, B:].reshape(1, DB, 2 * D)

    row = lambda v: v.reshape(1, -1)
    w_in = a_w_in[0].astype(BF16)
    a_args = (row(norm_pre[0]), a_mu[0], w_in, row(a_w0[0]), a_w1[0].astype(BF16), a_w2[0].astype(BF16),
              row(a_a0[0]), a_a1[0].astype(BF16), a_a2[0].astype(BF16))
    w_out_a = a_w_out[0].astype(BF16)
    k_k, k_a, r_k = row(a_k_k[0]), row(a_k_a[0]), row(a_r_k[0])
    gn_w, gn_b = row(a_gn_w[0]), row(a_gn_b[0])

    w_bin = b_w_in[0]
    wqa = w_bin[:, :Q_LORA].astype(BF16)
    wgate = w_bin[:, Q_LORA:].astype(BF16)
    kv_pe = kv_w_a[:, KV_LORA:]
    wkv = jnp.concatenate([kv_w_a[:, :KV_LORA], _pad_lanes(kv_pe), _pad_lanes(_swap_halves(kv_pe))],
                          axis=1).astype(BF16)
    w_q = b_w_q[0]
    wqn = w_q[:, :, :QK_NOPE].reshape(Q_LORA, MLA_H * QK_NOPE).astype(BF16)
    wq_pe = w_q[:, :, QK_NOPE:]
    wqp = _pad_lanes(wq_pe).reshape(Q_LORA, MLA_H * 128).astype(BF16)
    wqs = _pad_lanes(_swap_halves(wq_pe)).reshape(Q_LORA, MLA_H * 128).astype(BF16)
    wuk = jnp.transpose(kv_w_b[:, :, :QK_NOPE], (1, 2, 0)).astype(BF16)
    wuv = jnp.transpose(kv_w_b[:, :, QK_NOPE:], (1, 0, 2)).astype(BF16)
    wout_b = b_w_out[0].astype(BF16)
    b_args = (wqa, wgate, wkv, row(b_q_norm[0]), row(kv_a_norm), wqn, wqp, wqs, wuk)

    tm = ROW_TILE
    r, lw, k, v, a, g, h_last = _rwkv_proj_call(x_prompt, x_prompt, mod_p[0], *a_args,
                                                seq_shift=True, tm=tm)
    yw, s_packed = _wkv_chunk_call(r, lw, k, v, a, k_k, k_a, r_k, gn_w, gn_b,
                                   t_block=min(T, 512), n_pairs=4)
    x1 = _rwkv_out_call(yw, g, x_prompt, mod_p[0], row(norm_post[0]), w_out_a, per_row=False, tm=tm)
    s_packed = s_packed.reshape(B, H // 2, 2, RWKV_HEAD, 2, RWKV_HEAD)
    wkv_p = jnp.stack([s_packed[:, :, 0, :, 0, :], s_packed[:, :, 1, :, 1, :]], axis=2)
    wkv_p = wkv_p.reshape(1, B, H, RWKV_HEAD, RWKV_HEAD)
    shift_p = h_last.reshape(1, B, D)

    cos_p, sin_p = _rope_tables(jnp.arange(T, dtype=jnp.int32))
    q_p, kcat_p, ckv_p, kpe_p, sg_p = _mla_proj_call(
        x1, mod_p[1], kvmod_p, row(norm_pre[1]), row(kv_norm), cos_p, sin_p, *b_args,
        per_row=False, tm=tm)
    o_p = _flash_call(q_p, kcat_p, tq=min(T, 256))
    y_prompt = _mla_out_call(o_p, sg_p, x1, mod_p[1], row(norm_post[1]), wuv, wout_b,
                             per_row=False, tm=tm)

    xs = x_sample.reshape(1, DB, D)
    rs, lws, ks, vs, as_, gs, hs = _rwkv_proj_call(xs, state_shift[0].reshape(1, DB, D), mod_s[0],
                                                   *a_args, seq_shift=False, tm=DB)
    hk = lambda t: t.reshape(DB, H, RWKV_HEAD)
    pk = lambda t: t.reshape(H, RWKV_HEAD)
    s_new, yws = _wkv_step_call(state_wkv[0], hk(rs), hk(lws), hk(ks), hk(vs), hk(as_),
                                pk(k_k), pk(k_a), pk(r_k), pk(gn_w), pk(gn_b))
    x1s = _rwkv_out_call(yws.reshape(1, DB, D), gs, xs, mod_s[0], row(norm_post[0]), w_out_a,
                         per_row=True, tm=DB)
    n_pages = page_table.shape[1]
    past_len = n_pages * cache_kv_latent.shape[1]
    cos_s, sin_s = _rope_tables(jnp.full((DB,), past_len, dtype=jnp.int32))
    q_s, _, ckv_s, kpe_s, sg_s = _mla_proj_call(
        x1s, mod_s[1], kvmod_s, row(norm_pre[1]), row(kv_norm), cos_s, sin_s, *b_args,
        per_row=True, tm=DB)
    o_s = _decode_call(page_table, jnp.transpose(q_s[0], (1, 0, 2)),
                       ckv_s.reshape(DB, 1, KV_LORA), kpe_s.reshape(DB, 1, QK_ROPE),
                       cache_kv_latent, jnp.swapaxes(cache_k_rope, 1, 2))
    o_s = jnp.transpose(o_s, (1, 0, 2)).astype(BF16)[None]
    y_s = _mla_out_call(o_s, sg_s, x1s, mod_s[1], row(norm_post[1]), wuv, wout_b,
                        per_row=True, tm=DB)

    return (y_prompt, y_s.reshape(DB, 1, D), wkv_p, shift_p, ckv_p, kpe_p,
            s_new[None], hs.reshape(1, DB, D), ckv_s.reshape(DB, 1, KV_LORA),
            kpe_s.reshape(DB, 1, QK_ROPE))
```

```python
import functools
import math

import jax
import jax.numpy as jnp
from jax import lax
from jax.experimental import pallas as pl
from jax.experimental.pallas import tpu as pltpu

F32 = jnp.float32
BF16 = jnp.bfloat16
HIGHEST = lax.Precision.HIGHEST

RWKV_HEAD = 64
GN_EPS = 64e-5
EPS = 1e-6
MLA_H = 8
QK_NOPE = 128
QK_ROPE = 64
V_HEAD = 128
Q_LORA = 384
KV_LORA = 256
ROPE_THETA = 10000.0
ATTN_SCALE = (QK_NOPE + QK_ROPE) ** -0.5
Q_SCALE = ATTN_SCALE * math.log2(math.e)
KCAT = KV_LORA + 128

V7X_LANES = 128
V7X_SUBLANES = 8
V7X_VMEM_BYTES = 64 * 1024 * 1024
V7X_VMEM_REQUEST_CAP = V7X_VMEM_BYTES - 8 * 1024 * 1024

WKV_CHUNK = 64
ROW_TILE = 256
FLASH_HEAD_GROUP = 2


def _vmem_limit(pipelined_bytes, resident_bytes=0):
    est = 2 * pipelined_bytes + resident_bytes + 4 * 1024 * 1024
    return int(min(max(est, 16 * 1024 * 1024), V7X_VMEM_REQUEST_CAP))


def _nbytes(shape, dtype):
    return math.prod(shape) * jnp.dtype(dtype).itemsize


def _params(sem, pipelined_bytes, resident_bytes=0):
    return pltpu.CompilerParams(
        dimension_semantics=sem,
        vmem_limit_bytes=_vmem_limit(pipelined_bytes, resident_bytes))


def _dot(a, b):
    return jnp.dot(a.astype(BF16), b.astype(BF16), preferred_element_type=F32)


def _dot_hi(a, b, dims=(((1,), (0,)), ((), ()))):
    return lax.dot_general(a, b, dims, precision=HIGHEST, preferred_element_type=F32)


_NT = (((1,), (1,)), ((), ()))
_TN = (((0,), (0,)), ((), ()))


def _sigmoid(x):
    return 1.0 / (1.0 + jnp.exp(-x))


def _ada_kernel(c_ref, w_ref, b_ref, o_ref):
    o_ref[0] = _dot_hi(c_ref[...], w_ref[0]) + b_ref[0]


def _ada_call(c, w, b):
    G, D, N = w.shape
    M = c.shape[0]
    tn = 1024
    pipelined = _nbytes((D, tn), F32) + _nbytes((M, tn), F32) + _nbytes((M, D), F32)
    return pl.pallas_call(
        _ada_kernel,
        grid=(G, N // tn),
        in_specs=[
            pl.BlockSpec((M, D), lambda g, j: (0, 0)),
            pl.BlockSpec((1, D, tn), lambda g, j: (g, 0, j)),
            pl.BlockSpec((1, 1, tn), lambda g, j: (g, 0, j)),
        ],
        out_specs=pl.BlockSpec((1, M, tn), lambda g, j: (g, 0, j)),
        out_shape=jax.ShapeDtypeStruct((G, M, N), F32),
        compiler_params=_params(("parallel", "parallel"), pipelined),
        name="ada_modulation",
    )(c, w, b.reshape(G, 1, N))


def _mod_spec(per_row, tm, D, col):
    if per_row:
        return pl.BlockSpec((1, tm, D), lambda b, t: (b, t, col))
    return pl.BlockSpec((1, 1, D), lambda b, t: (b, 0, col))


def _modnorm(x, gain, scale, shift):
    ms = jnp.mean(x * x, axis=-1, keepdims=True)
    return x * lax.rsqrt(ms + EPS) * gain * (1.0 + scale) + shift


def _rmsnorm(x, gain):
    ms = jnp.mean(x * x, axis=-1, keepdims=True)
    return x * lax.rsqrt(ms + EPS) * gain


def _rwkv_proj_kernel(x_ref, prev_ref, shift_ref, scale_ref, gain_ref, mu_ref, win_ref,
                      w0_ref, w1_ref, w2_ref, a0_ref, a1_ref, a2_ref,
                      r_ref, lw_ref, k_ref, v_ref, a_ref, g_ref, hlast_ref, *, seq_shift):
    x = x_ref[0]
    gain = gain_ref[...]
    scale = scale_ref[0]
    shift = shift_ref[0]
    h = _modnorm(x, gain, scale, shift)
    tm = h.shape[0]
    if seq_shift:
        hp = _modnorm(prev_ref[0][V7X_SUBLANES - 1:V7X_SUBLANES, :], gain, scale, shift)
        hp = jnp.where(pl.program_id(1) == 0, 0.0, hp)
        row = lax.broadcasted_iota(jnp.int32, (tm, 1), 0)
        hs = jnp.where(row == 0, hp, pltpu.roll(h, 1, axis=0))
        hlast_ref[0] = h[tm - 1:tm, :]
    else:
        hs = prev_ref[0]
        hlast_ref[0] = h
    xx = hs - h
    mu = mu_ref[...]
    outs = (r_ref, k_ref, v_ref, g_ref)
    for m in range(4):
        xm = h + xx * mu[m:m + 1, :]
        outs[m][0] = _dot(xm, win_ref[m]).astype(outs[m].dtype)
    xw = h + xx * mu[4:5, :]
    xa = h + xx * mu[5:6, :]
    wl = w0_ref[...] + _dot(jnp.tanh(_dot(xw, w1_ref[...])), w2_ref[...])
    z = -wl
    softplus = jnp.maximum(z, 0.0) + jnp.log(1.0 + jnp.exp(-jnp.abs(z)))
    lw_ref[0] = -jnp.exp(-softplus - 0.5)
    al = a0_ref[...] + _dot(_dot(xa, a1_ref[...]), a2_ref[...])
    a_ref[0] = _sigmoid(al).astype(a_ref.dtype)


def _rwkv_proj_call(x, prev, mod, gain, mu, w_in, w0, w1, w2, a0, a1, a2, *, seq_shift, tm):
    B, T, D = x.shape
    per_row = not seq_shift
    nt = T // tm
    tile = pl.BlockSpec((1, tm, D), lambda b, t: (b, t, 0))
    if seq_shift:
        sub = tm // V7X_SUBLANES
        prev_spec = pl.BlockSpec((1, V7X_SUBLANES, D),
                                 lambda b, t: (b, jnp.maximum(t * sub - 1, 0), 0))
        hlast_shape = jax.ShapeDtypeStruct((B, 1, D), F32)
        hlast_spec = pl.BlockSpec((1, 1, D), lambda b, t: (b, 0, 0))
        sem = ("parallel", "arbitrary")
    else:
        prev_spec = tile
        hlast_shape = jax.ShapeDtypeStruct((B, T, D), F32)
        hlast_spec = tile
        sem = ("parallel", "parallel")
    full = lambda shape: pl.BlockSpec(shape, lambda b, t: (0,) * len(shape))
    lora = w1.shape[1]
    pipelined = (9 * _nbytes((tm, D), F32) + _nbytes((4, D, D), BF16)
                 + 4 * _nbytes((D, V7X_LANES), BF16))
    resident = 8 * _nbytes((tm, D), F32)
    out_sds = lambda dt: jax.ShapeDtypeStruct((B, T, D), dt)
    return pl.pallas_call(
        functools.partial(_rwkv_proj_kernel, seq_shift=seq_shift),
        grid=(B, nt),
        in_specs=[tile, prev_spec, _mod_spec(per_row, tm, D, 0), _mod_spec(per_row, tm, D, 1),
                  full((1, D)), full((6, D)), full((4, D, D)),
                  full((1, D)), full((D, lora)), full((lora, D)),
                  full((1, D)), full((D, lora)), full((lora, D))],
        out_specs=[tile] * 6 + [hlast_spec],
        out_shape=[out_sds(BF16), out_sds(F32)] + [out_sds(BF16)] * 4 + [hlast_shape],
        compiler_params=_params(sem, pipelined, resident),
        name="rwkv_proj",
    )(x, prev, mod, mod, gain, mu, w_in, w0, w1, w2, a0, a1, a2)


def _split(x):
    hi = x.astype(BF16)
    return hi, (x - hi.astype(F32)).astype(BF16)


def _dot_split(a, b, dims=(((1,), (0,)), ((), ()))):
    (ah, al), (bh, bl) = a, b
    dot = lambda x, y: lax.dot_general(x, y, dims, preferred_element_type=F32)
    return dot(ah, bh) + dot(ah, bl) + dot(al, bh)


def _cumsum_rows(cum, x):
    hi = x.astype(BF16)
    rest = x - hi.astype(F32)
    mid = rest.astype(BF16)
    lo = (rest - mid.astype(F32)).astype(BF16)
    dot = lambda t: jnp.dot(cum, t, preferred_element_type=F32)
    return dot(hi) + dot(mid) + dot(lo)


def _wkv_chunk_kernel(r_ref, lw_ref, k_ref, v_ref, a_ref, kk_ref, ka_ref, rk_ref, gw_ref, gb_ref,
                      y_ref, sfin_ref, s_scr, wr_scr, tinv_scr, av_scr, ar_scr, bkp_scr, v2_scr, pend_scr,
                      y2_scr, bias_scr, *, n_chunks, n_pairs):
    L = WKV_CHUNK
    L2 = 2 * L
    tc = pl.program_id(2)

    @pl.when(tc == 0)
    def _():
        s_scr[...] = jnp.zeros_like(s_scr)

    lane = lax.broadcasted_iota(jnp.int32, (L2, V7X_LANES), 1)
    srow = lax.broadcasted_iota(jnp.int32, (L2, V7X_LANES), 0)
    own = (srow < L) == (lane < RWKV_HEAD)
    ti = lax.broadcasted_iota(jnp.int32, (L, L), 0)
    tj = lax.broadcasted_iota(jnp.int32, (L, L), 1)
    cum = (ti >= tj).astype(BF16)
    ri = lax.broadcasted_iota(jnp.int32, (L2, L2), 0)
    rj = lax.broadcasted_iota(jnp.int32, (L2, L2), 1)
    strict = ri > rj
    incl = ri >= rj
    eye = (ri == rj).astype(F32)
    n_rounds = int(math.log2(L))

    def stack(x):
        return jnp.where(own, jnp.concatenate([x, x], axis=0), 0.0)

    def fold(x2):
        return x2[:L] + x2[L:]

    def precompute(c, carry):
        t0 = pl.multiple_of(c * L, L)
        pairs = range(n_pairs)
        cols = [slice(p * V7X_LANES, (p + 1) * V7X_LANES) for p in pairs]
        load = lambda ref: [ref[0, pl.ds(t0, L), cols[p]].astype(F32) for p in pairs]
        r, lw, k, v, a = load(r_ref), load(lw_ref), load(k_ref), load(v_ref), load(a_ref)

        r2 = [stack(r[p]) for p in pairs]
        v2 = [stack(v[p]) for p in pairs]
        a2 = [stack(a[p]) for p in pairs]
        kkr = [stack(k[p] * kk_ref[:, cols[p]]) for p in pairs]
        nrm = [jnp.sqrt(jnp.sum(kkr[p] * kkr[p], axis=-1, keepdims=True)) for p in pairs]
        kk2 = [kkr[p] / jnp.maximum(nrm[p], 1e-12) for p in pairs]
        b2 = [kk2[p] * a2[p] for p in pairs]
        km2 = [stack(k[p] * (1.0 + (a[p] - 1.0) * ka_ref[:, cols[p]])) for p in pairs]

        cs = [_cumsum_rows(cum, lw[p]) for p in pairs]
        cs_end = [cs[p][L - 1:L, :] for p in pairs]
        cs2 = [jnp.concatenate([cs[p], cs[p]], axis=0) for p in pairs]
        lw2 = [jnp.concatenate([lw[p], lw[p]], axis=0) for p in pairs]
        wr = [jnp.concatenate([kk2[p] * jnp.exp(cs2[p] - lw2[p]), r2[p] * jnp.exp(cs2[p])],
                              axis=0).astype(BF16) for p in pairs]
        e_neg = [jnp.exp(-cs2[p]) for p in pairs]
        bk = [jnp.concatenate([b2[p] * e_neg[p], km2[p] * e_neg[p]], axis=0).astype(BF16) for p in pairs]
        aa = [lax.dot_general(wr[p], bk[p], _NT, preferred_element_type=F32) for p in pairs]

        m0 = [jnp.where(strict, -aa[p][:L2, :L2], 0.0) for p in pairs]
        m = m0
        x = [eye for p in pairs]
        for it in range(n_rounds):
            if it + 1 < n_rounds:
                mx = [_dot(m[p], jnp.concatenate([m[p], x[p]], axis=1)) for p in pairs]
                x = [x[p] + mx[p][:, L2:] for p in pairs]
                m = [mx[p][:, :L2] for p in pairs]
            else:
                x = [x[p] + _dot(m[p], x[p]) for p in pairs]
        res = [eye - x[p] + _dot_split(_split(m0[p]), _split(x[p])) for p in pairs]
        x = [x[p] + _dot(x[p], res[p]) for p in pairs]
        av = [_dot(jnp.where(strict, aa[p][:L2, L2:], 0.0), v2[p]) for p in pairs]

        for p in pairs:
            e_end = jnp.exp(cs_end[p] - cs2[p])
            bkp_hi, bkp_lo = _split(jnp.concatenate([b2[p] * e_end, km2[p] * e_end], axis=0))
            v2_hi, v2_lo = _split(v2[p])
            wr_scr[c, p] = wr[p]
            tinv_scr[c, p] = x[p].astype(BF16)
            av_scr[c, p] = av[p]
            ar_scr[c, p] = jnp.where(jnp.concatenate([incl, incl], axis=1), aa[p][L2:, :], 0.0).astype(BF16)
            bkp_scr[c, p, 0] = bkp_hi
            bkp_scr[c, p, 1] = bkp_lo
            v2_scr[c, p, 0] = v2_hi
            v2_scr[c, p, 1] = v2_lo
            pend_scr[c, p] = jnp.broadcast_to(jnp.exp(cs_end[p]), (V7X_SUBLANES, V7X_LANES))
            bonus = fold(jnp.sum(r2[p] * km2[p] * rk_ref[:, cols[p]], axis=-1, keepdims=True) * v2[p])
            bias_scr[c, p] = gb_ref[:, cols[p]] + bonus
        return carry

    def recur(c, carry):
        pairs = range(n_pairs)
        s0 = [s_scr[p] for p in pairs]
        g = [lax.dot_general(wr_scr[c, p], s0[p].astype(BF16), _NT, preferred_element_type=F32)
             for p in pairs]
        u = [_dot(tinv_scr[c, p], -(g[p][:L2] + av_scr[c, p])) for p in pairs]
        us = [_split(u[p]) for p in pairs]
        uv = [tuple(jnp.concatenate([us[p][i], v2_scr[c, p, i]], axis=0) for i in range(2)) for p in pairs]
        for p in pairs:
            s_scr[p] = (s0[p] * pend_scr[c, p][0:1, :]
                        + _dot_split(uv[p], (bkp_scr[c, p, 0], bkp_scr[c, p, 1]), _TN))
        for p in pairs:
            y2_scr[c, p] = g[p][L2:] + jnp.dot(ar_scr[c, p], uv[p][0], preferred_element_type=F32)
        return carry

    def normalise(c, carry):
        t0 = pl.multiple_of(c * L, L)
        pairs = range(n_pairs)
        y2 = [y2_scr[c, p] for p in pairs]
        mean = [jnp.sum(y2[p], axis=-1, keepdims=True) * (1.0 / RWKV_HEAD) for p in pairs]
        cen = [jnp.where(own, y2[p] - mean[p], 0.0) for p in pairs]
        var = [jnp.sum(cen[p] * cen[p], axis=-1, keepdims=True) * (1.0 / RWKV_HEAD) for p in pairs]
        for p in pairs:
            cols = slice(p * V7X_LANES, (p + 1) * V7X_LANES)
            yn = fold(cen[p] * lax.rsqrt(var[p] + GN_EPS))
            y_ref[0, pl.ds(t0, L), cols] = (bias_scr[c, p] + yn * gw_ref[:, cols]).astype(y_ref.dtype)
        return carry

    lax.fori_loop(0, n_chunks, precompute, 0)
    lax.fori_loop(0, n_chunks, recur, 0)
    lax.fori_loop(0, n_chunks, normalise, 0)

    @pl.when(tc == pl.num_programs(2) - 1)
    def _():
        sfin_ref[0] = s_scr[...]


def _wkv_chunk_call(r, lw, k, v, a, k_k, k_a, r_k, gn_w, gn_b, *, t_block, n_pairs):
    B, T, D = r.shape
    L2 = 2 * WKV_CHUNK
    n_chunks = t_block // WKV_CHUNK
    wcol = n_pairs * V7X_LANES
    n_col = D // wcol
    tile = pl.BlockSpec((1, t_block, wcol), lambda b, p, t: (b, t, p))
    vec = pl.BlockSpec((1, wcol), lambda b, p, t: (0, p))
    per_chunk = lambda shape, dt: ((n_chunks, n_pairs) + shape, dt)
    per_chunk_scratch = [
        per_chunk((2 * L2, V7X_LANES), BF16),
        per_chunk((L2, L2), BF16),
        per_chunk((L2, V7X_LANES), F32),
        per_chunk((L2, 2 * L2), BF16),
        per_chunk((2, 2 * L2, V7X_LANES), BF16),
        per_chunk((2, L2, V7X_LANES), BF16),
        per_chunk((V7X_SUBLANES, V7X_LANES), F32),
        per_chunk((L2, V7X_LANES), F32),
        per_chunk((WKV_CHUNK, V7X_LANES), F32)]
    scratch = ([pltpu.VMEM((n_pairs, V7X_LANES, V7X_LANES), F32)]
               + [pltpu.VMEM(shape, dt) for shape, dt in per_chunk_scratch])
    scratch_bytes = sum(_nbytes(shape, dt) for shape, dt in per_chunk_scratch)
    pipelined = 6 * _nbytes((t_block, wcol), F32) + _nbytes((n_pairs, 128, 128), F32)
    resident = scratch_bytes + _nbytes((n_pairs, 128, 128), F32) + 24 * _nbytes((256, 256), F32)
    return pl.pallas_call(
        functools.partial(_wkv_chunk_kernel, n_chunks=n_chunks, n_pairs=n_pairs),
        grid=(B, n_col, T // t_block),
        in_specs=[tile] * 5 + [vec] * 5,
        out_specs=[tile, pl.BlockSpec((1, n_pairs, V7X_LANES, V7X_LANES), lambda b, p, t: (b, p, 0, 0))],
        out_shape=[jax.ShapeDtypeStruct((B, T, D), BF16),
                   jax.ShapeDtypeStruct((B, D // V7X_LANES, V7X_LANES, V7X_LANES), F32)],
        scratch_shapes=scratch,
        compiler_params=_params(("parallel", "parallel", "arbitrary"), pipelined, resident),
        name="wkv_chunked",
    )(r, lw, k, v, a, k_k, k_a, r_k, gn_w, gn_b)


def _wkv_step_kernel(s_ref, r_ref, lw_ref, k_ref, v_ref, a_ref, kk_ref, ka_ref, rk_ref, gw_ref, gb_ref,
                     snew_ref, y_ref):
    K = RWKV_HEAD
    S = s_ref[0]
    r, lw, k, v, a = (t[0].astype(F32) for t in (r_ref, lw_ref, k_ref, v_ref, a_ref))
    kkr = k * kk_ref[...]
    nrm = jnp.sqrt(jnp.sum(kkr * kkr, axis=-1, keepdims=True))
    kk = kkr / jnp.maximum(nrm, 1e-12)
    b = kk * a
    km = k * (1.0 + (a - 1.0) * ka_ref[...])
    w = jnp.exp(lw)
    eye = (lax.broadcasted_iota(jnp.int32, (K, K), 0)
           == lax.broadcasted_iota(jnp.int32, (K, K), 1)).astype(F32)
    row = lambda t: t[:, None, :]
    col = lambda t: jnp.sum(eye[None] * t[:, None, :], axis=-1, keepdims=True)
    sa = -jnp.sum(S * row(kk), axis=-1, keepdims=True)
    s_new = S * row(w) + sa * row(b) + col(v) * row(km)
    snew_ref[0] = s_new
    y_col = jnp.sum(s_new * row(r), axis=-1, keepdims=True)
    y = jnp.sum(y_col * eye[None], axis=1)
    mean = jnp.mean(y, axis=-1, keepdims=True)
    cen = y - mean
    var = jnp.mean(cen * cen, axis=-1, keepdims=True)
    yn = cen * lax.rsqrt(var + GN_EPS) * gw_ref[...] + gb_ref[...]
    bonus = jnp.sum(r * km * rk_ref[...], axis=-1, keepdims=True) * v
    y_ref[0] = yn + bonus


def _wkv_step_call(s0, r, lw, k, v, a, k_k, k_a, r_k, gn_w, gn_b):
    B, H, K, _ = s0.shape
    st = pl.BlockSpec((1, H, K, K), lambda b: (b, 0, 0, 0))
    vec = pl.BlockSpec((1, H, K), lambda b: (b, 0, 0))
    par = pl.BlockSpec((H, K), lambda b: (0, 0))
    pipelined = 2 * _nbytes((H, K, V7X_LANES), F32) + 6 * _nbytes((H, V7X_LANES), F32)
    resident = 6 * _nbytes((H, K, V7X_LANES), F32)
    return pl.pallas_call(
        _wkv_step_kernel,
        grid=(B,),
        in_specs=[st] + [vec] * 5 + [par] * 5,
        out_specs=[st, vec],
        out_shape=[jax.ShapeDtypeStruct(s0.shape, F32), jax.ShapeDtypeStruct((B, H, K), F32)],
        compiler_params=_params(("parallel",), pipelined, resident),
        name="wkv_step",
    )(s0, r, lw, k, v, a, k_k, k_a, r_k, gn_w, gn_b)


def _rwkv_out_kernel(y_ref, g_ref, x_ref, gate_ref, gain_ref, wout_ref, o_ref):
    g = g_ref[0].astype(F32)
    z = _dot(y_ref[0].astype(F32) * (g * _sigmoid(g)), wout_ref[...])
    o_ref[0] = x_ref[0] + gate_ref[0] * _rmsnorm(z, gain_ref[...])


def _rwkv_out_call(y, g, x, mod, gain, w_out, *, per_row, tm):
    B, T, D = x.shape
    tile = pl.BlockSpec((1, tm, D), lambda b, t: (b, t, 0))
    full = lambda shape: pl.BlockSpec(shape, lambda b, t: (0,) * len(shape))
    pipelined = 5 * _nbytes((tm, D), F32) + _nbytes((D, D), BF16)
    return pl.pallas_call(
        _rwkv_out_kernel,
        grid=(B, T // tm),
        in_specs=[tile, tile, tile, _mod_spec(per_row, tm, D, 2), full((1, D)), full((D, D))],
        out_specs=tile,
        out_shape=jax.ShapeDtypeStruct((B, T, D), F32),
        compiler_params=_params(("parallel", "parallel"), pipelined, 4 * _nbytes((tm, D), F32)),
        name="rwkv_out",
    )(y, g, x, mod, gain, w_out)


def _mla_proj_kernel(x_ref, shift_ref, scale_ref, kshift_ref, kscale_ref, gain_ref, kgain_ref,
                     cos_ref, sin_ref, wqa_ref, wgate_ref, wkv_ref, qnorm_ref, kvnorm_ref,
                     wqn_ref, wqp_ref, wqs_ref, wuk_ref,
                     q_ref, kcat_ref, ckv_ref, kpe_ref, sg_ref):
    x = x_ref[0]
    ms = jnp.mean(x * x, axis=-1, keepdims=True)
    xn = x * lax.rsqrt(ms + EPS)
    h = xn * gain_ref[...] * (1.0 + scale_ref[0]) + shift_ref[0]
    hk = xn * kgain_ref[...] * (1.0 + kscale_ref[0]) + kshift_ref[0]
    cos = cos_ref[...]
    sin = sin_ref[...]

    kv = _dot(hk, wkv_ref[...])
    ckv = _rmsnorm(kv[:, :KV_LORA], kvnorm_ref[...])
    kpe = kv[:, KV_LORA:KV_LORA + 128] * cos + kv[:, KV_LORA + 128:] * sin
    ckv_ref[0] = ckv
    kpe_ref[0] = kpe[:, :QK_ROPE]
    kcat_ref[0, :, :KV_LORA] = ckv.astype(BF16)
    kcat_ref[0, :, KV_LORA:] = kpe.astype(BF16)

    g = _dot(h, wgate_ref[...])
    sg_ref[0] = (g * _sigmoid(g)).astype(sg_ref.dtype)

    qn = _rmsnorm(_dot(h, wqa_ref[...]), qnorm_ref[...]).astype(BF16)
    q_nope = _dot(qn, wqn_ref[...])
    q_pe = _dot(qn, wqp_ref[...])
    q_ps = _dot(qn, wqs_ref[...])
    for hd in range(MLA_H):
        cols = slice(hd * 128, (hd + 1) * 128)
        q_lat = _dot(q_nope[:, cols], wuk_ref[hd])
        q_ref[0, hd, :, :KV_LORA] = (q_lat * Q_SCALE).astype(BF16)
        q_ref[0, hd, :, KV_LORA:] = ((q_pe[:, cols] * cos + q_ps[:, cols] * sin) * Q_SCALE).astype(BF16)


def _mla_proj_call(x, mod, kvmod, gain, kgain, cos, sin, wqa, wgate, wkv, qnorm, kvnorm,
                   wqn, wqp, wqs, wuk, *, per_row, tm):
    B, T, D = x.shape
    tile = lambda w: pl.BlockSpec((1, tm, w), lambda b, t: (b, t, 0))
    full = lambda shape: pl.BlockSpec(shape, lambda b, t: (0,) * len(shape))
    tab = pl.BlockSpec((tm, 128), lambda b, t: (t, 0))
    weights = (wqa, wgate, wkv, wqn, wqp, wqs, wuk)
    pipelined = (sum(_nbytes(w.shape, BF16) for w in weights) + 3 * _nbytes((tm, D), F32)
                 + _nbytes((MLA_H, tm, KCAT), BF16) + 2 * _nbytes((tm, KCAT), F32))
    return pl.pallas_call(
        _mla_proj_kernel,
        grid=(B, T // tm),
        in_specs=[tile(D), _mod_spec(per_row, tm, D, 0), _mod_spec(per_row, tm, D, 1),
                  _mod_spec(per_row, tm, D, 0), _mod_spec(per_row, tm, D, 1),
                  full((1, D)), full((1, D)), tab, tab,
                  full(wqa.shape), full(wgate.shape), full(wkv.shape), full((1, Q_LORA)), full((1, KV_LORA)),
                  full(wqn.shape), full(wqp.shape), full(wqs.shape), full(wuk.shape)],
        out_specs=[pl.BlockSpec((1, MLA_H, tm, KCAT), lambda b, t: (b, 0, t, 0)),
                   tile(KCAT), tile(KV_LORA), tile(QK_ROPE), tile(MLA_H * V_HEAD)],
        out_shape=[jax.ShapeDtypeStruct((B, MLA_H, T, KCAT), BF16),
                   jax.ShapeDtypeStruct((B, T, KCAT), BF16),
                   jax.ShapeDtypeStruct((B, T, KV_LORA), F32),
                   jax.ShapeDtypeStruct((B, T, QK_ROPE), F32),
                   jax.ShapeDtypeStruct((B, T, MLA_H * V_HEAD), BF16)],
        compiler_params=_params(("parallel", "parallel"), pipelined, 12 * _nbytes((tm, D), F32)),
        name="mla_proj",
    )(x, mod, mod, kvmod, kvmod, gain, kgain, cos, sin, wqa, wgate, wkv, qnorm, kvnorm,
      wqn, wqp, wqs, wuk)


def _lane_tile(t, width):
    return jnp.concatenate([t] * (width // V7X_LANES), axis=1)


def _flash_kernel(qi_ref, ki_ref, q_ref, k_ref, o_ref, m_scr, l_scr, acc_scr, *, tq):
    step = pl.program_id(1)
    qi = qi_ref[step]
    ki = ki_ref[step]

    @pl.when(ki == 0)
    def _():
        m_scr[...] = jnp.full_like(m_scr, -jnp.inf)
        l_scr[...] = jnp.zeros_like(l_scr)
        acc_scr[...] = jnp.zeros_like(acc_scr)

    def update(masked):
        kc = k_ref[0]
        vc = kc[:, :KV_LORA]
        scores = lambda hd: lax.dot_general(q_ref[0, hd], kc, _NT, preferred_element_type=F32)
        if masked:
            causal = (lax.broadcasted_iota(jnp.int32, (tq, tq), 1)
                      <= lax.broadcasted_iota(jnp.int32, (tq, tq), 0))
        groups = [range(g, g + FLASH_HEAD_GROUP) for g in range(0, MLA_H, FLASH_HEAD_GROUP)]
        s_next = [scores(hd) for hd in groups[0]]
        for gi, heads in enumerate(groups):
            s = s_next
            if gi + 1 < len(groups):
                s_next = [scores(hd) for hd in groups[gi + 1]]
            if masked:
                s = [jnp.where(causal, t, -jnp.inf) for t in s]
            m_prev = [m_scr[hd] for hd in heads]
            m_new = [jnp.maximum(mp, jnp.max(t, axis=-1, keepdims=True)) for mp, t in zip(m_prev, s)]
            alpha = [jnp.exp2(mp - mn) for mp, mn in zip(m_prev, m_new)]
            p = [jnp.exp2(t - _lane_tile(mn, tq)) for t, mn in zip(s, m_new)]
            pv = [jnp.dot(t.astype(BF16), vc, preferred_element_type=F32) for t in p]
            for i, hd in enumerate(heads):
                l_scr[hd] = alpha[i] * l_scr[hd] + jnp.sum(p[i], axis=-1, keepdims=True)
                acc_scr[hd] = _lane_tile(alpha[i], KV_LORA) * acc_scr[hd] + pv[i]
                m_scr[hd] = m_new[i]

    @pl.when(ki < qi)
    def _():
        update(False)

    @pl.when(ki == qi)
    def _():
        update(True)
        for hd in range(MLA_H):
            o_ref[0, hd] = (acc_scr[hd] * _lane_tile(1.0 / l_scr[hd], KV_LORA)).astype(BF16)


def _flash_call(q, kcat, *, tq):
    B, H, T, _ = q.shape
    nq = T // tq
    pairs = [(i, j) for i in range(nq) for j in range(i + 1)]
    qi_tab = jnp.asarray([i for i, _ in pairs], jnp.int32)
    ki_tab = jnp.asarray([j for _, j in pairs], jnp.int32)
    pipelined = (_nbytes((H, tq, KCAT), BF16) + _nbytes((tq, KCAT), BF16)
                 + _nbytes((H, tq, KV_LORA), BF16))
    resident = _nbytes((H, tq, KV_LORA + 2 * V7X_LANES), F32) + 8 * _nbytes((tq, tq), F32)
    return pl.pallas_call(
        functools.partial(_flash_kernel, tq=tq),
        grid_spec=pltpu.PrefetchScalarGridSpec(
            num_scalar_prefetch=2,
            grid=(B, len(pairs)),
            in_specs=[pl.BlockSpec((1, H, tq, KCAT), lambda b, s, qt, kt: (b, 0, qt[s], 0)),
                      pl.BlockSpec((1, tq, KCAT), lambda b, s, qt, kt: (b, kt[s], 0))],
            out_specs=pl.BlockSpec((1, H, tq, KV_LORA), lambda b, s, qt, kt: (b, 0, qt[s], 0)),
            scratch_shapes=[pltpu.VMEM((H, tq, V7X_LANES), F32), pltpu.VMEM((H, tq, V7X_LANES), F32),
                            pltpu.VMEM((H, tq, KV_LORA), F32)]),
        out_shape=jax.ShapeDtypeStruct((B, H, T, KV_LORA), BF16),
        compiler_params=_params(("parallel", "arbitrary"), pipelined, resident),
        name="mla_flash",
    )(qi_tab, ki_tab, q, kcat)


DECODE_PAGES = 32


def _decode_kernel(pt_ref, q_ref, cnew_ref, pnew_ref, *refs):
    del pt_ref
    ck_refs = refs[:DECODE_PAGES]
    kp_refs = refs[DECODE_PAGES:2 * DECODE_PAGES]
    o_ref, m_scr, l_scr, acc_scr, kbuf, pbuf = refs[2 * DECODE_PAGES:]
    j = pl.program_id(1)
    q = q_ref[0]
    q_lat = q[:, :KV_LORA]
    q_pe = q[:, KV_LORA:KV_LORA + QK_ROPE]

    @pl.when(j == 0)
    def _():
        cn = cnew_ref[0]
        s_new = (jnp.sum(q_lat.astype(F32) * cn, axis=-1, keepdims=True)
                 + jnp.sum(q_pe.astype(F32) * pnew_ref[0], axis=-1, keepdims=True))
        m_scr[...] = s_new
        l_scr[...] = jnp.ones_like(l_scr)
        acc_scr[...] = jnp.broadcast_to(cn, acc_scr.shape)

    ps = ck_refs[0].shape[1]
    for i in range(DECODE_PAGES):
        kbuf[i * ps:(i + 1) * ps, :] = ck_refs[i][0].astype(BF16)
        pbuf[:, i * ps:(i + 1) * ps] = kp_refs[i][0].astype(BF16)
    keys = kbuf[...]
    s = (lax.dot_general(q_lat, keys, _NT, preferred_element_type=F32)
         + jnp.dot(q_pe, pbuf[...], preferred_element_type=F32))
    m_prev = m_scr[...]
    m_new = jnp.maximum(m_prev, jnp.max(s, axis=-1, keepdims=True))
    alpha = jnp.exp2(m_prev - m_new)
    p = jnp.exp2(s - m_new)
    l_scr[...] = alpha * l_scr[...] + jnp.sum(p, axis=-1, keepdims=True)
    acc_scr[...] = alpha * acc_scr[...] + jnp.dot(p.astype(BF16), keys, preferred_element_type=F32)
    m_scr[...] = m_new

    @pl.when(j == pl.num_programs(1) - 1)
    def _():
        o_ref[0] = acc_scr[...] / l_scr[...]


def _decode_call(page_table, q, c_new, p_new, cache_ckv, cache_kpe):
    B, H, _ = q.shape
    n_pages = page_table.shape[1]
    ps = cache_ckv.shape[1]
    assert n_pages % DECODE_PAGES == 0
    steps = n_pages // DECODE_PAGES

    def page_spec(rows, width, i):
        return pl.BlockSpec((1, rows, width), lambda b, j, pt: (pt[b, j * DECODE_PAGES + i], 0, 0))

    in_specs = ([pl.BlockSpec((1, H, KCAT), lambda b, j, pt: (b, 0, 0)),
                 pl.BlockSpec((1, 1, KV_LORA), lambda b, j, pt: (b, 0, 0)),
                 pl.BlockSpec((1, 1, QK_ROPE), lambda b, j, pt: (b, 0, 0))]
                + [page_spec(ps, KV_LORA, i) for i in range(DECODE_PAGES)]
                + [page_spec(QK_ROPE, ps, i) for i in range(DECODE_PAGES)])
    pipelined = DECODE_PAGES * (_nbytes((ps, KV_LORA), F32) + _nbytes((QK_ROPE, ps), F32))
    resident = DECODE_PAGES * _nbytes((ps, KV_LORA), F32)
    return pl.pallas_call(
        _decode_kernel,
        grid_spec=pltpu.PrefetchScalarGridSpec(
            num_scalar_prefetch=1,
            grid=(B, steps),
            in_specs=in_specs,
            out_specs=pl.BlockSpec((1, H, KV_LORA), lambda b, j, pt: (b, 0, 0)),
            scratch_shapes=[pltpu.VMEM((H, 1), F32), pltpu.VMEM((H, 1), F32),
                            pltpu.VMEM((H, KV_LORA), F32),
                            pltpu.VMEM((DECODE_PAGES * ps, KV_LORA), BF16),
                            pltpu.VMEM((QK_ROPE, DECODE_PAGES * ps), BF16)]),
        out_shape=jax.ShapeDtypeStruct((B, H, KV_LORA), F32),
        compiler_params=_params(("parallel", "arbitrary"), pipelined, resident),
        name="mla_decode",
    )(page_table, q, c_new, p_new, *([cache_ckv] * DECODE_PAGES), *([cache_kpe] * DECODE_PAGES))


def _mla_out_kernel(o_ref, sg_ref, x_ref, gate_ref, gain_ref, wuv_ref, wout_ref, y_ref, og_scr):
    for hd in range(MLA_H):
        cols = slice(hd * V_HEAD, (hd + 1) * V_HEAD)
        o = _dot(o_ref[0, hd], wuv_ref[hd])
        og_scr[:, cols] = (o * sg_ref[0, :, cols].astype(F32)).astype(BF16)
    z = _dot(og_scr[...], wout_ref[...])
    y_ref[0] = x_ref[0] + gate_ref[0] * _rmsnorm(z, gain_ref[...])


def _mla_out_call(o_lat, sg, x, mod, gain, wuv, wout, *, per_row, tm):
    B, T, D = x.shape
    tile = lambda w: pl.BlockSpec((1, tm, w), lambda b, t: (b, t, 0))
    full = lambda shape: pl.BlockSpec(shape, lambda b, t: (0,) * len(shape))
    pipelined = (_nbytes((MLA_H, tm, KV_LORA), BF16) + 4 * _nbytes((tm, D), F32)
                 + _nbytes(wuv.shape, BF16) + _nbytes(wout.shape, BF16))
    return pl.pallas_call(
        _mla_out_kernel,
        grid=(B, T // tm),
        in_specs=[pl.BlockSpec((1, MLA_H, tm, KV_LORA), lambda b, t: (b, 0, t, 0)),
                  tile(MLA_H * V_HEAD), tile(D), _mod_spec(per_row, tm, D, 2), full((1, D)),
                  full(wuv.shape), full(wout.shape)],
        out_specs=tile(D),
        out_shape=jax.ShapeDtypeStruct((B, T, D), F32),
        scratch_shapes=[pltpu.VMEM((tm, MLA_H * V_HEAD), BF16)],
        compiler_params=_params(("parallel", "parallel"), pipelined, 4 * _nbytes((tm, D), F32)),
        name="mla_out",
    )(o_lat, sg, x, mod, gain, wuv, wout)


def _rope_tables(pos):
    half = QK_ROPE // 2
    inv = ROPE_THETA ** (-jnp.arange(half, dtype=F32) / half)
    ang = pos.astype(F32)[:, None] * inv[None, :]
    c, s = jnp.cos(ang), jnp.sin(ang)
    z = jnp.zeros((pos.shape[0], 128 - QK_ROPE), F32)
    return jnp.concatenate([c, c, z], axis=1), jnp.concatenate([-s, s, z], axis=1)


def _swap_halves(w):
    half = w.shape[-1] // 2
    return jnp.concatenate([w[..., half:], w[..., :half]], axis=-1)


def _pad_lanes(w):
    return jnp.concatenate([w, jnp.zeros(w.shape[:-1] + (128 - w.shape[-1],), w.dtype)], axis=-1)


def kernel(x_prompt, x_sample, c_prompt, c_sample, state_wkv, state_shift, cache_kv_latent, cache_k_rope, page_table, ada_w, ada_b, norm_pre, norm_post, a_mu, a_w_in, a_w0, a_w1, a_w2, a_a0, a_a1, a_a2, a_k_k, a_k_a, a_r_k, a_gn_w, a_gn_b, a_w_out, kv_ada_w, kv_ada_b, kv_norm, kv_w_a, kv_a_norm, kv_w_b, b_w_in, b_q_norm, b_w_q, b_w_out):
    B, T, D = x_prompt.shape
    DB = x_sample.shape[0]
    H = D // RWKV_HEAD
    assert ada_w.shape[0] == 2 and a_mu.shape[0] == 1 and b_w_in.shape[0] == 1
    assert x_sample.shape[1] == 1 and T % ROW_TILE == 0 and DB % V7X_SUBLANES == 0

    c_all = jnp.concatenate([c_prompt, c_sample], axis=0)
    mods = _ada_call(c_all, ada_w, ada_b)
    kvmods = _ada_call(c_all, kv_ada_w[None], kv_ada_b[None])
    mod_p = [mods[i, :B].reshape(B, 1, 3 * D) for i in range(2)]
    mod_s = [mods[i, B:].reshape(1, DB, 3 * D) for i in range(2)]
    kvmod_p = kvmods[0, :B].reshape(B, 1, 2 * D)
    kvmod_s = kvmods[0, B:].reshape(1, DB, 2 * D)

    row = lambda v: v.reshape(1, -1)
    w_in = a_w_in[0].astype(BF16)
    a_args = (row(norm_pre[0]), a_mu[0], w_in, row(a_w0[0]), a_w1[0].astype(BF16), a_w2[0].astype(BF16),
              row(a_a0[0]), a_a1[0].astype(BF16), a_a2[0].astype(BF16))
    w_out_a = a_w_out[0].astype(BF16)
    k_k, k_a, r_k = row(a_k_k[0]), row(a_k_a[0]), row(a_r_k[0])
    gn_w, gn_b = row(a_gn_w[0]), row(a_gn_b[0])

    w_bin = b_w_in[0]
    wqa = w_bin[:, :Q_LORA].astype(BF16)
    wgate = w_bin[:, Q_LORA:].astype(BF16)
    kv_pe = kv_w_a[:, KV_LORA:]
    wkv = jnp.concatenate([kv_w_a[:, :KV_LORA], _pad_lanes(kv_pe), _pad_lanes(_swap_halves(kv_pe))],
                          axis=1).astype(BF16)
    w_q = b_w_q[0]
    wqn = w_q[:, :, :QK_NOPE].reshape(Q_LORA, MLA_H * QK_NOPE).astype(BF16)
    wq_pe = w_q[:, :, QK_NOPE:]
    wqp = _pad_lanes(wq_pe).reshape(Q_LORA, MLA_H * 128).astype(BF16)
    wqs = _pad_lanes(_swap_halves(wq_pe)).reshape(Q_LORA, MLA_H * 128).astype(BF16)
    wuk = jnp.transpose(kv_w_b[:, :, :QK_NOPE], (1, 2, 0)).astype(BF16)
    wuv = jnp.transpose(kv_w_b[:, :, QK_NOPE:], (1, 0, 2)).astype(BF16)
    wout_b = b_w_out[0].astype(BF16)
    b_args = (wqa, wgate, wkv, row(b_q_norm[0]), row(kv_a_norm), wqn, wqp, wqs, wuk)

    tm = ROW_TILE
    r, lw, k, v, a, g, h_last = _rwkv_proj_call(x_prompt, x_prompt, mod_p[0], *a_args,
                                                seq_shift=True, tm=tm)
    yw, s_packed = _wkv_chunk_call(r, lw, k, v, a, k_k, k_a, r_k, gn_w, gn_b,
                                   t_block=min(T, 256), n_pairs=8)
    x1 = _rwkv_out_call(yw, g, x_prompt, mod_p[0], row(norm_post[0]), w_out_a, per_row=False, tm=tm)
    s_packed = s_packed.reshape(B, H // 2, 2, RWKV_HEAD, 2, RWKV_HEAD)
    wkv_p = jnp.stack([s_packed[:, :, 0, :, 0, :], s_packed[:, :, 1, :, 1, :]], axis=2)
    wkv_p = wkv_p.reshape(1, B, H, RWKV_HEAD, RWKV_HEAD)
    shift_p = h_last.reshape(1, B, D)

    cos_p, sin_p = _rope_tables(jnp.arange(T, dtype=jnp.int32))
    q_p, kcat_p, ckv_p, kpe_p, sg_p = _mla_proj_call(
        x1, mod_p[1], kvmod_p, row(norm_pre[1]), row(kv_norm), cos_p, sin_p, *b_args,
        per_row=False, tm=tm)
    o_p = _flash_call(q_p, kcat_p, tq=min(T, 256))
    y_prompt = _mla_out_call(o_p, sg_p, x1, mod_p[1], row(norm_post[1]), wuv, wout_b,
                             per_row=False, tm=tm)

    xs = x_sample.reshape(1, DB, D)
    rs, lws, ks, vs, as_, gs, hs = _rwkv_proj_call(xs, state_shift[0].reshape(1, DB, D), mod_s[0],
                                                   *a_args, seq_shift=False, tm=DB)
    hk = lambda t: t.reshape(DB, H, RWKV_HEAD)
    pk = lambda t: t.reshape(H, RWKV_HEAD)
    s_new, yws = _wkv_step_call(state_wkv[0], hk(rs), hk(lws), hk(ks), hk(vs), hk(as_),
                                pk(k_k), pk(k_a), pk(r_k), pk(gn_w), pk(gn_b))
    x1s = _rwkv_out_call(yws.reshape(1, DB, D), gs, xs, mod_s[0], row(norm_post[0]), w_out_a,
                         per_row=True, tm=DB)
    n_pages = page_table.shape[1]
    past_len = n_pages * cache_kv_latent.shape[1]
    cos_s, sin_s = _rope_tables(jnp.full((DB,), past_len, dtype=jnp.int32))
    q_s, _, ckv_s, kpe_s, sg_s = _mla_proj_call(
        x1s, mod_s[1], kvmod_s, row(norm_pre[1]), row(kv_norm), cos_s, sin_s, *b_args,
        per_row=True, tm=DB)
    o_s = _decode_call(page_table, jnp.transpose(q_s[0], (1, 0, 2)),
                       ckv_s.reshape(DB, 1, KV_LORA), kpe_s.reshape(DB, 1, QK_ROPE),
                       cache_kv_latent, jnp.swapaxes(cache_k_rope, 1, 2))
    o_s = jnp.transpose(o_s, (1, 0, 2)).astype(BF16)[None]
    y_s = _mla_out_call(o_s, sg_s, x1s, mod_s[1], row(norm_post[1]), wuv, wout_b,
                        per_row=True, tm=DB)

    return (y_prompt, y_s.reshape(DB, 1, D), wkv_p, shift_p, ckv_p, kpe_p,
            s_new[None], hs.reshape(1, DB, D), ckv_s.reshape(DB, 1, KV_LORA),
            kpe_s.reshape(DB, 1, QK_ROPE))
```

```python
import functools
import math

import jax
import jax.numpy as jnp
from jax import lax
from jax.experimental import pallas as pl
from jax.experimental.pallas import tpu as pltpu

F32 = jnp.float32
BF16 = jnp.bfloat16
HIGHEST = lax.Precision.HIGHEST

RWKV_HEAD = 64
GN_EPS = 64e-5
EPS = 1e-6
MLA_H = 8
QK_NOPE = 128
QK_ROPE = 64
V_HEAD = 128
Q_LORA = 384
KV_LORA = 256
ROPE_THETA = 10000.0
ATTN_SCALE = (QK_NOPE + QK_ROPE) ** -0.5
Q_SCALE = ATTN_SCALE * math.log2(math.e)
KCAT = KV_LORA + 128

V7X_LANES = 128
V7X_SUBLANES = 8
V7X_VMEM_BYTES = 64 * 1024 * 1024
V7X_VMEM_REQUEST_CAP = V7X_VMEM_BYTES - 8 * 1024 * 1024

WKV_CHUNK = 64
ROW_TILE = 512
FLASH_HEAD_GROUP = 2


def _vmem_limit(pipelined_bytes, resident_bytes=0):
    est = 2 * pipelined_bytes + resident_bytes + 4 * 1024 * 1024
    return int(min(max(est, 16 * 1024 * 1024), V7X_VMEM_REQUEST_CAP))


def _nbytes(shape, dtype):
    return math.prod(shape) * jnp.dtype(dtype).itemsize


def _params(sem, pipelined_bytes, resident_bytes=0):
    return pltpu.CompilerParams(
        dimension_semantics=sem,
        vmem_limit_bytes=_vmem_limit(pipelined_bytes, resident_bytes))


def _dot(a, b):
    return jnp.dot(a.astype(BF16), b.astype(BF16), preferred_element_type=F32)


def _dot_hi(a, b, dims=(((1,), (0,)), ((), ()))):
    return lax.dot_general(a, b, dims, precision=HIGHEST, preferred_element_type=F32)


_NT = (((1,), (1,)), ((), ()))
_TN = (((0,), (0,)), ((), ()))


def _sigmoid(x):
    return 1.0 / (1.0 + jnp.exp(-x))


def _ada_kernel(c_ref, w_ref, b_ref, o_ref):
    o_ref[0] = _dot_hi(c_ref[...], w_ref[0]) + b_ref[0]


def _ada_call(c, w, b):
    G, D, N = w.shape
    M = c.shape[0]
    tn = 1024
    pipelined = _nbytes((D, tn), F32) + _nbytes((M, tn), F32) + _nbytes((M, D), F32)
    return pl.pallas_call(
        _ada_kernel,
        grid=(G, N // tn),
        in_specs=[
            pl.BlockSpec((M, D), lambda g, j: (0, 0)),
            pl.BlockSpec((1, D, tn), lambda g, j: (g, 0, j)),
            pl.BlockSpec((1, 1, tn), lambda g, j: (g, 0, j)),
        ],
        out_specs=pl.BlockSpec((1, M, tn), lambda g, j: (g, 0, j)),
        out_shape=jax.ShapeDtypeStruct((G, M, N), F32),
        compiler_params=_params(("parallel", "parallel"), pipelined),
        name="ada_modulation",
    )(c, w, b.reshape(G, 1, N))


def _const_spec(shape):
    return pl.BlockSpec(shape, lambda *_: (0,) * len(shape), pipeline_mode=pl.Buffered(1))


def _mod_spec(per_row, tm, D, col):
    if per_row:
        return pl.BlockSpec((1, tm, D), lambda b, t: (b, t, col))
    return pl.BlockSpec((1, 1, D), lambda b, t: (b, 0, col))


def _modnorm(x, gain, scale, shift):
    ms = jnp.mean(x * x, axis=-1, keepdims=True)
    return x * lax.rsqrt(ms + EPS) * gain * (1.0 + scale) + shift


def _rmsnorm(x, gain):
    ms = jnp.mean(x * x, axis=-1, keepdims=True)
    return x * lax.rsqrt(ms + EPS) * gain


def _rwkv_proj_kernel(x_ref, prev_ref, shift_ref, scale_ref, gain_ref, mu_ref, win_ref,
                      w0_ref, w1_ref, w2_ref, a0_ref, a1_ref, a2_ref,
                      r_ref, lw_ref, k_ref, v_ref, a_ref, g_ref, hlast_ref, *, seq_shift):
    x = x_ref[0]
    gain = gain_ref[...]
    scale = scale_ref[0]
    shift = shift_ref[0]
    h = _modnorm(x, gain, scale, shift)
    tm = h.shape[0]
    if seq_shift:
        hp = _modnorm(prev_ref[0][V7X_SUBLANES - 1:V7X_SUBLANES, :], gain, scale, shift)
        hp = jnp.where(pl.program_id(1) == 0, 0.0, hp)
        row = lax.broadcasted_iota(jnp.int32, (tm, 1), 0)
        hs = jnp.where(row == 0, hp, pltpu.roll(h, 1, axis=0))
        hlast_ref[0] = h[tm - 1:tm, :]
    else:
        hs = prev_ref[0]
        hlast_ref[0] = h
    xx = hs - h
    mu = mu_ref[...]
    outs = (r_ref, k_ref, v_ref, g_ref)
    for m in range(4):
        xm = h + xx * mu[m:m + 1, :]
        outs[m][0] = _dot(xm, win_ref[m]).astype(outs[m].dtype)
    xw = h + xx * mu[4:5, :]
    xa = h + xx * mu[5:6, :]
    wl = w0_ref[...] + _dot(jnp.tanh(_dot(xw, w1_ref[...])), w2_ref[...])
    z = -wl
    softplus = jnp.maximum(z, 0.0) + jnp.log(1.0 + jnp.exp(-jnp.abs(z)))
    lw_ref[0] = -jnp.exp(-softplus - 0.5)
    al = a0_ref[...] + _dot(_dot(xa, a1_ref[...]), a2_ref[...])
    a_ref[0] = _sigmoid(al).astype(a_ref.dtype)


def _rwkv_proj_call(x, prev, mod, gain, mu, w_in, w0, w1, w2, a0, a1, a2, *, seq_shift, tm):
    B, T, D = x.shape
    per_row = not seq_shift
    nt = T // tm
    tile = pl.BlockSpec((1, tm, D), lambda b, t: (b, t, 0))
    if seq_shift:
        sub = tm // V7X_SUBLANES
        prev_spec = pl.BlockSpec((1, V7X_SUBLANES, D),
                                 lambda b, t: (b, jnp.maximum(t * sub - 1, 0), 0))
        hlast_shape = jax.ShapeDtypeStruct((B, 1, D), F32)
        hlast_spec = pl.BlockSpec((1, 1, D), lambda b, t: (b, 0, 0))
        sem = ("parallel", "arbitrary")
    else:
        prev_spec = tile
        hlast_shape = jax.ShapeDtypeStruct((B, T, D), F32)
        hlast_spec = tile
        sem = ("parallel", "parallel")
    full = _const_spec
    lora = w1.shape[1]
    pipelined = 5 * _nbytes((tm, D), F32)
    resident = (8 * _nbytes((tm, D), F32) + _nbytes((4, D, D), BF16)
                + 4 * _nbytes((D, V7X_LANES), BF16))
    out_sds = lambda dt: jax.ShapeDtypeStruct((B, T, D), dt)
    return pl.pallas_call(
        functools.partial(_rwkv_proj_kernel, seq_shift=seq_shift),
        grid=(B, nt),
        in_specs=[tile, prev_spec, _mod_spec(per_row, tm, D, 0), _mod_spec(per_row, tm, D, 1),
                  full((1, D)), full((6, D)), full((4, D, D)),
                  full((1, D)), full((D, lora)), full((lora, D)),
                  full((1, D)), full((D, lora)), full((lora, D))],
        out_specs=[tile] * 6 + [hlast_spec],
        out_shape=[out_sds(BF16), out_sds(F32)] + [out_sds(BF16)] * 4 + [hlast_shape],
        compiler_params=_params(sem, pipelined, resident),
        name="rwkv_proj",
    )(x, prev, mod, mod, gain, mu, w_in, w0, w1, w2, a0, a1, a2)


def _split(x):
    hi = x.astype(BF16)
    return hi, (x - hi.astype(F32)).astype(BF16)


def _dot_split(a, b, dims=(((1,), (0,)), ((), ()))):
    (ah, al), (bh, bl) = a, b
    dot = lambda x, y: lax.dot_general(x, y, dims, preferred_element_type=F32)
    return dot(ah, bh) + dot(ah, bl) + dot(al, bh)


def _cumsum_rows(cum, x):
    hi = x.astype(BF16)
    rest = x - hi.astype(F32)
    mid = rest.astype(BF16)
    lo = (rest - mid.astype(F32)).astype(BF16)
    dot = lambda t: jnp.dot(cum, t, preferred_element_type=F32)
    return dot(hi) + dot(mid) + dot(lo)


def _wkv_chunk_kernel(r_ref, lw_ref, k_ref, v_ref, a_ref, kk_ref, ka_ref, rk_ref, gw_ref, gb_ref,
                      y_ref, sfin_ref, s_scr, wr_scr, tinv_scr, av_scr, ar_scr, bkp_scr, v2_scr, pend_scr,
                      y2_scr, bias_scr, *, n_chunks, n_pairs):
    L = WKV_CHUNK
    L2 = 2 * L
    tc = pl.program_id(2)

    @pl.when(tc == 0)
    def _():
        s_scr[...] = jnp.zeros_like(s_scr)

    lane = lax.broadcasted_iota(jnp.int32, (L2, V7X_LANES), 1)
    srow = lax.broadcasted_iota(jnp.int32, (L2, V7X_LANES), 0)
    own = (srow < L) == (lane < RWKV_HEAD)
    ti = lax.broadcasted_iota(jnp.int32, (L, L), 0)
    tj = lax.broadcasted_iota(jnp.int32, (L, L), 1)
    cum = (ti >= tj).astype(BF16)
    ri = lax.broadcasted_iota(jnp.int32, (L2, L2), 0)
    rj = lax.broadcasted_iota(jnp.int32, (L2, L2), 1)
    strict = ri > rj
    incl = ri >= rj
    eye = (ri == rj).astype(F32)
    n_rounds = int(math.log2(L)) - 1

    def stack(x):
        return jnp.where(own, jnp.concatenate([x, x], axis=0), 0.0)

    def fold(x2):
        return x2[:L] + x2[L:]

    def precompute(c, carry):
        t0 = pl.multiple_of(c * L, L)
        pairs = range(n_pairs)
        cols = [slice(p * V7X_LANES, (p + 1) * V7X_LANES) for p in pairs]
        load = lambda ref: [ref[0, pl.ds(t0, L), cols[p]].astype(F32) for p in pairs]
        r, lw, k, v, a = load(r_ref), load(lw_ref), load(k_ref), load(v_ref), load(a_ref)

        r2 = [stack(r[p]) for p in pairs]
        v2 = [stack(v[p]) for p in pairs]
        a2 = [stack(a[p]) for p in pairs]
        kkr = [stack(k[p] * kk_ref[:, cols[p]]) for p in pairs]
        nrm = [jnp.sqrt(jnp.sum(kkr[p] * kkr[p], axis=-1, keepdims=True)) for p in pairs]
        kk2 = [kkr[p] / jnp.maximum(nrm[p], 1e-12) for p in pairs]
        b2 = [kk2[p] * a2[p] for p in pairs]
        km2 = [stack(k[p] * (1.0 + (a[p] - 1.0) * ka_ref[:, cols[p]])) for p in pairs]

        cs = [_cumsum_rows(cum, lw[p]) for p in pairs]
        cs_end = [cs[p][L - 1:L, :] for p in pairs]
        cs2 = [jnp.concatenate([cs[p], cs[p]], axis=0) for p in pairs]
        lw2 = [jnp.concatenate([lw[p], lw[p]], axis=0) for p in pairs]
        wr = [jnp.concatenate([kk2[p] * jnp.exp(cs2[p] - lw2[p]), r2[p] * jnp.exp(cs2[p])],
                              axis=0).astype(BF16) for p in pairs]
        e_neg = [jnp.exp(-cs2[p]) for p in pairs]
        bk = [jnp.concatenate([b2[p] * e_neg[p], km2[p] * e_neg[p]], axis=0).astype(BF16) for p in pairs]
        aa = [lax.dot_general(wr[p], bk[p], _NT, preferred_element_type=F32) for p in pairs]

        m0 = [jnp.where(strict, -aa[p][:L2, :L2], 0.0) for p in pairs]
        m = m0
        x = [eye for p in pairs]
        for it in range(n_rounds):
            if it + 1 < n_rounds:
                mx = [_dot(m[p], jnp.concatenate([m[p], x[p]], axis=1)) for p in pairs]
                x = [x[p] + mx[p][:, L2:] for p in pairs]
                m = [mx[p][:, :L2] for p in pairs]
            else:
                x = [x[p] + _dot(m[p], x[p]) for p in pairs]
        res = [eye - x[p] + _dot_split(_split(m0[p]), _split(x[p])) for p in pairs]
        x = [x[p] + _dot(x[p], res[p]) for p in pairs]
        av = [_dot(jnp.where(strict, aa[p][:L2, L2:], 0.0), v2[p]) for p in pairs]

        for p in pairs:
            e_end = jnp.exp(cs_end[p] - cs2[p])
            bkp_hi, bkp_lo = _split(jnp.concatenate([b2[p] * e_end, km2[p] * e_end], axis=0))
            v2_hi, v2_lo = _split(v2[p])
            wr_scr[c, p] = wr[p]
            tinv_scr[c, p] = x[p].astype(BF16)
            av_scr[c, p] = av[p]
            ar_scr[c, p] = jnp.where(jnp.concatenate([incl, incl], axis=1), aa[p][L2:, :], 0.0).astype(BF16)
            bkp_scr[c, p, 0] = bkp_hi
            bkp_scr[c, p, 1] = bkp_lo
            v2_scr[c, p, 0] = v2_hi
            v2_scr[c, p, 1] = v2_lo
            pend_scr[c, p] = jnp.broadcast_to(jnp.exp(cs_end[p]), (V7X_SUBLANES, V7X_LANES))
            bonus = fold(jnp.sum(r2[p] * km2[p] * rk_ref[:, cols[p]], axis=-1, keepdims=True) * v2[p])
            bias_scr[c, p] = gb_ref[:, cols[p]] + bonus
        return carry

    def recur(c, carry):
        pairs = range(n_pairs)
        s0 = [s_scr[p] for p in pairs]
        g = [lax.dot_general(wr_scr[c, p], s0[p].astype(BF16), _NT, preferred_element_type=F32)
             for p in pairs]
        u = [_dot(tinv_scr[c, p], -(g[p][:L2] + av_scr[c, p])) for p in pairs]
        us = [_split(u[p]) for p in pairs]
        uv = [tuple(jnp.concatenate([us[p][i], v2_scr[c, p, i]], axis=0) for i in range(2)) for p in pairs]
        for p in pairs:
            s_scr[p] = (s0[p] * pend_scr[c, p][0:1, :]
                        + _dot_split(uv[p], (bkp_scr[c, p, 0], bkp_scr[c, p, 1]), _TN))
        for p in pairs:
            y2_scr[c, p] = g[p][L2:] + jnp.dot(ar_scr[c, p], uv[p][0], preferred_element_type=F32)
        return carry

    def normalise(c, carry):
        t0 = pl.multiple_of(c * L, L)
        pairs = range(n_pairs)
        y2 = [y2_scr[c, p] for p in pairs]
        mean = [jnp.sum(y2[p], axis=-1, keepdims=True) * (1.0 / RWKV_HEAD) for p in pairs]
        cen = [jnp.where(own, y2[p] - mean[p], 0.0) for p in pairs]
        var = [jnp.sum(cen[p] * cen[p], axis=-1, keepdims=True) * (1.0 / RWKV_HEAD) for p in pairs]
        for p in pairs:
            cols = slice(p * V7X_LANES, (p + 1) * V7X_LANES)
            yn = fold(cen[p] * lax.rsqrt(var[p] + GN_EPS))
            y_ref[0, pl.ds(t0, L), cols] = (bias_scr[c, p] + yn * gw_ref[:, cols]).astype(y_ref.dtype)
        return carry

    lax.fori_loop(0, n_chunks, precompute, 0)
    lax.fori_loop(0, n_chunks, recur, 0)
    lax.fori_loop(0, n_chunks, normalise, 0)

    @pl.when(tc == pl.num_programs(2) - 1)
    def _():
        for p in range(n_pairs):
            s = s_scr[p]
            sfin_ref[0, 2 * p] = s[:RWKV_HEAD, :RWKV_HEAD]
            sfin_ref[0, 2 * p + 1] = s[RWKV_HEAD:, RWKV_HEAD:]


def _wkv_chunk_call(r, lw, k, v, a, k_k, k_a, r_k, gn_w, gn_b, *, t_block, n_pairs):
    B, T, D = r.shape
    L2 = 2 * WKV_CHUNK
    n_chunks = t_block // WKV_CHUNK
    wcol = n_pairs * V7X_LANES
    n_col = D // wcol
    tile = pl.BlockSpec((1, t_block, wcol), lambda b, p, t: (b, t, p))
    vec = pl.BlockSpec((1, wcol), lambda b, p, t: (0, p))
    per_chunk = lambda shape, dt: ((n_chunks, n_pairs) + shape, dt)
    per_chunk_scratch = [
        per_chunk((2 * L2, V7X_LANES), BF16),
        per_chunk((L2, L2), BF16),
        per_chunk((L2, V7X_LANES), F32),
        per_chunk((L2, 2 * L2), BF16),
        per_chunk((2, 2 * L2, V7X_LANES), BF16),
        per_chunk((2, L2, V7X_LANES), BF16),
        per_chunk((V7X_SUBLANES, V7X_LANES), F32),
        per_chunk((L2, V7X_LANES), F32),
        per_chunk((WKV_CHUNK, V7X_LANES), F32)]
    scratch = ([pltpu.VMEM((n_pairs, V7X_LANES, V7X_LANES), F32)]
               + [pltpu.VMEM(shape, dt) for shape, dt in per_chunk_scratch])
    scratch_bytes = sum(_nbytes(shape, dt) for shape, dt in per_chunk_scratch)
    pipelined = 6 * _nbytes((t_block, wcol), F32) + _nbytes((n_pairs, 128, 128), F32)
    resident = scratch_bytes + _nbytes((n_pairs, 128, 128), F32) + 24 * _nbytes((256, 256), F32)
    return pl.pallas_call(
        functools.partial(_wkv_chunk_kernel, n_chunks=n_chunks, n_pairs=n_pairs),
        grid=(B, n_col, T // t_block),
        in_specs=[tile] * 5 + [vec] * 5,
        out_specs=[tile, pl.BlockSpec((1, 2 * n_pairs, RWKV_HEAD, RWKV_HEAD), lambda b, p, t: (b, p, 0, 0))],
        out_shape=[jax.ShapeDtypeStruct((B, T, D), BF16),
                   jax.ShapeDtypeStruct((B, D // RWKV_HEAD, RWKV_HEAD, RWKV_HEAD), F32)],
        scratch_shapes=scratch,
        compiler_params=_params(("parallel", "parallel", "arbitrary"), pipelined, resident),
        name="wkv_chunked",
    )(r, lw, k, v, a, k_k, k_a, r_k, gn_w, gn_b)


def _wkv_step_kernel(s_ref, r_ref, lw_ref, k_ref, v_ref, a_ref, kk_ref, ka_ref, rk_ref, gw_ref, gb_ref,
                     snew_ref, y_ref):
    S = s_ref[0]
    r, lw, k, v, a = (t[...].astype(F32) for t in (r_ref, lw_ref, k_ref, v_ref, a_ref))
    kkr = k * kk_ref[...]
    nrm = jnp.sqrt(jnp.sum(kkr * kkr, axis=0, keepdims=True))
    kk = kkr / jnp.maximum(nrm, 1e-12)
    b = kk * a
    km = k * (1.0 + (a - 1.0) * ka_ref[...])
    w = jnp.exp(lw)
    sa = -jnp.sum(S * kk[None], axis=1)
    s_new = S * w[None] + sa[:, None, :] * b[None] + v[:, None, :] * km[None]
    snew_ref[0] = s_new
    y = jnp.sum(s_new * r[None], axis=1)
    mean = jnp.mean(y, axis=0, keepdims=True)
    cen = y - mean
    var = jnp.mean(cen * cen, axis=0, keepdims=True)
    yn = cen * lax.rsqrt(var + GN_EPS) * gw_ref[...] + gb_ref[...]
    bonus = jnp.sum(r * km * rk_ref[...], axis=0, keepdims=True) * v
    y_ref[...] = yn + bonus


def _wkv_step_call(s0, r, lw, k, v, a, k_k, k_a, r_k, gn_w, gn_b):
    H, K, _, B = s0.shape
    st = pl.BlockSpec((1, K, K, B), lambda h: (h, 0, 0, 0))
    vec = pl.BlockSpec((K, B), lambda h: (h, 0))
    pipelined = 2 * _nbytes((K, K, B), F32) + 11 * _nbytes((K, B), F32)
    resident = 6 * _nbytes((K, K, B), F32)
    return pl.pallas_call(
        _wkv_step_kernel,
        grid=(H,),
        in_specs=[st] + [vec] * 10,
        out_specs=[st, vec],
        out_shape=[jax.ShapeDtypeStruct(s0.shape, F32), jax.ShapeDtypeStruct((H * K, B), F32)],
        compiler_params=_params(("parallel",), pipelined, resident),
        name="wkv_step",
    )(s0, r, lw, k, v, a, k_k, k_a, r_k, gn_w, gn_b)


def _rwkv_out_kernel(y_ref, g_ref, x_ref, gate_ref, gain_ref, wout_ref, o_ref):
    g = g_ref[0].astype(F32)
    z = _dot(y_ref[0].astype(F32) * (g * _sigmoid(g)), wout_ref[...])
    o_ref[0] = x_ref[0] + gate_ref[0] * _rmsnorm(z, gain_ref[...])


def _rwkv_out_call(y, g, x, mod, gain, w_out, *, per_row, tm):
    B, T, D = x.shape
    tile = pl.BlockSpec((1, tm, D), lambda b, t: (b, t, 0))
    full = _const_spec
    pipelined = 3 * _nbytes((tm, D), F32)
    return pl.pallas_call(
        _rwkv_out_kernel,
        grid=(B, T // tm),
        in_specs=[tile, tile, tile, _mod_spec(per_row, tm, D, 2), full((1, D)), full((D, D))],
        out_specs=tile,
        out_shape=jax.ShapeDtypeStruct((B, T, D), F32),
        compiler_params=_params(("parallel", "parallel"), pipelined,
                                4 * _nbytes((tm, D), F32) + _nbytes((D, D), BF16)),
        name="rwkv_out",
    )(y, g, x, mod, gain, w_out)


def _mla_proj_kernel(x_ref, shift_ref, scale_ref, kshift_ref, kscale_ref, gain_ref, kgain_ref,
                     cos_ref, sin_ref, wqa_ref, wgate_ref, wkv_ref, qnorm_ref, kvnorm_ref,
                     wqn_ref, wqp_ref, wqs_ref, wuk_ref,
                     q_ref, kcat_ref, ckv_ref, kpe_ref, sg_ref):
    x = x_ref[0]
    ms = jnp.mean(x * x, axis=-1, keepdims=True)
    xn = x * lax.rsqrt(ms + EPS)
    h = xn * gain_ref[...] * (1.0 + scale_ref[0]) + shift_ref[0]
    hk = xn * kgain_ref[...] * (1.0 + kscale_ref[0]) + kshift_ref[0]
    cos = cos_ref[...]
    sin = sin_ref[...]

    kv = _dot(hk, wkv_ref[...])
    ckv = _rmsnorm(kv[:, :KV_LORA], kvnorm_ref[...])
    kpe = kv[:, KV_LORA:KV_LORA + 128] * cos + kv[:, KV_LORA + 128:] * sin
    ckv_ref[0] = ckv
    kpe_ref[0] = kpe[:, :QK_ROPE]
    kcat_ref[0, :, :KV_LORA] = ckv.astype(BF16)
    kcat_ref[0, :, KV_LORA:] = kpe.astype(BF16)

    g = _dot(h, wgate_ref[...])
    sg_ref[0] = (g * _sigmoid(g)).astype(sg_ref.dtype)

    qn = _rmsnorm(_dot(h, wqa_ref[...]), qnorm_ref[...]).astype(BF16)
    q_nope = _dot(qn, wqn_ref[...])
    q_pe = _dot(qn, wqp_ref[...])
    q_ps = _dot(qn, wqs_ref[...])
    for hd in range(MLA_H):
        cols = slice(hd * 128, (hd + 1) * 128)
        q_lat = _dot(q_nope[:, cols], wuk_ref[hd])
        q_ref[0, hd, :, :KV_LORA] = (q_lat * Q_SCALE).astype(BF16)
        q_ref[0, hd, :, KV_LORA:] = ((q_pe[:, cols] * cos + q_ps[:, cols] * sin) * Q_SCALE).astype(BF16)


def _mla_proj_call(x, mod, kvmod, gain, kgain, cos, sin, wqa, wgate, wkv, qnorm, kvnorm,
                   wqn, wqp, wqs, wuk, *, per_row, tm):
    B, T, D = x.shape
    tile = lambda w: pl.BlockSpec((1, tm, w), lambda b, t: (b, t, 0))
    full = _const_spec
    tab = pl.BlockSpec((tm, 128), lambda b, t: (t, 0))
    weights = (wqa, wgate, wkv, wqn, wqp, wqs, wuk)
    pipelined = (2 * _nbytes((tm, D), F32) + _nbytes((MLA_H, tm, KCAT), BF16)
                 + 2 * _nbytes((tm, KCAT), F32))
    weight_bytes = sum(_nbytes(w.shape, BF16) for w in weights)
    return pl.pallas_call(
        _mla_proj_kernel,
        grid=(B, T // tm),
        in_specs=[tile(D), _mod_spec(per_row, tm, D, 0), _mod_spec(per_row, tm, D, 1),
                  _mod_spec(per_row, tm, D, 0), _mod_spec(per_row, tm, D, 1),
                  full((1, D)), full((1, D)), tab, tab,
                  full(wqa.shape), full(wgate.shape), full(wkv.shape), full((1, Q_LORA)), full((1, KV_LORA)),
                  full(wqn.shape), full(wqp.shape), full(wqs.shape), full(wuk.shape)],
        out_specs=[pl.BlockSpec((1, MLA_H, tm, KCAT), lambda b, t: (b, 0, t, 0)),
                   tile(KCAT), tile(KV_LORA), tile(QK_ROPE), tile(MLA_H * V_HEAD)],
        out_shape=[jax.ShapeDtypeStruct((B, MLA_H, T, KCAT), BF16),
                   jax.ShapeDtypeStruct((B, T, KCAT), BF16),
                   jax.ShapeDtypeStruct((B, T, KV_LORA), F32),
                   jax.ShapeDtypeStruct((B, T, QK_ROPE), F32),
                   jax.ShapeDtypeStruct((B, T, MLA_H * V_HEAD), BF16)],
        compiler_params=_params(("parallel", "parallel"), pipelined,
                                12 * _nbytes((tm, D), F32) + weight_bytes),
        name="mla_proj",
    )(x, mod, mod, kvmod, kvmod, gain, kgain, cos, sin, wqa, wgate, wkv, qnorm, kvnorm,
      wqn, wqp, wqs, wuk)


def _lane_tile(t, width):
    return jnp.concatenate([t] * (width // V7X_LANES), axis=1)


def _flash_kernel(qi_ref, ki_ref, q_ref, k_ref, o_ref, m_scr, l_scr, acc_scr, *, tq):
    step = pl.program_id(1)
    qi = qi_ref[step]
    ki = ki_ref[step]

    @pl.when(ki == 0)
    def _():
        m_scr[...] = jnp.full_like(m_scr, -jnp.inf)
        l_scr[...] = jnp.zeros_like(l_scr)
        acc_scr[...] = jnp.zeros_like(acc_scr)

    def update(masked):
        kc = k_ref[0]
        vc = kc[:, :KV_LORA]
        scores = lambda hd: lax.dot_general(q_ref[0, hd], kc, _NT, preferred_element_type=F32)
        if masked:
            causal = (lax.broadcasted_iota(jnp.int32, (tq, tq), 1)
                      <= lax.broadcasted_iota(jnp.int32, (tq, tq), 0))
        groups = [range(g, g + FLASH_HEAD_GROUP) for g in range(0, MLA_H, FLASH_HEAD_GROUP)]
        s_next = [scores(hd) for hd in groups[0]]
        for gi, heads in enumerate(groups):
            s = s_next
            if gi + 1 < len(groups):
                s_next = [scores(hd) for hd in groups[gi + 1]]
            if masked:
                s = [jnp.where(causal, t, -jnp.inf) for t in s]
            m_prev = [m_scr[hd] for hd in heads]
            m_new = [jnp.maximum(mp, jnp.max(t, axis=-1, keepdims=True)) for mp, t in zip(m_prev, s)]
            alpha = [jnp.exp2(mp - mn) for mp, mn in zip(m_prev, m_new)]
            p = [jnp.exp2(t - _lane_tile(mn, tq)) for t, mn in zip(s, m_new)]
            pv = [jnp.dot(t.astype(BF16), vc, preferred_element_type=F32) for t in p]
            for i, hd in enumerate(heads):
                l_scr[hd] = alpha[i] * l_scr[hd] + jnp.sum(p[i], axis=-1, keepdims=True)
                acc_scr[hd] = _lane_tile(alpha[i], KV_LORA) * acc_scr[hd] + pv[i]
                m_scr[hd] = m_new[i]

    @pl.when(ki < qi)
    def _():
        update(False)

    @pl.when(ki == qi)
    def _():
        update(True)
        for hd in range(MLA_H):
            o_ref[0, hd] = (acc_scr[hd] * _lane_tile(1.0 / l_scr[hd], KV_LORA)).astype(BF16)


def _flash_call(q, kcat, *, tq):
    B, H, T, _ = q.shape
    nq = T // tq
    pairs = [(i, j) for i in range(nq) for j in range(i + 1)]
    qi_tab = jnp.asarray([i for i, _ in pairs], jnp.int32)
    ki_tab = jnp.asarray([j for _, j in pairs], jnp.int32)
    pipelined = (_nbytes((H, tq, KCAT), BF16) + _nbytes((tq, KCAT), BF16)
                 + _nbytes((H, tq, KV_LORA), BF16))
    resident = _nbytes((H, tq, KV_LORA + 2 * V7X_LANES), F32) + 8 * _nbytes((tq, tq), F32)
    return pl.pallas_call(
        functools.partial(_flash_kernel, tq=tq),
        grid_spec=pltpu.PrefetchScalarGridSpec(
            num_scalar_prefetch=2,
            grid=(B, len(pairs)),
            in_specs=[pl.BlockSpec((1, H, tq, KCAT), lambda b, s, qt, kt: (b, 0, qt[s], 0)),
                      pl.BlockSpec((1, tq, KCAT), lambda b, s, qt, kt: (b, kt[s], 0))],
            out_specs=pl.BlockSpec((1, H, tq, KV_LORA), lambda b, s, qt, kt: (b, 0, qt[s], 0)),
            scratch_shapes=[pltpu.VMEM((H, tq, V7X_LANES), F32), pltpu.VMEM((H, tq, V7X_LANES), F32),
                            pltpu.VMEM((H, tq, KV_LORA), F32)]),
        out_shape=jax.ShapeDtypeStruct((B, H, T, KV_LORA), BF16),
        compiler_params=_params(("parallel", "arbitrary"), pipelined, resident),
        name="mla_flash",
    )(qi_tab, ki_tab, q, kcat)


DECODE_PAGES = 32


def _decode_kernel(pt_ref, q_ref, cnew_ref, pnew_ref, *refs):
    del pt_ref
    ck_refs = refs[:DECODE_PAGES]
    kp_refs = refs[DECODE_PAGES:2 * DECODE_PAGES]
    o_ref, m_scr, l_scr, acc_scr, kbuf, pbuf = refs[2 * DECODE_PAGES:]
    j = pl.program_id(1)
    q = q_ref[0]
    q_lat = q[:, :KV_LORA]
    q_pe = q[:, KV_LORA:KV_LORA + QK_ROPE]

    @pl.when(j == 0)
    def _():
        cn = cnew_ref[0]
        s_new = (jnp.sum(q_lat.astype(F32) * cn, axis=-1, keepdims=True)
                 + jnp.sum(q_pe.astype(F32) * pnew_ref[0], axis=-1, keepdims=True))
        m_scr[...] = s_new
        l_scr[...] = jnp.ones_like(l_scr)
        acc_scr[...] = jnp.broadcast_to(cn, acc_scr.shape)

    ps = ck_refs[0].shape[1]
    for i in range(DECODE_PAGES):
        kbuf[i * ps:(i + 1) * ps, :] = ck_refs[i][0].astype(BF16)
        pbuf[:, i * ps:(i + 1) * ps] = kp_refs[i][0].astype(BF16)
    keys = kbuf[...]
    s = (lax.dot_general(q_lat, keys, _NT, preferred_element_type=F32)
         + jnp.dot(q_pe, pbuf[...], preferred_element_type=F32))
    m_prev = m_scr[...]
    m_new = jnp.maximum(m_prev, jnp.max(s, axis=-1, keepdims=True))
    alpha = jnp.exp2(m_prev - m_new)
    p = jnp.exp2(s - m_new)
    l_scr[...] = alpha * l_scr[...] + jnp.sum(p, axis=-1, keepdims=True)
    acc_scr[...] = alpha * acc_scr[...] + jnp.dot(p.astype(BF16), keys, preferred_element_type=F32)
    m_scr[...] = m_new

    @pl.when(j == pl.num_programs(1) - 1)
    def _():
        o_ref[0] = acc_scr[...] / l_scr[...]


def _decode_call(page_table, q, c_new, p_new, cache_ckv, cache_kpe):
    B, H, _ = q.shape
    n_pages = page_table.shape[1]
    ps = cache_ckv.shape[1]
    assert n_pages % DECODE_PAGES == 0
    steps = n_pages // DECODE_PAGES

    def page_spec(rows, width, i):
        return pl.BlockSpec((1, rows, width), lambda b, j, pt: (pt[b, j * DECODE_PAGES + i], 0, 0))

    in_specs = ([pl.BlockSpec((1, H, KCAT), lambda b, j, pt: (b, 0, 0)),
                 pl.BlockSpec((1, 1, KV_LORA), lambda b, j, pt: (b, 0, 0)),
                 pl.BlockSpec((1, 1, QK_ROPE), lambda b, j, pt: (b, 0, 0))]
                + [page_spec(ps, KV_LORA, i) for i in range(DECODE_PAGES)]
                + [page_spec(QK_ROPE, ps, i) for i in range(DECODE_PAGES)])
    pipelined = DECODE_PAGES * (_nbytes((ps, KV_LORA), F32) + _nbytes((QK_ROPE, ps), F32))
    resident = DECODE_PAGES * _nbytes((ps, KV_LORA), F32)
    return pl.pallas_call(
        _decode_kernel,
        grid_spec=pltpu.PrefetchScalarGridSpec(
            num_scalar_prefetch=1,
            grid=(B, steps),
            in_specs=in_specs,
            out_specs=pl.BlockSpec((1, H, KV_LORA), lambda b, j, pt: (b, 0, 0)),
            scratch_shapes=[pltpu.VMEM((H, 1), F32), pltpu.VMEM((H, 1), F32),
                            pltpu.VMEM((H, KV_LORA), F32),
                            pltpu.VMEM((DECODE_PAGES * ps, KV_LORA), BF16),
                            pltpu.VMEM((QK_ROPE, DECODE_PAGES * ps), BF16)]),
        out_shape=jax.ShapeDtypeStruct((B, H, KV_LORA), F32),
        compiler_params=_params(("parallel", "arbitrary"), pipelined, resident),
        name="mla_decode",
    )(page_table, q, c_new, p_new, *([cache_ckv] * DECODE_PAGES), *([cache_kpe] * DECODE_PAGES))


def _mla_out_kernel(o_ref, sg_ref, x_ref, gate_ref, gain_ref, wuv_ref, wout_ref, y_ref, og_scr):
    for hd in range(MLA_H):
        cols = slice(hd * V_HEAD, (hd + 1) * V_HEAD)
        o = _dot(o_ref[0, hd], wuv_ref[hd])
        og_scr[:, cols] = (o * sg_ref[0, :, cols].astype(F32)).astype(BF16)
    z = _dot(og_scr[...], wout_ref[...])
    y_ref[0] = x_ref[0] + gate_ref[0] * _rmsnorm(z, gain_ref[...])


def _mla_out_call(o_lat, sg, x, mod, gain, wuv, wout, *, per_row, tm):
    B, T, D = x.shape
    tile = lambda w: pl.BlockSpec((1, tm, w), lambda b, t: (b, t, 0))
    full = _const_spec
    pipelined = _nbytes((MLA_H, tm, KV_LORA), BF16) + 3 * _nbytes((tm, D), F32)
    weight_bytes = _nbytes(wuv.shape, BF16) + _nbytes(wout.shape, BF16)
    return pl.pallas_call(
        _mla_out_kernel,
        grid=(B, T // tm),
        in_specs=[pl.BlockSpec((1, MLA_H, tm, KV_LORA), lambda b, t: (b, 0, t, 0)),
                  tile(MLA_H * V_HEAD), tile(D), _mod_spec(per_row, tm, D, 2), full((1, D)),
                  full(wuv.shape), full(wout.shape)],
        out_specs=tile(D),
        out_shape=jax.ShapeDtypeStruct((B, T, D), F32),
        scratch_shapes=[pltpu.VMEM((tm, MLA_H * V_HEAD), BF16)],
        compiler_params=_params(("parallel", "parallel"), pipelined,
                                4 * _nbytes((tm, D), F32) + weight_bytes),
        name="mla_out",
    )(o_lat, sg, x, mod, gain, wuv, wout)


def _rope_tables(pos):
    half = QK_ROPE // 2
    inv = ROPE_THETA ** (-jnp.arange(half, dtype=F32) / half)
    ang = pos.astype(F32)[:, None] * inv[None, :]
    c, s = jnp.cos(ang), jnp.sin(ang)
    z = jnp.zeros((pos.shape[0], 128 - QK_ROPE), F32)
    return jnp.concatenate([c, c, z], axis=1), jnp.concatenate([-s, s, z], axis=1)


def _swap_halves(w):
    half = w.shape[-1] // 2
    return jnp.concatenate([w[..., half:], w[..., :half]], axis=-1)


def _pad_lanes(w):
    return jnp.concatenate([w, jnp.zeros(w.shape[:-1] + (128 - w.shape[-1],), w.dtype)], axis=-1)


def kernel(x_prompt, x_sample, c_prompt, c_sample, state_wkv, state_shift, cache_kv_latent, cache_k_rope, page_table, ada_w, ada_b, norm_pre, norm_post, a_mu, a_w_in, a_w0, a_w1, a_w2, a_a0, a_a1, a_a2, a_k_k, a_k_a, a_r_k, a_gn_w, a_gn_b, a_w_out, kv_ada_w, kv_ada_b, kv_norm, kv_w_a, kv_a_norm, kv_w_b, b_w_in, b_q_norm, b_w_q, b_w_out):
    B, T, D = x_prompt.shape
    DB = x_sample.shape[0]
    H = D // RWKV_HEAD
    assert ada_w.shape[0] == 2 and a_mu.shape[0] == 1 and b_w_in.shape[0] == 1
    assert x_sample.shape[1] == 1 and T % ROW_TILE == 0 and DB % V7X_SUBLANES == 0

    c_all = jnp.concatenate([c_prompt, c_sample], axis=0)
    mods = _ada_call(c_all, ada_w, ada_b)
    kvmods = _ada_call(c_all, kv_ada_w[None], kv_ada_b[None])
    mod_p = [mods[i, :B].reshape(B, 1, 3 * D) for i in range(2)]
    mod_s = [mods[i, B:].reshape(1, DB, 3 * D) for i in range(2)]
    kvmod_p = kvmods[0, :B].reshape(B, 1, 2 * D)
    kvmod_s = kvmods[0, B:].reshape(1, DB, 2 * D)

    row = lambda v: v.reshape(1, -1)
    w_in = a_w_in[0].astype(BF16)
    a_args = (row(norm_pre[0]), a_mu[0], w_in, row(a_w0[0]), a_w1[0].astype(BF16), a_w2[0].astype(BF16),
              row(a_a0[0]), a_a1[0].astype(BF16), a_a2[0].astype(BF16))
    w_out_a = a_w_out[0].astype(BF16)
    k_k, k_a, r_k = row(a_k_k[0]), row(a_k_a[0]), row(a_r_k[0])
    gn_w, gn_b = row(a_gn_w[0]), row(a_gn_b[0])

    w_bin = b_w_in[0]
    wqa = w_bin[:, :Q_LORA].astype(BF16)
    wgate = w_bin[:, Q_LORA:].astype(BF16)
    kv_pe = kv_w_a[:, KV_LORA:]
    wkv = jnp.concatenate([kv_w_a[:, :KV_LORA], _pad_lanes(kv_pe), _pad_lanes(_swap_halves(kv_pe))],
                          axis=1).astype(BF16)
    w_q = b_w_q[0]
    wqn = w_q[:, :, :QK_NOPE].reshape(Q_LORA, MLA_H * QK_NOPE).astype(BF16)
    wq_pe = w_q[:, :, QK_NOPE:]
    wqp = _pad_lanes(wq_pe).reshape(Q_LORA, MLA_H * 128).astype(BF16)
    wqs = _pad_lanes(_swap_halves(wq_pe)).reshape(Q_LORA, MLA_H * 128).astype(BF16)
    wuk = jnp.transpose(kv_w_b[:, :, :QK_NOPE], (1, 2, 0)).astype(BF16)
    wuv = jnp.transpose(kv_w_b[:, :, QK_NOPE:], (1, 0, 2)).astype(BF16)
    wout_b = b_w_out[0].astype(BF16)
    b_args = (wqa, wgate, wkv, row(b_q_norm[0]), row(kv_a_norm), wqn, wqp, wqs, wuk)

    tm = ROW_TILE
    r, lw, k, v, a, g, h_last = _rwkv_proj_call(x_prompt, x_prompt, mod_p[0], *a_args,
                                                seq_shift=True, tm=tm)
    yw, wkv_p = _wkv_chunk_call(r, lw, k, v, a, k_k, k_a, r_k, gn_w, gn_b,
                                t_block=min(T, 256), n_pairs=8)
    x1 = _rwkv_out_call(yw, g, x_prompt, mod_p[0], row(norm_post[0]), w_out_a, per_row=False, tm=tm)
    wkv_p = wkv_p[None]
    shift_p = h_last.reshape(1, B, D)

    cos_p, sin_p = _rope_tables(jnp.arange(T, dtype=jnp.int32))
    q_p, kcat_p, ckv_p, kpe_p, sg_p = _mla_proj_call(
        x1, mod_p[1], kvmod_p, row(norm_pre[1]), row(kv_norm), cos_p, sin_p, *b_args,
        per_row=False, tm=tm)
    o_p = _flash_call(q_p, kcat_p, tq=min(T, 256))
    y_prompt = _mla_out_call(o_p, sg_p, x1, mod_p[1], row(norm_post[1]), wuv, wout_b,
                             per_row=False, tm=tm)

    xs = x_sample.reshape(1, DB, D)
    rs, lws, ks, vs, as_, gs, hs = _rwkv_proj_call(xs, state_shift[0].reshape(1, DB, D), mod_s[0],
                                                   *a_args, seq_shift=False, tm=DB)
    bm = lambda t: jnp.swapaxes(t.reshape(DB, D), 0, 1)
    pb = lambda t: jnp.broadcast_to(t.reshape(D, 1), (D, DB))
    s_new, yws = _wkv_step_call(jnp.transpose(state_wkv[0], (1, 2, 3, 0)),
                                bm(rs), bm(lws), bm(ks), bm(vs), bm(as_),
                                pb(k_k), pb(k_a), pb(r_k), pb(gn_w), pb(gn_b))
    s_new = jnp.transpose(s_new, (3, 0, 1, 2))
    x1s = _rwkv_out_call(jnp.swapaxes(yws, 0, 1).reshape(1, DB, D), gs, xs, mod_s[0],
                         row(norm_post[0]), w_out_a, per_row=True, tm=DB)
    n_pages = page_table.shape[1]
    past_len = n_pages * cache_kv_latent.shape[1]
    cos_s, sin_s = _rope_tables(jnp.full((DB,), past_len, dtype=jnp.int32))
    q_s, _, ckv_s, kpe_s, sg_s = _mla_proj_call(
        x1s, mod_s[1], kvmod_s, row(norm_pre[1]), row(kv_norm), cos_s, sin_s, *b_args,
        per_row=True, tm=DB)
    o_s = _decode_call(page_table, jnp.transpose(q_s[0], (1, 0, 2)),
                       ckv_s.reshape(DB, 1, KV_LORA), kpe_s.reshape(DB, 1, QK_ROPE),
                       cache_kv_latent, jnp.swapaxes(cache_k_rope, 1, 2))
    o_s = jnp.transpose(o_s, (1, 0, 2)).astype(BF16)[None]
    y_s = _mla_out_call(o_s, sg_s, x1s, mod_s[1], row(norm_post[1]), wuv, wout_b,
                        per_row=True, tm=DB)

    return (y_prompt, y_s.reshape(DB, 1, D), wkv_p, shift_p, ckv_p, kpe_p,
            s_new[None], hs.reshape(1, DB, D), ckv_s.reshape(DB, 1, KV_LORA),
            kpe_s.reshape(DB, 1, QK_ROPE))
```

```python
import functools
import math

import jax
import jax.numpy as jnp
from jax import lax
from jax.experimental import pallas as pl
from jax.experimental.pallas import tpu as pltpu

F32 = jnp.float32
BF16 = jnp.bfloat16
HIGHEST = lax.Precision.HIGHEST

RWKV_HEAD = 64
GN_EPS = 64e-5
EPS = 1e-6
MLA_H = 8
QK_NOPE = 128
QK_ROPE = 64
V_HEAD = 128
Q_LORA = 384
KV_LORA = 256
ROPE_THETA = 10000.0
ATTN_SCALE = (QK_NOPE + QK_ROPE) ** -0.5
Q_SCALE = ATTN_SCALE * math.log2(math.e)
KCAT = KV_LORA + 128

V7X_LANES = 128
V7X_SUBLANES = 8
V7X_VMEM_BYTES = 64 * 1024 * 1024
V7X_VMEM_REQUEST_CAP = V7X_VMEM_BYTES - 8 * 1024 * 1024

WKV_CHUNK = 64
ROW_TILE = 512
FLASH_HEAD_GROUP = 2


def _vmem_limit(pipelined_bytes, resident_bytes=0):
    est = 2 * pipelined_bytes + resident_bytes + 4 * 1024 * 1024
    return int(min(max(est, 16 * 1024 * 1024), V7X_VMEM_REQUEST_CAP))


def _nbytes(shape, dtype):
    return math.prod(shape) * jnp.dtype(dtype).itemsize


def _params(sem, pipelined_bytes, resident_bytes=0):
    return pltpu.CompilerParams(
        dimension_semantics=sem,
        vmem_limit_bytes=_vmem_limit(pipelined_bytes, resident_bytes))


def _dot(a, b):
    return jnp.dot(a.astype(BF16), b.astype(BF16), preferred_element_type=F32)


def _dot_hi(a, b, dims=(((1,), (0,)), ((), ()))):
    return lax.dot_general(a, b, dims, precision=HIGHEST, preferred_element_type=F32)


_NT = (((1,), (1,)), ((), ()))
_TN = (((0,), (0,)), ((), ()))


def _sigmoid(x):
    return 1.0 / (1.0 + jnp.exp(-x))


def _ada_kernel(c_ref, w_ref, b_ref, o_ref):
    o_ref[0] = _dot_hi(c_ref[...], w_ref[0]) + b_ref[0]


def _ada_call(c, w, b):
    G, D, N = w.shape
    M = c.shape[0]
    tn = 1024
    pipelined = _nbytes((D, tn), F32) + _nbytes((M, tn), F32) + _nbytes((M, D), F32)
    return pl.pallas_call(
        _ada_kernel,
        grid=(G, N // tn),
        in_specs=[
            pl.BlockSpec((M, D), lambda g, j: (0, 0)),
            pl.BlockSpec((1, D, tn), lambda g, j: (g, 0, j)),
            pl.BlockSpec((1, 1, tn), lambda g, j: (g, 0, j)),
        ],
        out_specs=pl.BlockSpec((1, M, tn), lambda g, j: (g, 0, j)),
        out_shape=jax.ShapeDtypeStruct((G, M, N), F32),
        compiler_params=_params(("parallel", "parallel"), pipelined),
        name="ada_modulation",
    )(c, w, b.reshape(G, 1, N))


def _const_spec(shape):
    return pl.BlockSpec(shape, lambda *_: (0,) * len(shape), pipeline_mode=pl.Buffered(1))


def _mod_spec(per_row, tm, D, col):
    if per_row:
        return pl.BlockSpec((1, tm, D), lambda b, t: (b, t, col))
    return pl.BlockSpec((1, 1, D), lambda b, t: (b, 0, col))


def _modnorm(x, gain, scale, shift):
    ms = jnp.mean(x * x, axis=-1, keepdims=True)
    return x * lax.rsqrt(ms + EPS) * gain * (1.0 + scale) + shift


def _rmsnorm(x, gain):
    ms = jnp.mean(x * x, axis=-1, keepdims=True)
    return x * lax.rsqrt(ms + EPS) * gain


def _rwkv_proj_kernel(x_ref, prev_ref, shift_ref, scale_ref, gain_ref, mu_ref, win_ref,
                      w0_ref, w1_ref, w2_ref, a0_ref, a1_ref, a2_ref,
                      r_ref, lw_ref, k_ref, v_ref, a_ref, g_ref, hlast_ref, *, seq_shift):
    x = x_ref[0]
    gain = gain_ref[...]
    scale = scale_ref[0]
    shift = shift_ref[0]
    h = _modnorm(x, gain, scale, shift)
    tm = h.shape[0]
    if seq_shift:
        hp = _modnorm(prev_ref[0][V7X_SUBLANES - 1:V7X_SUBLANES, :], gain, scale, shift)
        hp = jnp.where(pl.program_id(1) == 0, 0.0, hp)
        row = lax.broadcasted_iota(jnp.int32, (tm, 1), 0)
        hs = jnp.where(row == 0, hp, pltpu.roll(h, 1, axis=0))
        hlast_ref[0] = h[tm - 1:tm, :]
    else:
        hs = prev_ref[0]
        hlast_ref[0] = h
    xx = hs - h
    mu = mu_ref[...]
    outs = (r_ref, k_ref, v_ref, g_ref)
    for m in range(4):
        xm = h + xx * mu[m:m + 1, :]
        outs[m][0] = _dot(xm, win_ref[m]).astype(outs[m].dtype)
    xw = h + xx * mu[4:5, :]
    xa = h + xx * mu[5:6, :]
    wl = w0_ref[...] + _dot(jnp.tanh(_dot(xw, w1_ref[...])), w2_ref[...])
    z = -wl
    softplus = jnp.maximum(z, 0.0) + jnp.log(1.0 + jnp.exp(-jnp.abs(z)))
    lw_ref[0] = -jnp.exp(-softplus - 0.5)
    al = a0_ref[...] + _dot(_dot(xa, a1_ref[...]), a2_ref[...])
    a_ref[0] = _sigmoid(al).astype(a_ref.dtype)


def _rwkv_proj_call(x, prev, mod, gain, mu, w_in, w0, w1, w2, a0, a1, a2, *, seq_shift, tm):
    B, T, D = x.shape
    per_row = not seq_shift
    nt = T // tm
    tile = pl.BlockSpec((1, tm, D), lambda b, t: (b, t, 0))
    if seq_shift:
        sub = tm // V7X_SUBLANES
        prev_spec = pl.BlockSpec((1, V7X_SUBLANES, D),
                                 lambda b, t: (b, jnp.maximum(t * sub - 1, 0), 0))
        hlast_shape = jax.ShapeDtypeStruct((B, 1, D), F32)
        hlast_spec = pl.BlockSpec((1, 1, D), lambda b, t: (b, 0, 0))
        sem = ("parallel", "arbitrary")
    else:
        prev_spec = tile
        hlast_shape = jax.ShapeDtypeStruct((B, T, D), F32)
        hlast_spec = tile
        sem = ("parallel", "parallel")
    full = _const_spec
    lora = w1.shape[1]
    pipelined = 5 * _nbytes((tm, D), F32)
    resident = (8 * _nbytes((tm, D), F32) + _nbytes((4, D, D), BF16)
                + 4 * _nbytes((D, V7X_LANES), BF16))
    out_sds = lambda dt: jax.ShapeDtypeStruct((B, T, D), dt)
    return pl.pallas_call(
        functools.partial(_rwkv_proj_kernel, seq_shift=seq_shift),
        grid=(B, nt),
        in_specs=[tile, prev_spec, _mod_spec(per_row, tm, D, 0), _mod_spec(per_row, tm, D, 1),
                  full((1, D)), full((6, D)), full((4, D, D)),
                  full((1, D)), full((D, lora)), full((lora, D)),
                  full((1, D)), full((D, lora)), full((lora, D))],
        out_specs=[tile] * 6 + [hlast_spec],
        out_shape=[out_sds(BF16), out_sds(F32)] + [out_sds(BF16)] * 4 + [hlast_shape],
        compiler_params=_params(sem, pipelined, resident),
        name="rwkv_proj",
    )(x, prev, mod, mod, gain, mu, w_in, w0, w1, w2, a0, a1, a2)


def _split(x):
    hi = x.astype(BF16)
    return hi, (x - hi.astype(F32)).astype(BF16)


def _dot_split(a, b, dims=(((1,), (0,)), ((), ()))):
    (ah, al), (bh, bl) = a, b
    dot = lambda x, y: lax.dot_general(x, y, dims, preferred_element_type=F32)
    return dot(ah, bh) + dot(ah, bl) + dot(al, bh)


def _cumsum_rows(cum, x):
    hi = x.astype(BF16)
    rest = x - hi.astype(F32)
    mid = rest.astype(BF16)
    lo = (rest - mid.astype(F32)).astype(BF16)
    dot = lambda t: jnp.dot(cum, t, preferred_element_type=F32)
    return dot(hi) + dot(mid) + dot(lo)


def _wkv_chunk_kernel(r_ref, lw_ref, k_ref, v_ref, a_ref, kk_ref, ka_ref, rk_ref, gw_ref, gb_ref,
                      y_ref, sfin_ref, s_scr, wr_scr, tinv_scr, av_scr, ar_scr, bkp_scr, v2_scr, pend_scr,
                      y2_scr, bias_scr, *, n_chunks, n_pairs):
    L = WKV_CHUNK
    L2 = 2 * L
    tc = pl.program_id(2)

    @pl.when(tc == 0)
    def _():
        s_scr[...] = jnp.zeros_like(s_scr)

    lane = lax.broadcasted_iota(jnp.int32, (L2, V7X_LANES), 1)
    srow = lax.broadcasted_iota(jnp.int32, (L2, V7X_LANES), 0)
    own = (srow < L) == (lane < RWKV_HEAD)
    ti = lax.broadcasted_iota(jnp.int32, (L, L), 0)
    tj = lax.broadcasted_iota(jnp.int32, (L, L), 1)
    cum = (ti >= tj).astype(BF16)
    ri = lax.broadcasted_iota(jnp.int32, (L2, L2), 0)
    rj = lax.broadcasted_iota(jnp.int32, (L2, L2), 1)
    strict = ri > rj
    incl = ri >= rj
    eye = (ri == rj).astype(F32)
    n_rounds = int(math.log2(L)) - 1

    def stack(x):
        return jnp.where(own, jnp.concatenate([x, x], axis=0), 0.0)

    def fold(x2):
        return x2[:L] + x2[L:]

    def precompute(c, carry):
        t0 = pl.multiple_of(c * L, L)
        pairs = range(n_pairs)
        cols = [slice(p * V7X_LANES, (p + 1) * V7X_LANES) for p in pairs]
        load = lambda ref: [ref[0, pl.ds(t0, L), cols[p]].astype(F32) for p in pairs]
        r, lw, k, v, a = load(r_ref), load(lw_ref), load(k_ref), load(v_ref), load(a_ref)

        r2 = [stack(r[p]) for p in pairs]
        v2 = [stack(v[p]) for p in pairs]
        a2 = [stack(a[p]) for p in pairs]
        kkr = [stack(k[p] * kk_ref[:, cols[p]]) for p in pairs]
        nrm = [jnp.sqrt(jnp.sum(kkr[p] * kkr[p], axis=-1, keepdims=True)) for p in pairs]
        kk2 = [kkr[p] / jnp.maximum(nrm[p], 1e-12) for p in pairs]
        b2 = [kk2[p] * a2[p] for p in pairs]
        km2 = [stack(k[p] * (1.0 + (a[p] - 1.0) * ka_ref[:, cols[p]])) for p in pairs]

        cs = [_cumsum_rows(cum, lw[p]) for p in pairs]
        cs_end = [cs[p][L - 1:L, :] for p in pairs]
        cs2 = [jnp.concatenate([cs[p], cs[p]], axis=0) for p in pairs]
        lw2 = [jnp.concatenate([lw[p], lw[p]], axis=0) for p in pairs]
        wr = [jnp.concatenate([kk2[p] * jnp.exp(cs2[p] - lw2[p]), r2[p] * jnp.exp(cs2[p])],
                              axis=0).astype(BF16) for p in pairs]
        e_neg = [jnp.exp(-cs2[p]) for p in pairs]
        bk = [jnp.concatenate([b2[p] * e_neg[p], km2[p] * e_neg[p]], axis=0).astype(BF16) for p in pairs]
        aa = [lax.dot_general(wr[p], bk[p], _NT, preferred_element_type=F32) for p in pairs]

        m0 = [jnp.where(strict, -aa[p][:L2, :L2], 0.0) for p in pairs]
        m = m0
        x = [eye for p in pairs]
        for it in range(n_rounds):
            if it + 1 < n_rounds:
                mx = [_dot(m[p], jnp.concatenate([m[p], x[p]], axis=1)) for p in pairs]
                x = [x[p] + mx[p][:, L2:] for p in pairs]
                m = [mx[p][:, :L2] for p in pairs]
            else:
                x = [x[p] + _dot(m[p], x[p]) for p in pairs]
        res = [eye - x[p] + _dot_split(_split(m0[p]), _split(x[p])) for p in pairs]
        x = [x[p] + _dot(x[p], res[p]) for p in pairs]
        av = [_dot(jnp.where(strict, aa[p][:L2, L2:], 0.0), v2[p]) for p in pairs]

        for p in pairs:
            e_end = jnp.exp(cs_end[p] - cs2[p])
            wr_scr[c, p] = wr[p]
            tinv_scr[c, p] = x[p].astype(BF16)
            av_scr[c, p] = av[p]
            ar_scr[c, p] = jnp.where(jnp.concatenate([incl, incl], axis=1), aa[p][L2:, :], 0.0).astype(BF16)
            bkp_scr[c, p] = jnp.concatenate([b2[p] * e_end, km2[p] * e_end], axis=0).astype(BF16)
            v2_scr[c, p] = v2[p].astype(BF16)
            pend_scr[c, p] = jnp.broadcast_to(jnp.exp(cs_end[p]), (V7X_SUBLANES, V7X_LANES))
            bonus = fold(jnp.sum(r2[p] * km2[p] * rk_ref[:, cols[p]], axis=-1, keepdims=True) * v2[p])
            bias_scr[c, p] = gb_ref[:, cols[p]] + bonus
        return carry

    def recur(c, carry):
        pairs = range(n_pairs)
        s0 = [s_scr[p] for p in pairs]
        g = [lax.dot_general(wr_scr[c, p], s0[p].astype(BF16), _NT, preferred_element_type=F32)
             for p in pairs]
        u = [_dot(tinv_scr[c, p], -(g[p][:L2] + av_scr[c, p])) for p in pairs]
        uv = [jnp.concatenate([u[p].astype(BF16), v2_scr[c, p]], axis=0) for p in pairs]
        for p in pairs:
            s_scr[p] = (s0[p] * pend_scr[c, p][0:1, :]
                        + lax.dot_general(uv[p], bkp_scr[c, p], _TN, preferred_element_type=F32))
        for p in pairs:
            y2_scr[c, p] = g[p][L2:] + jnp.dot(ar_scr[c, p], uv[p], preferred_element_type=F32)
        return carry

    def normalise(c, carry):
        t0 = pl.multiple_of(c * L, L)
        pairs = range(n_pairs)
        y2 = [y2_scr[c, p] for p in pairs]
        mean = [jnp.sum(y2[p], axis=-1, keepdims=True) * (1.0 / RWKV_HEAD) for p in pairs]
        cen = [jnp.where(own, y2[p] - mean[p], 0.0) for p in pairs]
        var = [jnp.sum(cen[p] * cen[p], axis=-1, keepdims=True) * (1.0 / RWKV_HEAD) for p in pairs]
        for p in pairs:
            cols = slice(p * V7X_LANES, (p + 1) * V7X_LANES)
            yn = fold(cen[p] * lax.rsqrt(var[p] + GN_EPS))
            y_ref[0, pl.ds(t0, L), cols] = (bias_scr[c, p] + yn * gw_ref[:, cols]).astype(y_ref.dtype)
        return carry

    lax.fori_loop(0, n_chunks, precompute, 0)
    lax.fori_loop(0, n_chunks, recur, 0)
    lax.fori_loop(0, n_chunks, normalise, 0)

    @pl.when(tc == pl.num_programs(2) - 1)
    def _():
        for p in range(n_pairs):
            s = s_scr[p]
            sfin_ref[0, 2 * p] = s[:RWKV_HEAD, :RWKV_HEAD]
            sfin_ref[0, 2 * p + 1] = s[RWKV_HEAD:, RWKV_HEAD:]


def _wkv_chunk_call(r, lw, k, v, a, k_k, k_a, r_k, gn_w, gn_b, *, t_block, n_pairs):
    B, T, D = r.shape
    L2 = 2 * WKV_CHUNK
    n_chunks = t_block // WKV_CHUNK
    wcol = n_pairs * V7X_LANES
    n_col = D // wcol
    tile = pl.BlockSpec((1, t_block, wcol), lambda b, p, t: (b, t, p))
    vec = pl.BlockSpec((1, wcol), lambda b, p, t: (0, p))
    per_chunk = lambda shape, dt: ((n_chunks, n_pairs) + shape, dt)
    per_chunk_scratch = [
        per_chunk((2 * L2, V7X_LANES), BF16),
        per_chunk((L2, L2), BF16),
        per_chunk((L2, V7X_LANES), F32),
        per_chunk((L2, 2 * L2), BF16),
        per_chunk((2 * L2, V7X_LANES), BF16),
        per_chunk((L2, V7X_LANES), BF16),
        per_chunk((V7X_SUBLANES, V7X_LANES), F32),
        per_chunk((L2, V7X_LANES), F32),
        per_chunk((WKV_CHUNK, V7X_LANES), F32)]
    scratch = ([pltpu.VMEM((n_pairs, V7X_LANES, V7X_LANES), F32)]
               + [pltpu.VMEM(shape, dt) for shape, dt in per_chunk_scratch])
    scratch_bytes = sum(_nbytes(shape, dt) for shape, dt in per_chunk_scratch)
    pipelined = 6 * _nbytes((t_block, wcol), F32) + _nbytes((n_pairs, 128, 128), F32)
    resident = scratch_bytes + _nbytes((n_pairs, 128, 128), F32) + 24 * _nbytes((256, 256), F32)
    return pl.pallas_call(
        functools.partial(_wkv_chunk_kernel, n_chunks=n_chunks, n_pairs=n_pairs),
        grid=(B, n_col, T // t_block),
        in_specs=[tile] * 5 + [vec] * 5,
        out_specs=[tile, pl.BlockSpec((1, 2 * n_pairs, RWKV_HEAD, RWKV_HEAD), lambda b, p, t: (b, p, 0, 0))],
        out_shape=[jax.ShapeDtypeStruct((B, T, D), BF16),
                   jax.ShapeDtypeStruct((B, D // RWKV_HEAD, RWKV_HEAD, RWKV_HEAD), F32)],
        scratch_shapes=scratch,
        compiler_params=_params(("parallel", "parallel", "arbitrary"), pipelined, resident),
        name="wkv_chunked",
    )(r, lw, k, v, a, k_k, k_a, r_k, gn_w, gn_b)


def _wkv_step_kernel(s_ref, r_ref, lw_ref, k_ref, v_ref, a_ref, kk_ref, ka_ref, rk_ref, gw_ref, gb_ref,
                     snew_ref, y_ref):
    S = s_ref[0]
    r, lw, k, v, a = (t[...].astype(F32) for t in (r_ref, lw_ref, k_ref, v_ref, a_ref))
    kkr = k * kk_ref[...]
    nrm = jnp.sqrt(jnp.sum(kkr * kkr, axis=0, keepdims=True))
    kk = kkr / jnp.maximum(nrm, 1e-12)
    b = kk * a
    km = k * (1.0 + (a - 1.0) * ka_ref[...])
    w = jnp.exp(lw)
    sa = -jnp.sum(S * kk[None], axis=1)
    s_new = S * w[None] + sa[:, None, :] * b[None] + v[:, None, :] * km[None]
    snew_ref[0] = s_new
    y = jnp.sum(s_new * r[None], axis=1)
    mean = jnp.mean(y, axis=0, keepdims=True)
    cen = y - mean
    var = jnp.mean(cen * cen, axis=0, keepdims=True)
    yn = cen * lax.rsqrt(var + GN_EPS) * gw_ref[...] + gb_ref[...]
    bonus = jnp.sum(r * km * rk_ref[...], axis=0, keepdims=True) * v
    y_ref[...] = yn + bonus


def _wkv_step_call(s0, r, lw, k, v, a, k_k, k_a, r_k, gn_w, gn_b):
    H, K, _, B = s0.shape
    st = pl.BlockSpec((1, K, K, B), lambda h: (h, 0, 0, 0))
    vec = pl.BlockSpec((K, B), lambda h: (h, 0))
    pipelined = 2 * _nbytes((K, K, B), F32) + 11 * _nbytes((K, B), F32)
    resident = 6 * _nbytes((K, K, B), F32)
    return pl.pallas_call(
        _wkv_step_kernel,
        grid=(H,),
        in_specs=[st] + [vec] * 10,
        out_specs=[st, vec],
        out_shape=[jax.ShapeDtypeStruct(s0.shape, F32), jax.ShapeDtypeStruct((H * K, B), F32)],
        compiler_params=_params(("parallel",), pipelined, resident),
        name="wkv_step",
    )(s0, r, lw, k, v, a, k_k, k_a, r_k, gn_w, gn_b)


def _rwkv_out_kernel(y_ref, g_ref, x_ref, gate_ref, gain_ref, wout_ref, o_ref):
    g = g_ref[0].astype(F32)
    z = _dot(y_ref[0].astype(F32) * (g * _sigmoid(g)), wout_ref[...])
    o_ref[0] = x_ref[0] + gate_ref[0] * _rmsnorm(z, gain_ref[...])


def _rwkv_out_call(y, g, x, mod, gain, w_out, *, per_row, tm):
    B, T, D = x.shape
    tile = pl.BlockSpec((1, tm, D), lambda b, t: (b, t, 0))
    full = _const_spec
    pipelined = 3 * _nbytes((tm, D), F32)
    return pl.pallas_call(
        _rwkv_out_kernel,
        grid=(B, T // tm),
        in_specs=[tile, tile, tile, _mod_spec(per_row, tm, D, 2), full((1, D)), full((D, D))],
        out_specs=tile,
        out_shape=jax.ShapeDtypeStruct((B, T, D), F32),
        compiler_params=_params(("parallel", "parallel"), pipelined,
                                4 * _nbytes((tm, D), F32) + _nbytes((D, D), BF16)),
        name="rwkv_out",
    )(y, g, x, mod, gain, w_out)


def _mla_proj_kernel(x_ref, shift_ref, scale_ref, kshift_ref, kscale_ref, gain_ref, kgain_ref,
                     cos_ref, sin_ref, wqa_ref, wgate_ref, wkv_ref, qnorm_ref, kvnorm_ref,
                     wqn_ref, wqp_ref, wqs_ref, wuk_ref,
                     q_ref, kcat_ref, ckv_ref, kpe_ref, sg_ref):
    x = x_ref[0]
    ms = jnp.mean(x * x, axis=-1, keepdims=True)
    xn = x * lax.rsqrt(ms + EPS)
    h = xn * gain_ref[...] * (1.0 + scale_ref[0]) + shift_ref[0]
    hk = xn * kgain_ref[...] * (1.0 + kscale_ref[0]) + kshift_ref[0]
    cos = cos_ref[...]
    sin = sin_ref[...]

    kv = _dot(hk, wkv_ref[...])
    ckv = _rmsnorm(kv[:, :KV_LORA], kvnorm_ref[...])
    kpe = kv[:, KV_LORA:KV_LORA + 128] * cos + kv[:, KV_LORA + 128:] * sin
    ckv_ref[0] = ckv
    kpe_ref[0] = kpe[:, :QK_ROPE]
    kcat_ref[0, :, :KV_LORA] = ckv.astype(BF16)
    kcat_ref[0, :, KV_LORA:] = kpe.astype(BF16)

    g = _dot(h, wgate_ref[...])
    sg_ref[0] = (g * _sigmoid(g)).astype(sg_ref.dtype)

    qn = _rmsnorm(_dot(h, wqa_ref[...]), qnorm_ref[...]).astype(BF16)
    q_nope = _dot(qn, wqn_ref[...])
    q_pe = _dot(qn, wqp_ref[...])
    q_ps = _dot(qn, wqs_ref[...])
    for hd in range(MLA_H):
        cols = slice(hd * 128, (hd + 1) * 128)
        q_lat = _dot(q_nope[:, cols], wuk_ref[hd])
        q_ref[0, hd, :, :KV_LORA] = (q_lat * Q_SCALE).astype(BF16)
        q_ref[0, hd, :, KV_LORA:] = ((q_pe[:, cols] * cos + q_ps[:, cols] * sin) * Q_SCALE).astype(BF16)


def _mla_proj_call(x, mod, kvmod, gain, kgain, cos, sin, wqa, wgate, wkv, qnorm, kvnorm,
                   wqn, wqp, wqs, wuk, *, per_row, tm):
    B, T, D = x.shape
    tile = lambda w: pl.BlockSpec((1, tm, w), lambda b, t: (b, t, 0))
    full = _const_spec
    tab = pl.BlockSpec((tm, 128), lambda b, t: (t, 0))
    weights = (wqa, wgate, wkv, wqn, wqp, wqs, wuk)
    pipelined = (2 * _nbytes((tm, D), F32) + _nbytes((MLA_H, tm, KCAT), BF16)
                 + 2 * _nbytes((tm, KCAT), F32))
    weight_bytes = sum(_nbytes(w.shape, BF16) for w in weights)
    return pl.pallas_call(
        _mla_proj_kernel,
        grid=(B, T // tm),
        in_specs=[tile(D), _mod_spec(per_row, tm, D, 0), _mod_spec(per_row, tm, D, 1),
                  _mod_spec(per_row, tm, D, 0), _mod_spec(per_row, tm, D, 1),
                  full((1, D)), full((1, D)), tab, tab,
                  full(wqa.shape), full(wgate.shape), full(wkv.shape), full((1, Q_LORA)), full((1, KV_LORA)),
                  full(wqn.shape), full(wqp.shape), full(wqs.shape), full(wuk.shape)],
        out_specs=[pl.BlockSpec((1, MLA_H, tm, KCAT), lambda b, t: (b, 0, t, 0)),
                   tile(KCAT), tile(KV_LORA), tile(QK_ROPE), tile(MLA_H * V_HEAD)],
        out_shape=[jax.ShapeDtypeStruct((B, MLA_H, T, KCAT), BF16),
                   jax.ShapeDtypeStruct((B, T, KCAT), BF16),
                   jax.ShapeDtypeStruct((B, T, KV_LORA), F32),
                   jax.ShapeDtypeStruct((B, T, QK_ROPE), F32),
                   jax.ShapeDtypeStruct((B, T, MLA_H * V_HEAD), BF16)],
        compiler_params=_params(("parallel", "parallel"), pipelined,
                                12 * _nbytes((tm, D), F32) + weight_bytes),
        name="mla_proj",
    )(x, mod, mod, kvmod, kvmod, gain, kgain, cos, sin, wqa, wgate, wkv, qnorm, kvnorm,
      wqn, wqp, wqs, wuk)


def _lane_tile(t, width):
    return jnp.concatenate([t] * (width // V7X_LANES), axis=1)


def _flash_kernel(qi_ref, ki_ref, q_ref, k_ref, o_ref, m_scr, l_scr, acc_scr, *, tq):
    step = pl.program_id(1)
    qi = qi_ref[step]
    ki = ki_ref[step]

    @pl.when(ki == 0)
    def _():
        m_scr[...] = jnp.full_like(m_scr, -jnp.inf)
        l_scr[...] = jnp.zeros_like(l_scr)
        acc_scr[...] = jnp.zeros_like(acc_scr)

    def update(masked):
        kc = k_ref[0]
        vc = kc[:, :KV_LORA]
        scores = lambda hd: lax.dot_general(q_ref[0, hd], kc, _NT, preferred_element_type=F32)
        if masked:
            causal = (lax.broadcasted_iota(jnp.int32, (tq, tq), 1)
                      <= lax.broadcasted_iota(jnp.int32, (tq, tq), 0))
        groups = [range(g, g + FLASH_HEAD_GROUP) for g in range(0, MLA_H, FLASH_HEAD_GROUP)]
        s_next = [scores(hd) for hd in groups[0]]
        for gi, heads in enumerate(groups):
            s = s_next
            if gi + 1 < len(groups):
                s_next = [scores(hd) for hd in groups[gi + 1]]
            if masked:
                s = [jnp.where(causal, t, -jnp.inf) for t in s]
            m_prev = [m_scr[hd] for hd in heads]
            m_new = [jnp.maximum(mp, jnp.max(t, axis=-1, keepdims=True)) for mp, t in zip(m_prev, s)]
            alpha = [jnp.exp2(mp - mn) for mp, mn in zip(m_prev, m_new)]
            p = [jnp.exp2(t - _lane_tile(mn, tq)) for t, mn in zip(s, m_new)]
            pv = [jnp.dot(t.astype(BF16), vc, preferred_element_type=F32) for t in p]
            for i, hd in enumerate(heads):
                l_scr[hd] = alpha[i] * l_scr[hd] + jnp.sum(p[i], axis=-1, keepdims=True)
                acc_scr[hd] = _lane_tile(alpha[i], KV_LORA) * acc_scr[hd] + pv[i]
                m_scr[hd] = m_new[i]

    @pl.when(ki < qi)
    def _():
        update(False)

    @pl.when(ki == qi)
    def _():
        update(True)
        for hd in range(MLA_H):
            o_ref[0, hd] = (acc_scr[hd] * _lane_tile(1.0 / l_scr[hd], KV_LORA)).astype(BF16)


def _flash_call(q, kcat, *, tq):
    B, H, T, _ = q.shape
    nq = T // tq
    pairs = [(i, j) for i in range(nq) for j in range(i + 1)]
    qi_tab = jnp.asarray([i for i, _ in pairs], jnp.int32)
    ki_tab = jnp.asarray([j for _, j in pairs], jnp.int32)
    pipelined = (_nbytes((H, tq, KCAT), BF16) + _nbytes((tq, KCAT), BF16)
                 + _nbytes((H, tq, KV_LORA), BF16))
    resident = _nbytes((H, tq, KV_LORA + 2 * V7X_LANES), F32) + 8 * _nbytes((tq, tq), F32)
    return pl.pallas_call(
        functools.partial(_flash_kernel, tq=tq),
        grid_spec=pltpu.PrefetchScalarGridSpec(
            num_scalar_prefetch=2,
            grid=(B, len(pairs)),
            in_specs=[pl.BlockSpec((1, H, tq, KCAT), lambda b, s, qt, kt: (b, 0, qt[s], 0)),
                      pl.BlockSpec((1, tq, KCAT), lambda b, s, qt, kt: (b, kt[s], 0))],
            out_specs=pl.BlockSpec((1, H, tq, KV_LORA), lambda b, s, qt, kt: (b, 0, qt[s], 0)),
            scratch_shapes=[pltpu.VMEM((H, tq, V7X_LANES), F32), pltpu.VMEM((H, tq, V7X_LANES), F32),
                            pltpu.VMEM((H, tq, KV_LORA), F32)]),
        out_shape=jax.ShapeDtypeStruct((B, H, T, KV_LORA), BF16),
        compiler_params=_params(("parallel", "arbitrary"), pipelined, resident),
        name="mla_flash",
    )(qi_tab, ki_tab, q, kcat)


def _decode_kernel(pt_ref, q_ref, cnew_ref, pnew_ref, ckv_hbm, kpe_hbm, o_ref,
                   ck_buf, kp_buf, sems, keys_buf, rope_buf):
    b = pl.program_id(0)
    n_seq = pl.num_programs(0)
    n_pages, ps = ck_buf.shape[1], ck_buf.shape[2]
    slot = lax.rem(b, 2)

    def page_copies(seq, slot_, i):
        page = pt_ref[seq, i]
        return (pltpu.make_async_copy(ckv_hbm.at[page], ck_buf.at[slot_, i], sems.at[slot_]),
                pltpu.make_async_copy(kpe_hbm.at[page], kp_buf.at[slot_, i], sems.at[slot_]))

    def start_gather(seq, slot_):
        for i in range(n_pages):
            for cp in page_copies(seq, slot_, i):
                cp.start()

    @pl.when(b == 0)
    def _():
        start_gather(0, 0)

    @pl.when(b + 1 < n_seq)
    def _():
        start_gather(b + 1, 1 - slot)

    for i in range(n_pages):
        for cp in page_copies(b, slot, i):
            cp.wait()

    for i in range(n_pages):
        keys_buf[i * ps:(i + 1) * ps, :] = ck_buf[slot, i].astype(BF16)
        rope_buf[:, i * ps:(i + 1) * ps] = kp_buf[slot, i].astype(BF16)
    q = q_ref[0]
    q_lat = q[:, :KV_LORA]
    q_pe = q[:, KV_LORA:KV_LORA + QK_ROPE]
    keys = keys_buf[...]
    s = (lax.dot_general(q_lat, keys, _NT, preferred_element_type=F32)
         + jnp.dot(q_pe, rope_buf[...], preferred_element_type=F32))
    cn = cnew_ref[0]
    s_new = (jnp.sum(q_lat.astype(F32) * cn, axis=-1, keepdims=True)
             + jnp.sum(q_pe.astype(F32) * pnew_ref[0], axis=-1, keepdims=True))
    m = jnp.maximum(jnp.max(s, axis=-1, keepdims=True), s_new)
    p = jnp.exp2(s - m)
    p_new = jnp.exp2(s_new - m)
    denom = jnp.sum(p, axis=-1, keepdims=True) + p_new
    acc = jnp.dot(p.astype(BF16), keys, preferred_element_type=F32) + p_new * cn
    o_ref[0] = acc / denom


def _decode_call(page_table, q, c_new, p_new, cache_ckv, cache_kpe):
    B, H, _ = q.shape
    n_pages = page_table.shape[1]
    ps = cache_ckv.shape[1]
    past = n_pages * ps
    scratch = [((2, n_pages, ps, KV_LORA), F32),
               ((2, n_pages, QK_ROPE, ps), F32),
               ((past, KV_LORA), BF16),
               ((QK_ROPE, past), BF16)]
    scratch_bytes = sum(_nbytes(shape, dt) for shape, dt in scratch)
    (ck, kp, kb, rb) = [pltpu.VMEM(shape, dt) for shape, dt in scratch]
    return pl.pallas_call(
        _decode_kernel,
        grid_spec=pltpu.PrefetchScalarGridSpec(
            num_scalar_prefetch=1,
            grid=(B,),
            in_specs=[pl.BlockSpec((1, H, KCAT), lambda b, pt: (b, 0, 0)),
                      pl.BlockSpec((1, 1, KV_LORA), lambda b, pt: (b, 0, 0)),
                      pl.BlockSpec((1, 1, QK_ROPE), lambda b, pt: (b, 0, 0)),
                      pl.BlockSpec(memory_space=pl.ANY),
                      pl.BlockSpec(memory_space=pl.ANY)],
            out_specs=pl.BlockSpec((1, H, KV_LORA), lambda b, pt: (b, 0, 0)),
            scratch_shapes=[ck, kp, pltpu.SemaphoreType.DMA((2,)), kb, rb]),
        out_shape=jax.ShapeDtypeStruct((B, H, KV_LORA), F32),
        compiler_params=_params(("arbitrary",), _nbytes((H, KCAT), F32),
                                scratch_bytes + 4 * _nbytes((H, past), F32)),
        name="mla_decode",
    )(page_table, q, c_new, p_new, cache_ckv, cache_kpe)


def _mla_out_kernel(o_ref, sg_ref, x_ref, gate_ref, gain_ref, wuv_ref, wout_ref, y_ref, og_scr):
    for hd in range(MLA_H):
        cols = slice(hd * V_HEAD, (hd + 1) * V_HEAD)
        o = _dot(o_ref[0, hd], wuv_ref[hd])
        og_scr[:, cols] = (o * sg_ref[0, :, cols].astype(F32)).astype(BF16)
    z = _dot(og_scr[...], wout_ref[...])
    y_ref[0] = x_ref[0] + gate_ref[0] * _rmsnorm(z, gain_ref[...])


def _mla_out_call(o_lat, sg, x, mod, gain, wuv, wout, *, per_row, tm):
    B, T, D = x.shape
    tile = lambda w: pl.BlockSpec((1, tm, w), lambda b, t: (b, t, 0))
    full = _const_spec
    pipelined = _nbytes((MLA_H, tm, KV_LORA), BF16) + 3 * _nbytes((tm, D), F32)
    weight_bytes = _nbytes(wuv.shape, BF16) + _nbytes(wout.shape, BF16)
    return pl.pallas_call(
        _mla_out_kernel,
        grid=(B, T // tm),
        in_specs=[pl.BlockSpec((1, MLA_H, tm, KV_LORA), lambda b, t: (b, 0, t, 0)),
                  tile(MLA_H * V_HEAD), tile(D), _mod_spec(per_row, tm, D, 2), full((1, D)),
                  full(wuv.shape), full(wout.shape)],
        out_specs=tile(D),
        out_shape=jax.ShapeDtypeStruct((B, T, D), F32),
        scratch_shapes=[pltpu.VMEM((tm, MLA_H * V_HEAD), BF16)],
        compiler_params=_params(("parallel", "parallel"), pipelined,
                                4 * _nbytes((tm, D), F32) + weight_bytes),
        name="mla_out",
    )(o_lat, sg, x, mod, gain, wuv, wout)


def _rope_tables(pos):
    half = QK_ROPE // 2
    inv = ROPE_THETA ** (-jnp.arange(half, dtype=F32) / half)
    ang = pos.astype(F32)[:, None] * inv[None, :]
    c, s = jnp.cos(ang), jnp.sin(ang)
    z = jnp.zeros((pos.shape[0], 128 - QK_ROPE), F32)
    return jnp.concatenate([c, c, z], axis=1), jnp.concatenate([-s, s, z], axis=1)


def _swap_halves(w):
    half = w.shape[-1] // 2
    return jnp.concatenate([w[..., half:], w[..., :half]], axis=-1)


def _pad_lanes(w):
    return jnp.concatenate([w, jnp.zeros(w.shape[:-1] + (128 - w.shape[-1],), w.dtype)], axis=-1)


def kernel(x_prompt, x_sample, c_prompt, c_sample, state_wkv, state_shift, cache_kv_latent, cache_k_rope, page_table, ada_w, ada_b, norm_pre, norm_post, a_mu, a_w_in, a_w0, a_w1, a_w2, a_a0, a_a1, a_a2, a_k_k, a_k_a, a_r_k, a_gn_w, a_gn_b, a_w_out, kv_ada_w, kv_ada_b, kv_norm, kv_w_a, kv_a_norm, kv_w_b, b_w_in, b_q_norm, b_w_q, b_w_out):
    B, T, D = x_prompt.shape
    DB = x_sample.shape[0]
    H = D // RWKV_HEAD
    assert ada_w.shape[0] == 2 and a_mu.shape[0] == 1 and b_w_in.shape[0] == 1
    assert x_sample.shape[1] == 1 and T % ROW_TILE == 0 and DB % V7X_SUBLANES == 0

    c_all = jnp.concatenate([c_prompt, c_sample], axis=0)
    mods = _ada_call(c_all, ada_w, ada_b)
    kvmods = _ada_call(c_all, kv_ada_w[None], kv_ada_b[None])
    mod_p = [mods[i, :B].reshape(B, 1, 3 * D) for i in range(2)]
    mod_s = [mods[i, B:].reshape(1, DB, 3 * D) for i in range(2)]
    kvmod_p = kvmods[0, :B].reshape(B, 1, 2 * D)
    kvmod_s = kvmods[0, B:].reshape(1, DB, 2 * D)

    row = lambda v: v.reshape(1, -1)
    w_in = a_w_in[0].astype(BF16)
    a_args = (row(norm_pre[0]), a_mu[0], w_in, row(a_w0[0]), a_w1[0].astype(BF16), a_w2[0].astype(BF16),
              row(a_a0[0]), a_a1[0].astype(BF16), a_a2[0].astype(BF16))
    w_out_a = a_w_out[0].astype(BF16)
    k_k, k_a, r_k = row(a_k_k[0]), row(a_k_a[0]), row(a_r_k[0])
    gn_w, gn_b = row(a_gn_w[0]), row(a_gn_b[0])

    w_bin = b_w_in[0]
    wqa = w_bin[:, :Q_LORA].astype(BF16)
    wgate = w_bin[:, Q_LORA:].astype(BF16)
    kv_pe = kv_w_a[:, KV_LORA:]
    wkv = jnp.concatenate([kv_w_a[:, :KV_LORA], _pad_lanes(kv_pe), _pad_lanes(_swap_halves(kv_pe))],
                          axis=1).astype(BF16)
    w_q = b_w_q[0]
    wqn = w_q[:, :, :QK_NOPE].reshape(Q_LORA, MLA_H * QK_NOPE).astype(BF16)
    wq_pe = w_q[:, :, QK_NOPE:]
    wqp = _pad_lanes(wq_pe).reshape(Q_LORA, MLA_H * 128).astype(BF16)
    wqs = _pad_lanes(_swap_halves(wq_pe)).reshape(Q_LORA, MLA_H * 128).astype(BF16)
    wuk = jnp.transpose(kv_w_b[:, :, :QK_NOPE], (1, 2, 0)).astype(BF16)
    wuv = jnp.transpose(kv_w_b[:, :, QK_NOPE:], (1, 0, 2)).astype(BF16)
    wout_b = b_w_out[0].astype(BF16)
    b_args = (wqa, wgate, wkv, row(b_q_norm[0]), row(kv_a_norm), wqn, wqp, wqs, wuk)

    tm = ROW_TILE
    r, lw, k, v, a, g, h_last = _rwkv_proj_call(x_prompt, x_prompt, mod_p[0], *a_args,
                                                seq_shift=True, tm=tm)
    yw, wkv_p = _wkv_chunk_call(r, lw, k, v, a, k_k, k_a, r_k, gn_w, gn_b,
                                t_block=min(T, 256), n_pairs=8)
    x1 = _rwkv_out_call(yw, g, x_prompt, mod_p[0], row(norm_post[0]), w_out_a, per_row=False, tm=tm)
    wkv_p = wkv_p[None]
    shift_p = h_last.reshape(1, B, D)

    cos_p, sin_p = _rope_tables(jnp.arange(T, dtype=jnp.int32))
    q_p, kcat_p, ckv_p, kpe_p, sg_p = _mla_proj_call(
        x1, mod_p[1], kvmod_p, row(norm_pre[1]), row(kv_norm), cos_p, sin_p, *b_args,
        per_row=False, tm=tm)
    o_p = _flash_call(q_p, kcat_p, tq=min(T, 256))
    y_prompt = _mla_out_call(o_p, sg_p, x1, mod_p[1], row(norm_post[1]), wuv, wout_b,
                             per_row=False, tm=tm)

    xs = x_sample.reshape(1, DB, D)
    rs, lws, ks, vs, as_, gs, hs = _rwkv_proj_call(xs, state_shift[0].reshape(1, DB, D), mod_s[0],
                                                   *a_args, seq_shift=False, tm=DB)
    bm = lambda t: jnp.swapaxes(t.reshape(DB, D), 0, 1)
    pb = lambda t: jnp.broadcast_to(t.reshape(D, 1), (D, DB))
    s_new, yws = _wkv_step_call(jnp.transpose(state_wkv[0], (1, 2, 3, 0)),
                                bm(rs), bm(lws), bm(ks), bm(vs), bm(as_),
                                pb(k_k), pb(k_a), pb(r_k), pb(gn_w), pb(gn_b))
    s_new = jnp.transpose(s_new, (3, 0, 1, 2))
    x1s = _rwkv_out_call(jnp.swapaxes(yws, 0, 1).reshape(1, DB, D), gs, xs, mod_s[0],
                         row(norm_post[0]), w_out_a, per_row=True, tm=DB)
    n_pages = page_table.shape[1]
    past_len = n_pages * cache_kv_latent.shape[1]
    cos_s, sin_s = _rope_tables(jnp.full((DB,), past_len, dtype=jnp.int32))
    q_s, _, ckv_s, kpe_s, sg_s = _mla_proj_call(
        x1s, mod_s[1], kvmod_s, row(norm_pre[1]), row(kv_norm), cos_s, sin_s, *b_args,
        per_row=True, tm=DB)
    o_s = _decode_call(page_table, jnp.transpose(q_s[0], (1, 0, 2)),
                       ckv_s.reshape(DB, 1, KV_LORA), kpe_s.reshape(DB, 1, QK_ROPE),
                       cache_kv_latent, jnp.swapaxes(cache_k_rope, 1, 2))
    o_s = jnp.transpose(o_s, (1, 0, 2)).astype(BF16)[None]
    y_s = _mla_out_call(o_s, sg_s, x1s, mod_s[1], row(norm_post[1]), wuv, wout_b,
                        per_row=True, tm=DB)

    return (y_prompt, y_s.reshape(DB, 1, D), wkv_p, shift_p, ckv_p, kpe_p,
            s_new[None], hs.reshape(1, DB, D), ckv_s.reshape(DB, 1, KV_LORA),
            kpe_s.reshape(DB, 1, QK_ROPE))
```

```python
import functools
import math

import jax
import jax.numpy as jnp
from jax import lax
from jax.experimental import pallas as pl
from jax.experimental.pallas import tpu as pltpu

F32 = jnp.float32
BF16 = jnp.bfloat16
HIGHEST = lax.Precision.HIGHEST

RWKV_HEAD = 64
GN_EPS = 64e-5
EPS = 1e-6
MLA_H = 8
QK_NOPE = 128
QK_ROPE = 64
V_HEAD = 128
Q_LORA = 384
KV_LORA = 256
ROPE_THETA = 10000.0
ATTN_SCALE = (QK_NOPE + QK_ROPE) ** -0.5
Q_SCALE = ATTN_SCALE * math.log2(math.e)
KCAT = KV_LORA + 128

V7X_LANES = 128
V7X_SUBLANES = 8
V7X_VMEM_BYTES = 64 * 1024 * 1024
V7X_VMEM_REQUEST_CAP = V7X_VMEM_BYTES - 8 * 1024 * 1024

WKV_CHUNK = 64
ROW_TILE = 512
FLASH_HEAD_GROUP = 2


def _vmem_limit(pipelined_bytes, resident_bytes=0):
    est = 2 * pipelined_bytes + resident_bytes + 4 * 1024 * 1024
    return int(min(max(est, 16 * 1024 * 1024), V7X_VMEM_REQUEST_CAP))


def _nbytes(shape, dtype):
    return math.prod(shape) * jnp.dtype(dtype).itemsize


def _params(sem, pipelined_bytes, resident_bytes=0):
    return pltpu.CompilerParams(
        dimension_semantics=sem,
        vmem_limit_bytes=_vmem_limit(pipelined_bytes, resident_bytes))


def _dot(a, b):
    return jnp.dot(a.astype(BF16), b.astype(BF16), preferred_element_type=F32)


def _dot_hi(a, b, dims=(((1,), (0,)), ((), ()))):
    return lax.dot_general(a, b, dims, precision=HIGHEST, preferred_element_type=F32)


_NT = (((1,), (1,)), ((), ()))
_TN = (((0,), (0,)), ((), ()))


def _sigmoid(x):
    return 1.0 / (1.0 + jnp.exp(-x))


def _ada_kernel(c_ref, w_ref, b_ref, o_ref):
    o_ref[0] = _dot_hi(c_ref[...], w_ref[0]) + b_ref[0]


def _ada_call(c, w, b):
    G, D, N = w.shape
    M = c.shape[0]
    tn = 1024
    pipelined = _nbytes((D, tn), F32) + _nbytes((M, tn), F32) + _nbytes((M, D), F32)
    return pl.pallas_call(
        _ada_kernel,
        grid=(G, N // tn),
        in_specs=[
            pl.BlockSpec((M, D), lambda g, j: (0, 0)),
            pl.BlockSpec((1, D, tn), lambda g, j: (g, 0, j)),
            pl.BlockSpec((1, 1, tn), lambda g, j: (g, 0, j)),
        ],
        out_specs=pl.BlockSpec((1, M, tn), lambda g, j: (g, 0, j)),
        out_shape=jax.ShapeDtypeStruct((G, M, N), F32),
        compiler_params=_params(("parallel", "parallel"), pipelined),
        name="ada_modulation",
    )(c, w, b.reshape(G, 1, N))


def _const_spec(shape):
    return pl.BlockSpec(shape, lambda *_: (0,) * len(shape), pipeline_mode=pl.Buffered(1))


def _mod_spec(per_row, tm, D, col):
    if per_row:
        return pl.BlockSpec((1, tm, D), lambda b, t: (b, t, col))
    return pl.BlockSpec((1, 1, D), lambda b, t: (b, 0, col))


def _modnorm(x, gain, scale, shift):
    ms = jnp.mean(x * x, axis=-1, keepdims=True)
    return x * lax.rsqrt(ms + EPS) * gain * (1.0 + scale) + shift


def _rmsnorm(x, gain):
    ms = jnp.mean(x * x, axis=-1, keepdims=True)
    return x * lax.rsqrt(ms + EPS) * gain


def _rwkv_proj_kernel(x_ref, prev_ref, shift_ref, scale_ref, gain_ref, mu_ref, win_ref,
                      w0_ref, w1_ref, w2_ref, a0_ref, a1_ref, a2_ref,
                      r_ref, lw_ref, k_ref, v_ref, a_ref, g_ref, hlast_ref, *, seq_shift):
    x = x_ref[0]
    gain = gain_ref[...]
    scale = scale_ref[0]
    shift = shift_ref[0]
    h = _modnorm(x, gain, scale, shift)
    tm = h.shape[0]
    if seq_shift:
        hp = _modnorm(prev_ref[0][V7X_SUBLANES - 1:V7X_SUBLANES, :], gain, scale, shift)
        hp = jnp.where(pl.program_id(1) == 0, 0.0, hp)
        row = lax.broadcasted_iota(jnp.int32, (tm, 1), 0)
        hs = jnp.where(row == 0, hp, pltpu.roll(h, 1, axis=0))
        hlast_ref[0] = h[tm - 1:tm, :]
    else:
        hs = prev_ref[0]
        hlast_ref[0] = h
    xx = hs - h
    mu = mu_ref[...]
    outs = (r_ref, k_ref, v_ref, g_ref)
    for m in range(4):
        xm = h + xx * mu[m:m + 1, :]
        outs[m][0] = _dot(xm, win_ref[m]).astype(outs[m].dtype)
    xw = h + xx * mu[4:5, :]
    xa = h + xx * mu[5:6, :]
    wl = w0_ref[...] + _dot(jnp.tanh(_dot(xw, w1_ref[...])), w2_ref[...])
    z = -wl
    softplus = jnp.maximum(z, 0.0) + jnp.log(1.0 + jnp.exp(-jnp.abs(z)))
    lw_ref[0] = -jnp.exp(-softplus - 0.5)
    al = a0_ref[...] + _dot(_dot(xa, a1_ref[...]), a2_ref[...])
    a_ref[0] = _sigmoid(al).astype(a_ref.dtype)


def _rwkv_proj_call(x, prev, mod, gain, mu, w_in, w0, w1, w2, a0, a1, a2, *, seq_shift, tm):
    B, T, D = x.shape
    per_row = not seq_shift
    nt = T // tm
    tile = pl.BlockSpec((1, tm, D), lambda b, t: (b, t, 0))
    if seq_shift:
        sub = tm // V7X_SUBLANES
        prev_spec = pl.BlockSpec((1, V7X_SUBLANES, D),
                                 lambda b, t: (b, jnp.maximum(t * sub - 1, 0), 0))
        hlast_shape = jax.ShapeDtypeStruct((B, 1, D), F32)
        hlast_spec = pl.BlockSpec((1, 1, D), lambda b, t: (b, 0, 0))
        sem = ("parallel", "arbitrary")
    else:
        prev_spec = tile
        hlast_shape = jax.ShapeDtypeStruct((B, T, D), F32)
        hlast_spec = tile
        sem = ("parallel", "parallel")
    full = _const_spec
    lora = w1.shape[1]
    pipelined = 5 * _nbytes((tm, D), F32)
    resident = (8 * _nbytes((tm, D), F32) + _nbytes((4, D, D), BF16)
                + 4 * _nbytes((D, V7X_LANES), BF16))
    out_sds = lambda dt: jax.ShapeDtypeStruct((B, T, D), dt)
    return pl.pallas_call(
        functools.partial(_rwkv_proj_kernel, seq_shift=seq_shift),
        grid=(B, nt),
        in_specs=[tile, prev_spec, _mod_spec(per_row, tm, D, 0), _mod_spec(per_row, tm, D, 1),
                  full((1, D)), full((6, D)), full((4, D, D)),
                  full((1, D)), full((D, lora)), full((lora, D)),
                  full((1, D)), full((D, lora)), full((lora, D))],
        out_specs=[tile] * 6 + [hlast_spec],
        out_shape=[out_sds(BF16), out_sds(F32)] + [out_sds(BF16)] * 4 + [hlast_shape],
        compiler_params=_params(sem, pipelined, resident),
        name="rwkv_proj",
    )(x, prev, mod, mod, gain, mu, w_in, w0, w1, w2, a0, a1, a2)


def _cumsum_rows(cum, x):
    hi = x.astype(BF16)
    rest = x - hi.astype(F32)
    mid = rest.astype(BF16)
    lo = (rest - mid.astype(F32)).astype(BF16)
    dot = lambda t: jnp.dot(cum, t, preferred_element_type=F32)
    return dot(hi) + dot(mid) + dot(lo)


def _wkv_chunk_kernel(r_ref, lw_ref, k_ref, v_ref, a_ref, kk_ref, ka_ref, rk_ref, gw_ref, gb_ref,
                      y_ref, sfin_ref, s_scr, wr_scr, tinv_scr, av_scr, ar_scr, bkp_scr, v2_scr, pend_scr,
                      y2_scr, bias_scr, *, n_chunks, n_pairs):
    L = WKV_CHUNK
    L2 = 2 * L
    tc = pl.program_id(2)

    @pl.when(tc == 0)
    def _():
        s_scr[...] = jnp.zeros_like(s_scr)

    lane = lax.broadcasted_iota(jnp.int32, (L2, V7X_LANES), 1)
    srow = lax.broadcasted_iota(jnp.int32, (L2, V7X_LANES), 0)
    own = (srow < L) == (lane < RWKV_HEAD)
    first_head = lax.broadcasted_iota(jnp.int32, (L, V7X_LANES), 1) < RWKV_HEAD
    ti = lax.broadcasted_iota(jnp.int32, (L, L), 0)
    tj = lax.broadcasted_iota(jnp.int32, (L, L), 1)
    cum = (ti >= tj).astype(BF16)
    ri = lax.broadcasted_iota(jnp.int32, (L2, L2), 0)
    rj = lax.broadcasted_iota(jnp.int32, (L2, L2), 1)
    strict = ri > rj
    incl = ri >= rj
    eye = (ri == rj).astype(F32)
    n_rounds = int(math.log2(L)) - 1

    def stack(x):
        return jnp.where(own, jnp.concatenate([x, x], axis=0), 0.0)

    def fold(x2):
        return x2[:L] + x2[L:]

    def precompute(c, carry):
        t0 = pl.multiple_of(c * L, L)
        pairs = range(n_pairs)
        cols = [slice(p * V7X_LANES, (p + 1) * V7X_LANES) for p in pairs]
        load = lambda ref: [ref[0, pl.ds(t0, L), cols[p]].astype(F32) for p in pairs]
        r, lw, k, v, a = load(r_ref), load(lw_ref), load(k_ref), load(v_ref), load(a_ref)

        def head_sum(t):
            s0 = jnp.sum(jnp.where(first_head, t, 0.0), axis=-1, keepdims=True)
            s1 = jnp.sum(t, axis=-1, keepdims=True) - s0
            return jnp.where(first_head, s0, s1)

        kkr = [k[p] * kk_ref[:, cols[p]] for p in pairs]
        nrm = [jnp.sqrt(head_sum(kkr[p] * kkr[p])) for p in pairs]
        kk = [kkr[p] / jnp.maximum(nrm[p], 1e-12) for p in pairs]
        b = [kk[p] * a[p] for p in pairs]
        km = [k[p] * (1.0 + (a[p] - 1.0) * ka_ref[:, cols[p]]) for p in pairs]
        v2 = [stack(v[p]) for p in pairs]

        cs = [_cumsum_rows(cum, lw[p]) for p in pairs]
        cs_end = [cs[p][L - 1:L, :] for p in pairs]
        wr = [jnp.concatenate([stack(kk[p] * jnp.exp(cs[p] - lw[p])), stack(r[p] * jnp.exp(cs[p]))],
                              axis=0).astype(BF16) for p in pairs]
        e_neg = [jnp.exp(-cs[p]) for p in pairs]
        bk = [jnp.concatenate([stack(b[p] * e_neg[p]), stack(km[p] * e_neg[p])], axis=0).astype(BF16)
              for p in pairs]
        aa = [lax.dot_general(wr[p], bk[p], _NT, preferred_element_type=F32) for p in pairs]

        m0 = [jnp.where(strict, -aa[p][:L2, :L2], 0.0) for p in pairs]
        m = m0
        x = [eye for p in pairs]
        for it in range(n_rounds):
            if it + 1 < n_rounds:
                mx = [_dot(m[p], jnp.concatenate([m[p], x[p]], axis=1)) for p in pairs]
                x = [x[p] + mx[p][:, L2:] for p in pairs]
                m = [mx[p][:, :L2] for p in pairs]
            else:
                x = [x[p] + _dot(m[p], x[p]) for p in pairs]
        res = [eye - x[p] + _dot(m0[p], x[p]) for p in pairs]
        x = [x[p] + _dot(x[p], res[p]) for p in pairs]
        av = [_dot(jnp.where(strict, aa[p][:L2, L2:], 0.0), v2[p]) for p in pairs]

        for p in pairs:
            e_end = jnp.exp(cs_end[p] - cs[p])
            wr_scr[c, p] = wr[p]
            tinv_scr[c, p] = x[p].astype(BF16)
            av_scr[c, p] = av[p]
            ar_scr[c, p] = jnp.where(jnp.concatenate([incl, incl], axis=1), aa[p][L2:, :], 0.0).astype(BF16)
            bkp_scr[c, p] = jnp.concatenate([stack(b[p] * e_end), stack(km[p] * e_end)],
                                            axis=0).astype(BF16)
            v2_scr[c, p] = v2[p].astype(BF16)
            pend_scr[c, p] = jnp.broadcast_to(jnp.exp(cs_end[p]), (V7X_SUBLANES, V7X_LANES))
            bonus = head_sum(r[p] * km[p] * rk_ref[:, cols[p]]) * v[p]
            bias_scr[c, p] = gb_ref[:, cols[p]] + bonus
        return carry

    def recur(c, carry):
        pairs = range(n_pairs)
        s0 = [s_scr[p] for p in pairs]
        g = [lax.dot_general(wr_scr[c, p], s0[p].astype(BF16), _NT, preferred_element_type=F32)
             for p in pairs]
        u = [_dot(tinv_scr[c, p], -(g[p][:L2] + av_scr[c, p])) for p in pairs]
        uv = [jnp.concatenate([u[p].astype(BF16), v2_scr[c, p]], axis=0) for p in pairs]
        for p in pairs:
            s_scr[p] = (s0[p] * pend_scr[c, p][0:1, :]
                        + lax.dot_general(uv[p], bkp_scr[c, p], _TN, preferred_element_type=F32))
        for p in pairs:
            y2_scr[c, p] = g[p][L2:] + jnp.dot(ar_scr[c, p], uv[p], preferred_element_type=F32)
        return carry

    def normalise(c, carry):
        t0 = c * L if isinstance(c, int) else pl.multiple_of(c * L, L)
        pairs = range(n_pairs)
        y2 = [y2_scr[c, p] for p in pairs]
        mean = [jnp.sum(y2[p], axis=-1, keepdims=True) * (1.0 / RWKV_HEAD) for p in pairs]
        cen = [jnp.where(own, y2[p] - mean[p], 0.0) for p in pairs]
        var = [jnp.sum(cen[p] * cen[p], axis=-1, keepdims=True) * (1.0 / RWKV_HEAD) for p in pairs]
        for p in pairs:
            cols = slice(p * V7X_LANES, (p + 1) * V7X_LANES)
            yn = fold(cen[p] * lax.rsqrt(var[p] + GN_EPS))
            y_ref[0, pl.ds(t0, L), cols] = (bias_scr[c, p] + yn * gw_ref[:, cols]).astype(y_ref.dtype)
        return carry

    def recur_and_normalise(c, carry):
        normalise(c - 1, carry)
        return recur(c, carry)

    lax.fori_loop(0, n_chunks, precompute, 0)
    recur(0, 0)
    lax.fori_loop(1, n_chunks, recur_and_normalise, 0)
    normalise(n_chunks - 1, 0)

    @pl.when(tc == pl.num_programs(2) - 1)
    def _():
        for p in range(n_pairs):
            s = s_scr[p]
            sfin_ref[0, 2 * p] = s[:RWKV_HEAD, :RWKV_HEAD]
            sfin_ref[0, 2 * p + 1] = s[RWKV_HEAD:, RWKV_HEAD:]


def _wkv_chunk_call(r, lw, k, v, a, k_k, k_a, r_k, gn_w, gn_b, *, t_block, n_pairs):
    B, T, D = r.shape
    L2 = 2 * WKV_CHUNK
    n_chunks = t_block // WKV_CHUNK
    wcol = n_pairs * V7X_LANES
    n_col = D // wcol
    tile = pl.BlockSpec((1, t_block, wcol), lambda b, p, t: (b, t, p))
    vec = pl.BlockSpec((1, wcol), lambda b, p, t: (0, p))
    per_chunk = lambda shape, dt: ((n_chunks, n_pairs) + shape, dt)
    per_chunk_scratch = [
        per_chunk((2 * L2, V7X_LANES), BF16),
        per_chunk((L2, L2), BF16),
        per_chunk((L2, V7X_LANES), F32),
        per_chunk((L2, 2 * L2), BF16),
        per_chunk((2 * L2, V7X_LANES), BF16),
        per_chunk((L2, V7X_LANES), BF16),
        per_chunk((V7X_SUBLANES, V7X_LANES), F32),
        per_chunk((L2, V7X_LANES), F32),
        per_chunk((WKV_CHUNK, V7X_LANES), F32)]
    scratch = ([pltpu.VMEM((n_pairs, V7X_LANES, V7X_LANES), F32)]
               + [pltpu.VMEM(shape, dt) for shape, dt in per_chunk_scratch])
    scratch_bytes = sum(_nbytes(shape, dt) for shape, dt in per_chunk_scratch)
    pipelined = 6 * _nbytes((t_block, wcol), F32) + _nbytes((n_pairs, 128, 128), F32)
    resident = scratch_bytes + _nbytes((n_pairs, 128, 128), F32) + 24 * _nbytes((256, 256), F32)
    return pl.pallas_call(
        functools.partial(_wkv_chunk_kernel, n_chunks=n_chunks, n_pairs=n_pairs),
        grid=(B, n_col, T // t_block),
        in_specs=[tile] * 5 + [vec] * 5,
        out_specs=[tile, pl.BlockSpec((1, 2 * n_pairs, RWKV_HEAD, RWKV_HEAD), lambda b, p, t: (b, p, 0, 0))],
        out_shape=[jax.ShapeDtypeStruct((B, T, D), BF16),
                   jax.ShapeDtypeStruct((B, D // RWKV_HEAD, RWKV_HEAD, RWKV_HEAD), F32)],
        scratch_shapes=scratch,
        compiler_params=_params(("parallel", "parallel", "arbitrary"), pipelined, resident),
        name="wkv_chunked",
    )(r, lw, k, v, a, k_k, k_a, r_k, gn_w, gn_b)


def _wkv_step_kernel(s_ref, r_ref, lw_ref, k_ref, v_ref, a_ref, kk_ref, ka_ref, rk_ref, gw_ref, gb_ref,
                     snew_ref, y_ref):
    S = s_ref[0]
    r, lw, k, v, a = (t[...].astype(F32) for t in (r_ref, lw_ref, k_ref, v_ref, a_ref))
    kkr = k * kk_ref[...]
    nrm = jnp.sqrt(jnp.sum(kkr * kkr, axis=0, keepdims=True))
    kk = kkr / jnp.maximum(nrm, 1e-12)
    b = kk * a
    km = k * (1.0 + (a - 1.0) * ka_ref[...])
    w = jnp.exp(lw)
    sa = -jnp.sum(S * kk[None], axis=1)
    s_new = S * w[None] + sa[:, None, :] * b[None] + v[:, None, :] * km[None]
    snew_ref[0] = s_new
    y = jnp.sum(s_new * r[None], axis=1)
    mean = jnp.mean(y, axis=0, keepdims=True)
    cen = y - mean
    var = jnp.mean(cen * cen, axis=0, keepdims=True)
    yn = cen * lax.rsqrt(var + GN_EPS) * gw_ref[...] + gb_ref[...]
    bonus = jnp.sum(r * km * rk_ref[...], axis=0, keepdims=True) * v
    y_ref[...] = yn + bonus


def _wkv_step_call(s0, r, lw, k, v, a, k_k, k_a, r_k, gn_w, gn_b):
    H, K, _, B = s0.shape
    st = pl.BlockSpec((1, K, K, B), lambda h: (h, 0, 0, 0))
    vec = pl.BlockSpec((K, B), lambda h: (h, 0))
    pipelined = 2 * _nbytes((K, K, B), F32) + 11 * _nbytes((K, B), F32)
    resident = 6 * _nbytes((K, K, B), F32)
    return pl.pallas_call(
        _wkv_step_kernel,
        grid=(H,),
        in_specs=[st] + [vec] * 10,
        out_specs=[st, vec],
        out_shape=[jax.ShapeDtypeStruct(s0.shape, F32), jax.ShapeDtypeStruct((H * K, B), F32)],
        compiler_params=_params(("parallel",), pipelined, resident),
        name="wkv_step",
    )(s0, r, lw, k, v, a, k_k, k_a, r_k, gn_w, gn_b)


def _rwkv_out_kernel(y_ref, g_ref, x_ref, gate_ref, gain_ref, wout_ref, o_ref):
    g = g_ref[0].astype(F32)
    z = _dot(y_ref[0].astype(F32) * (g * _sigmoid(g)), wout_ref[...])
    o_ref[0] = x_ref[0] + gate_ref[0] * _rmsnorm(z, gain_ref[...])


def _rwkv_out_call(y, g, x, mod, gain, w_out, *, per_row, tm):
    B, T, D = x.shape
    tile = pl.BlockSpec((1, tm, D), lambda b, t: (b, t, 0))
    full = _const_spec
    pipelined = 3 * _nbytes((tm, D), F32)
    return pl.pallas_call(
        _rwkv_out_kernel,
        grid=(B, T // tm),
        in_specs=[tile, tile, tile, _mod_spec(per_row, tm, D, 2), full((1, D)), full((D, D))],
        out_specs=tile,
        out_shape=jax.ShapeDtypeStruct((B, T, D), F32),
        compiler_params=_params(("parallel", "parallel"), pipelined,
                                4 * _nbytes((tm, D), F32) + _nbytes((D, D), BF16)),
        name="rwkv_out",
    )(y, g, x, mod, gain, w_out)


def _mla_proj_kernel(x_ref, shift_ref, scale_ref, kshift_ref, kscale_ref, gain_ref, kgain_ref,
                     cos_ref, sin_ref, wqa_ref, wgate_ref, wkv_ref, qnorm_ref, kvnorm_ref,
                     wqn_ref, wqp_ref, wqs_ref, wuk_ref,
                     q_ref, kcat_ref, ckv_ref, kpe_ref, sg_ref):
    x = x_ref[0]
    ms = jnp.mean(x * x, axis=-1, keepdims=True)
    xn = x * lax.rsqrt(ms + EPS)
    h = xn * gain_ref[...] * (1.0 + scale_ref[0]) + shift_ref[0]
    hk = xn * kgain_ref[...] * (1.0 + kscale_ref[0]) + kshift_ref[0]
    cos = cos_ref[...]
    sin = sin_ref[...]

    kv = _dot(hk, wkv_ref[...])
    ckv = _rmsnorm(kv[:, :KV_LORA], kvnorm_ref[...])
    kpe = kv[:, KV_LORA:KV_LORA + 128] * cos + kv[:, KV_LORA + 128:] * sin
    ckv_ref[0] = ckv
    kpe_ref[0] = kpe[:, :QK_ROPE]
    kcat_ref[0, :, :KV_LORA] = ckv.astype(BF16)
    kcat_ref[0, :, KV_LORA:] = kpe.astype(BF16)

    g = _dot(h, wgate_ref[...])
    sg_ref[0] = (g * _sigmoid(g)).astype(sg_ref.dtype)

    qn = _rmsnorm(_dot(h, wqa_ref[...]), qnorm_ref[...]).astype(BF16)
    q_nope = _dot(qn, wqn_ref[...])
    q_pe = _dot(qn, wqp_ref[...])
    q_ps = _dot(qn, wqs_ref[...])
    for hd in range(MLA_H):
        cols = slice(hd * 128, (hd + 1) * 128)
        q_lat = _dot(q_nope[:, cols], wuk_ref[hd])
        q_ref[0, hd, :, :KV_LORA] = (q_lat * Q_SCALE).astype(BF16)
        q_ref[0, hd, :, KV_LORA:] = ((q_pe[:, cols] * cos + q_ps[:, cols] * sin) * Q_SCALE).astype(BF16)


def _mla_proj_call(x, mod, kvmod, gain, kgain, cos, sin, wqa, wgate, wkv, qnorm, kvnorm,
                   wqn, wqp, wqs, wuk, *, per_row, tm):
    B, T, D = x.shape
    tile = lambda w: pl.BlockSpec((1, tm, w), lambda b, t: (b, t, 0))
    full = _const_spec
    tab = pl.BlockSpec((tm, 128), lambda b, t: (t, 0))
    weights = (wqa, wgate, wkv, wqn, wqp, wqs, wuk)
    pipelined = (2 * _nbytes((tm, D), F32) + _nbytes((MLA_H, tm, KCAT), BF16)
                 + 2 * _nbytes((tm, KCAT), F32))
    weight_bytes = sum(_nbytes(w.shape, BF16) for w in weights)
    return pl.pallas_call(
        _mla_proj_kernel,
        grid=(B, T // tm),
        in_specs=[tile(D), _mod_spec(per_row, tm, D, 0), _mod_spec(per_row, tm, D, 1),
                  _mod_spec(per_row, tm, D, 0), _mod_spec(per_row, tm, D, 1),
                  full((1, D)), full((1, D)), tab, tab,
                  full(wqa.shape), full(wgate.shape), full(wkv.shape), full((1, Q_LORA)), full((1, KV_LORA)),
                  full(wqn.shape), full(wqp.shape), full(wqs.shape), full(wuk.shape)],
        out_specs=[pl.BlockSpec((1, MLA_H, tm, KCAT), lambda b, t: (b, 0, t, 0)),
                   tile(KCAT), tile(KV_LORA), tile(QK_ROPE), tile(MLA_H * V_HEAD)],
        out_shape=[jax.ShapeDtypeStruct((B, MLA_H, T, KCAT), BF16),
                   jax.ShapeDtypeStruct((B, T, KCAT), BF16),
                   jax.ShapeDtypeStruct((B, T, KV_LORA), F32),
                   jax.ShapeDtypeStruct((B, T, QK_ROPE), F32),
                   jax.ShapeDtypeStruct((B, T, MLA_H * V_HEAD), BF16)],
        compiler_params=_params(("parallel", "parallel"), pipelined,
                                12 * _nbytes((tm, D), F32) + weight_bytes),
        name="mla_proj",
    )(x, mod, mod, kvmod, kvmod, gain, kgain, cos, sin, wqa, wgate, wkv, qnorm, kvnorm,
      wqn, wqp, wqs, wuk)


def _lane_tile(t, width):
    return jnp.concatenate([t] * (width // V7X_LANES), axis=1)


def _flash_kernel(qi_ref, ki_ref, q_ref, k_ref, o_ref, m_scr, l_scr, acc_scr, *, tq):
    step = pl.program_id(1)
    qi = qi_ref[step]
    ki = ki_ref[step]

    @pl.when(ki == 0)
    def _():
        m_scr[...] = jnp.full_like(m_scr, -jnp.inf)
        l_scr[...] = jnp.zeros_like(l_scr)
        acc_scr[...] = jnp.zeros_like(acc_scr)

    def update(masked):
        kc = k_ref[0]
        vc = kc[:, :KV_LORA]
        scores = lambda hd: lax.dot_general(q_ref[0, hd], kc, _NT, preferred_element_type=F32)
        if masked:
            causal = (lax.broadcasted_iota(jnp.int32, (tq, tq), 1)
                      <= lax.broadcasted_iota(jnp.int32, (tq, tq), 0))
        groups = [range(g, g + FLASH_HEAD_GROUP) for g in range(0, MLA_H, FLASH_HEAD_GROUP)]
        s_next = [scores(hd) for hd in groups[0]]
        for gi, heads in enumerate(groups):
            s = s_next
            if gi + 1 < len(groups):
                s_next = [scores(hd) for hd in groups[gi + 1]]
            if masked:
                s = [jnp.where(causal, t, -jnp.inf) for t in s]
            m_prev = [m_scr[hd] for hd in heads]
            m_new = [jnp.maximum(mp, jnp.max(t, axis=-1, keepdims=True)) for mp, t in zip(m_prev, s)]
            alpha = [jnp.exp2(mp - mn) for mp, mn in zip(m_prev, m_new)]
            p = [jnp.exp2(t - _lane_tile(mn, tq)) for t, mn in zip(s, m_new)]
            pv = [jnp.dot(t.astype(BF16), vc, preferred_element_type=F32) for t in p]
            for i, hd in enumerate(heads):
                l_scr[hd] = alpha[i] * l_scr[hd] + jnp.sum(p[i], axis=-1, keepdims=True)
                acc_scr[hd] = _lane_tile(alpha[i], KV_LORA) * acc_scr[hd] + pv[i]
                m_scr[hd] = m_new[i]

    @pl.when(ki < qi)
    def _():
        update(False)

    @pl.when(ki == qi)
    def _():
        update(True)
        for hd in range(MLA_H):
            o_ref[0, hd] = (acc_scr[hd] * _lane_tile(1.0 / l_scr[hd], KV_LORA)).astype(BF16)


def _flash_call(q, kcat, *, tq):
    B, H, T, _ = q.shape
    nq = T // tq
    pairs = [(i, j) for i in range(nq) for j in range(i + 1)]
    qi_tab = jnp.asarray([i for i, _ in pairs], jnp.int32)
    ki_tab = jnp.asarray([j for _, j in pairs], jnp.int32)
    pipelined = (_nbytes((H, tq, KCAT), BF16) + _nbytes((tq, KCAT), BF16)
                 + _nbytes((H, tq, KV_LORA), BF16))
    resident = _nbytes((H, tq, KV_LORA + 2 * V7X_LANES), F32) + 8 * _nbytes((tq, tq), F32)
    return pl.pallas_call(
        functools.partial(_flash_kernel, tq=tq),
        grid_spec=pltpu.PrefetchScalarGridSpec(
            num_scalar_prefetch=2,
            grid=(B, len(pairs)),
            in_specs=[pl.BlockSpec((1, H, tq, KCAT), lambda b, s, qt, kt: (b, 0, qt[s], 0)),
                      pl.BlockSpec((1, tq, KCAT), lambda b, s, qt, kt: (b, kt[s], 0))],
            out_specs=pl.BlockSpec((1, H, tq, KV_LORA), lambda b, s, qt, kt: (b, 0, qt[s], 0)),
            scratch_shapes=[pltpu.VMEM((H, tq, V7X_LANES), F32), pltpu.VMEM((H, tq, V7X_LANES), F32),
                            pltpu.VMEM((H, tq, KV_LORA), F32)]),
        out_shape=jax.ShapeDtypeStruct((B, H, T, KV_LORA), BF16),
        compiler_params=_params(("parallel", "arbitrary"), pipelined, resident),
        name="mla_flash",
    )(qi_tab, ki_tab, q, kcat)


def _decode_kernel(pt_ref, q_ref, cnew_ref, pnew_ref, ckv_hbm, kpe_hbm, o_ref,
                   ck_buf, kp_buf, sems, keys_buf, rope_buf):
    b = pl.program_id(0)
    n_seq = pl.num_programs(0)
    n_pages, ps = ck_buf.shape[1], ck_buf.shape[2]
    slot = lax.rem(b, 2)

    def page_copies(seq, slot_, i):
        page = pt_ref[seq, i]
        return (pltpu.make_async_copy(ckv_hbm.at[page], ck_buf.at[slot_, i], sems.at[slot_]),
                pltpu.make_async_copy(kpe_hbm.at[page], kp_buf.at[slot_, i], sems.at[slot_]))

    def start_gather(seq, slot_):
        for i in range(n_pages):
            for cp in page_copies(seq, slot_, i):
                cp.start()

    @pl.when(b == 0)
    def _():
        start_gather(0, 0)

    @pl.when(b + 1 < n_seq)
    def _():
        start_gather(b + 1, 1 - slot)

    for i in range(n_pages):
        for cp in page_copies(b, slot, i):
            cp.wait()

    for i in range(n_pages):
        keys_buf[i * ps:(i + 1) * ps, :] = ck_buf[slot, i].astype(BF16)
        rope_buf[:, i * ps:(i + 1) * ps] = kp_buf[slot, i].astype(BF16)
    q = q_ref[0]
    q_lat = q[:, :KV_LORA]
    q_pe = q[:, KV_LORA:KV_LORA + QK_ROPE]
    keys = keys_buf[...]
    s = (lax.dot_general(q_lat, keys, _NT, preferred_element_type=F32)
         + jnp.dot(q_pe, rope_buf[...], preferred_element_type=F32))
    cn = cnew_ref[0]
    s_new = (jnp.sum(q_lat.astype(F32) * cn, axis=-1, keepdims=True)
             + jnp.sum(q_pe.astype(F32) * pnew_ref[0], axis=-1, keepdims=True))
    m = jnp.maximum(jnp.max(s, axis=-1, keepdims=True), s_new)
    p = jnp.exp2(s - m)
    p_new = jnp.exp2(s_new - m)
    denom = jnp.sum(p, axis=-1, keepdims=True) + p_new
    acc = jnp.dot(p.astype(BF16), keys, preferred_element_type=F32) + p_new * cn
    o_ref[0] = acc / denom


def _decode_call(page_table, q, c_new, p_new, cache_ckv, cache_kpe):
    B, H, _ = q.shape
    n_pages = page_table.shape[1]
    ps = cache_ckv.shape[1]
    past = n_pages * ps
    scratch = [((2, n_pages, ps, KV_LORA), F32),
               ((2, n_pages, QK_ROPE, ps), F32),
               ((past, KV_LORA), BF16),
               ((QK_ROPE, past), BF16)]
    scratch_bytes = sum(_nbytes(shape, dt) for shape, dt in scratch)
    (ck, kp, kb, rb) = [pltpu.VMEM(shape, dt) for shape, dt in scratch]
    return pl.pallas_call(
        _decode_kernel,
        grid_spec=pltpu.PrefetchScalarGridSpec(
            num_scalar_prefetch=1,
            grid=(B,),
            in_specs=[pl.BlockSpec((1, H, KCAT), lambda b, pt: (b, 0, 0)),
                      pl.BlockSpec((1, 1, KV_LORA), lambda b, pt: (b, 0, 0)),
                      pl.BlockSpec((1, 1, QK_ROPE), lambda b, pt: (b, 0, 0)),
                      pl.BlockSpec(memory_space=pl.ANY),
                      pl.BlockSpec(memory_space=pl.ANY)],
            out_specs=pl.BlockSpec((1, H, KV_LORA), lambda b, pt: (b, 0, 0)),
            scratch_shapes=[ck, kp, pltpu.SemaphoreType.DMA((2,)), kb, rb]),
        out_shape=jax.ShapeDtypeStruct((B, H, KV_LORA), F32),
        compiler_params=_params(("arbitrary",), _nbytes((H, KCAT), F32),
                                scratch_bytes + 4 * _nbytes((H, past), F32)),
        name="mla_decode",
    )(page_table, q, c_new, p_new, cache_ckv, cache_kpe)


def _mla_out_kernel(o_ref, sg_ref, x_ref, gate_ref, gain_ref, wuv_ref, wout_ref, y_ref, og_scr):
    for hd in range(MLA_H):
        cols = slice(hd * V_HEAD, (hd + 1) * V_HEAD)
        o = _dot(o_ref[0, hd], wuv_ref[hd])
        og_scr[:, cols] = (o * sg_ref[0, :, cols].astype(F32)).astype(BF16)
    z = _dot(og_scr[...], wout_ref[...])
    y_ref[0] = x_ref[0] + gate_ref[0] * _rmsnorm(z, gain_ref[...])


def _mla_out_call(o_lat, sg, x, mod, gain, wuv, wout, *, per_row, tm):
    B, T, D = x.shape
    tile = lambda w: pl.BlockSpec((1, tm, w), lambda b, t: (b, t, 0))
    full = _const_spec
    pipelined = _nbytes((MLA_H, tm, KV_LORA), BF16) + 3 * _nbytes((tm, D), F32)
    weight_bytes = _nbytes(wuv.shape, BF16) + _nbytes(wout.shape, BF16)
    return pl.pallas_call(
        _mla_out_kernel,
        grid=(B, T // tm),
        in_specs=[pl.BlockSpec((1, MLA_H, tm, KV_LORA), lambda b, t: (b, 0, t, 0)),
                  tile(MLA_H * V_HEAD), tile(D), _mod_spec(per_row, tm, D, 2), full((1, D)),
                  full(wuv.shape), full(wout.shape)],
        out_specs=tile(D),
        out_shape=jax.ShapeDtypeStruct((B, T, D), F32),
        scratch_shapes=[pltpu.VMEM((tm, MLA_H * V_HEAD), BF16)],
        compiler_params=_params(("parallel", "parallel"), pipelined,
                                4 * _nbytes((tm, D), F32) + weight_bytes),
        name="mla_out",
    )(o_lat, sg, x, mod, gain, wuv, wout)


def _rope_tables(pos):
    half = QK_ROPE // 2
    inv = ROPE_THETA ** (-jnp.arange(half, dtype=F32) / half)
    ang = pos.astype(F32)[:, None] * inv[None, :]
    c, s = jnp.cos(ang), jnp.sin(ang)
    z = jnp.zeros((pos.shape[0], 128 - QK_ROPE), F32)
    return jnp.concatenate([c, c, z], axis=1), jnp.concatenate([-s, s, z], axis=1)


def _swap_halves(w):
    half = w.shape[-1] // 2
    return jnp.concatenate([w[..., half:], w[..., :half]], axis=-1)


def _pad_lanes(w):
    return jnp.concatenate([w, jnp.zeros(w.shape[:-1] + (128 - w.shape[-1],), w.dtype)], axis=-1)


def kernel(x_prompt, x_sample, c_prompt, c_sample, state_wkv, state_shift, cache_kv_latent, cache_k_rope, page_table, ada_w, ada_b, norm_pre, norm_post, a_mu, a_w_in, a_w0, a_w1, a_w2, a_a0, a_a1, a_a2, a_k_k, a_k_a, a_r_k, a_gn_w, a_gn_b, a_w_out, kv_ada_w, kv_ada_b, kv_norm, kv_w_a, kv_a_norm, kv_w_b, b_w_in, b_q_norm, b_w_q, b_w_out):
    B, T, D = x_prompt.shape
    DB = x_sample.shape[0]
    H = D // RWKV_HEAD
    assert ada_w.shape[0] == 2 and a_mu.shape[0] == 1 and b_w_in.shape[0] == 1
    assert x_sample.shape[1] == 1 and T % ROW_TILE == 0 and DB % V7X_SUBLANES == 0

    c_all = jnp.concatenate([c_prompt, c_sample], axis=0)
    mods = _ada_call(c_all, ada_w, ada_b)
    kvmods = _ada_call(c_all, kv_ada_w[None], kv_ada_b[None])
    mod_p = [mods[i, :B].reshape(B, 1, 3 * D) for i in range(2)]
    mod_s = [mods[i, B:].reshape(1, DB, 3 * D) for i in range(2)]
    kvmod_p = kvmods[0, :B].reshape(B, 1, 2 * D)
    kvmod_s = kvmods[0, B:].reshape(1, DB, 2 * D)

    row = lambda v: v.reshape(1, -1)
    w_in = a_w_in[0].astype(BF16)
    a_args = (row(norm_pre[0]), a_mu[0], w_in, row(a_w0[0]), a_w1[0].astype(BF16), a_w2[0].astype(BF16),
              row(a_a0[0]), a_a1[0].astype(BF16), a_a2[0].astype(BF16))
    w_out_a = a_w_out[0].astype(BF16)
    k_k, k_a, r_k = row(a_k_k[0]), row(a_k_a[0]), row(a_r_k[0])
    gn_w, gn_b = row(a_gn_w[0]), row(a_gn_b[0])

    w_bin = b_w_in[0]
    wqa = w_bin[:, :Q_LORA].astype(BF16)
    wgate = w_bin[:, Q_LORA:].astype(BF16)
    kv_pe = kv_w_a[:, KV_LORA:]
    wkv = jnp.concatenate([kv_w_a[:, :KV_LORA], _pad_lanes(kv_pe), _pad_lanes(_swap_halves(kv_pe))],
                          axis=1).astype(BF16)
    w_q = b_w_q[0]
    wqn = w_q[:, :, :QK_NOPE].reshape(Q_LORA, MLA_H * QK_NOPE).astype(BF16)
    wq_pe = w_q[:, :, QK_NOPE:]
    wqp = _pad_lanes(wq_pe).reshape(Q_LORA, MLA_H * 128).astype(BF16)
    wqs = _pad_lanes(_swap_halves(wq_pe)).reshape(Q_LORA, MLA_H * 128).astype(BF16)
    wuk = jnp.transpose(kv_w_b[:, :, :QK_NOPE], (1, 2, 0)).astype(BF16)
    wuv = jnp.transpose(kv_w_b[:, :, QK_NOPE:], (1, 0, 2)).astype(BF16)
    wout_b = b_w_out[0].astype(BF16)
    b_args = (wqa, wgate, wkv, row(b_q_norm[0]), row(kv_a_norm), wqn, wqp, wqs, wuk)

    tm = ROW_TILE
    r, lw, k, v, a, g, h_last = _rwkv_proj_call(x_prompt, x_prompt, mod_p[0], *a_args,
                                                seq_shift=True, tm=tm)
    yw, wkv_p = _wkv_chunk_call(r, lw, k, v, a, k_k, k_a, r_k, gn_w, gn_b,
                                t_block=min(T, 512), n_pairs=8)
    x1 = _rwkv_out_call(yw, g, x_prompt, mod_p[0], row(norm_post[0]), w_out_a, per_row=False, tm=tm)
    wkv_p = wkv_p[None]
    shift_p = h_last.reshape(1, B, D)

    cos_p, sin_p = _rope_tables(jnp.arange(T, dtype=jnp.int32))
    q_p, kcat_p, ckv_p, kpe_p, sg_p = _mla_proj_call(
        x1, mod_p[1], kvmod_p, row(norm_pre[1]), row(kv_norm), cos_p, sin_p, *b_args,
        per_row=False, tm=tm)
    o_p = _flash_call(q_p, kcat_p, tq=min(T, 256))
    y_prompt = _mla_out_call(o_p, sg_p, x1, mod_p[1], row(norm_post[1]), wuv, wout_b,
                             per_row=False, tm=tm)

    xs = x_sample.reshape(1, DB, D)
    rs, lws, ks, vs, as_, gs, hs = _rwkv_proj_call(xs, state_shift[0].reshape(1, DB, D), mod_s[0],
                                                   *a_args, seq_shift=False, tm=DB)
    bm = lambda t: jnp.swapaxes(t.reshape(DB, D), 0, 1)
    pb = lambda t: jnp.broadcast_to(t.reshape(D, 1), (D, DB))
    s_new, yws = _wkv_step_call(jnp.transpose(state_wkv[0], (1, 2, 3, 0)),
                                bm(rs), bm(lws), bm(ks), bm(vs), bm(as_),
                                pb(k_k), pb(k_a), pb(r_k), pb(gn_w), pb(gn_b))
    s_new = jnp.transpose(s_new, (3, 0, 1, 2))
    x1s = _rwkv_out_call(jnp.swapaxes(yws, 0, 1).reshape(1, DB, D), gs, xs, mod_s[0],
                         row(norm_post[0]), w_out_a, per_row=True, tm=DB)
    n_pages = page_table.shape[1]
    past_len = n_pages * cache_kv_latent.shape[1]
    cos_s, sin_s = _rope_tables(jnp.full((DB,), past_len, dtype=jnp.int32))
    q_s, _, ckv_s, kpe_s, sg_s = _mla_proj_call(
        x1s, mod_s[1], kvmod_s, row(norm_pre[1]), row(kv_norm), cos_s, sin_s, *b_args,
        per_row=True, tm=DB)
    o_s = _decode_call(page_table, jnp.transpose(q_s[0], (1, 0, 2)),
                       ckv_s.reshape(DB, 1, KV_LORA), kpe_s.reshape(DB, 1, QK_ROPE),
                       cache_kv_latent, jnp.swapaxes(cache_k_rope, 1, 2))
    o_s = jnp.transpose(o_s, (1, 0, 2)).astype(BF16)[None]
    y_s = _mla_out_call(o_s, sg_s, x1s, mod_s[1], row(norm_post[1]), wuv, wout_b,
                        per_row=True, tm=DB)

    return (y_prompt, y_s.reshape(DB, 1, D), wkv_p, shift_p, ckv_p, kpe_p,
            s_new[None], hs.reshape(1, DB, D), ckv_s.reshape(DB, 1, KV_LORA),
            kpe_s.reshape(DB, 1, QK_ROPE))
```

```python
import functools
import math

import jax
import jax.numpy as jnp
from jax import lax
from jax.experimental import pallas as pl
from jax.experimental.pallas import tpu as pltpu

F32 = jnp.float32
BF16 = jnp.bfloat16
HIGHEST = lax.Precision.HIGHEST

RWKV_HEAD = 64
GN_EPS = 64e-5
EPS = 1e-6
MLA_H = 8
QK_NOPE = 128
QK_ROPE = 64
V_HEAD = 128
Q_LORA = 384
KV_LORA = 256
ROPE_THETA = 10000.0
ATTN_SCALE = (QK_NOPE + QK_ROPE) ** -0.5
Q_SCALE = ATTN_SCALE * math.log2(math.e)
KCAT = KV_LORA + 128

V7X_LANES = 128
V7X_SUBLANES = 8
V7X_VMEM_BYTES = 64 * 1024 * 1024
V7X_VMEM_REQUEST_CAP = V7X_VMEM_BYTES - 8 * 1024 * 1024

WKV_CHUNK = 64
ROW_TILE = 512
FLASH_HEAD_GROUP = 2


def _vmem_limit(pipelined_bytes, resident_bytes=0):
    est = 2 * pipelined_bytes + resident_bytes + 4 * 1024 * 1024
    return int(min(max(est, 16 * 1024 * 1024), V7X_VMEM_REQUEST_CAP))


def _nbytes(shape, dtype):
    return math.prod(shape) * jnp.dtype(dtype).itemsize


def _params(sem, pipelined_bytes, resident_bytes=0):
    return pltpu.CompilerParams(
        dimension_semantics=sem,
        vmem_limit_bytes=_vmem_limit(pipelined_bytes, resident_bytes))


def _dot(a, b):
    return jnp.dot(a.astype(BF16), b.astype(BF16), preferred_element_type=F32)


def _dot_hi(a, b, dims=(((1,), (0,)), ((), ()))):
    return lax.dot_general(a, b, dims, precision=HIGHEST, preferred_element_type=F32)


_NT = (((1,), (1,)), ((), ()))
_TN = (((0,), (0,)), ((), ()))


def _sigmoid(x):
    return 1.0 / (1.0 + jnp.exp(-x))


def _ada_kernel(c_ref, w_ref, b_ref, o_ref):
    o_ref[0] = _dot_hi(c_ref[...], w_ref[0]) + b_ref[0]


def _ada_call(c, w, b):
    G, D, N = w.shape
    M = c.shape[0]
    tn = 1024
    pipelined = _nbytes((D, tn), F32) + _nbytes((M, tn), F32) + _nbytes((M, D), F32)
    return pl.pallas_call(
        _ada_kernel,
        grid=(G, N // tn),
        in_specs=[
            pl.BlockSpec((M, D), lambda g, j: (0, 0)),
            pl.BlockSpec((1, D, tn), lambda g, j: (g, 0, j)),
            pl.BlockSpec((1, 1, tn), lambda g, j: (g, 0, j)),
        ],
        out_specs=pl.BlockSpec((1, M, tn), lambda g, j: (g, 0, j)),
        out_shape=jax.ShapeDtypeStruct((G, M, N), F32),
        compiler_params=_params(("parallel", "parallel"), pipelined),
        name="ada_modulation",
    )(c, w, b.reshape(G, 1, N))


def _const_spec(shape):
    return pl.BlockSpec(shape, lambda *_: (0,) * len(shape), pipeline_mode=pl.Buffered(1))


def _mod_spec(per_row, tm, D, col):
    if per_row:
        return pl.BlockSpec((1, tm, D), lambda b, t: (b, t, col))
    return pl.BlockSpec((1, 1, D), lambda b, t: (b, 0, col))


def _modnorm(x, gain, scale, shift):
    ms = jnp.mean(x * x, axis=-1, keepdims=True)
    return x * lax.rsqrt(ms + EPS) * gain * (1.0 + scale) + shift


def _rmsnorm(x, gain):
    ms = jnp.mean(x * x, axis=-1, keepdims=True)
    return x * lax.rsqrt(ms + EPS) * gain


def _rwkv_proj_kernel(x_ref, prev_ref, shift_ref, scale_ref, gain_ref, mu_ref, win_ref,
                      w0_ref, w1_ref, w2_ref, a0_ref, a1_ref, a2_ref,
                      r_ref, lw_ref, k_ref, v_ref, a_ref, g_ref, hlast_ref, *, seq_shift):
    x = x_ref[0]
    gain = gain_ref[...]
    scale = scale_ref[0]
    shift = shift_ref[0]
    h = _modnorm(x, gain, scale, shift)
    tm = h.shape[0]
    if seq_shift:
        hp = _modnorm(prev_ref[0][V7X_SUBLANES - 1:V7X_SUBLANES, :], gain, scale, shift)
        hp = jnp.where(pl.program_id(1) == 0, 0.0, hp)
        row = lax.broadcasted_iota(jnp.int32, (tm, 1), 0)
        hs = jnp.where(row == 0, hp, pltpu.roll(h, 1, axis=0))
        hlast_ref[0] = h[tm - 1:tm, :]
    else:
        hs = prev_ref[0]
        hlast_ref[0] = h
    xx = hs - h
    mu = mu_ref[...]
    outs = (r_ref, k_ref, v_ref, g_ref)
    for m in range(4):
        xm = h + xx * mu[m:m + 1, :]
        outs[m][0] = _dot(xm, win_ref[m]).astype(outs[m].dtype)
    xw = h + xx * mu[4:5, :]
    xa = h + xx * mu[5:6, :]
    wl = w0_ref[...] + _dot(jnp.tanh(_dot(xw, w1_ref[...])), w2_ref[...])
    z = -wl
    softplus = jnp.maximum(z, 0.0) + jnp.log(1.0 + jnp.exp(-jnp.abs(z)))
    lw_ref[0] = -jnp.exp(-softplus - 0.5)
    al = a0_ref[...] + _dot(_dot(xa, a1_ref[...]), a2_ref[...])
    a_ref[0] = _sigmoid(al).astype(a_ref.dtype)


def _rwkv_proj_call(x, prev, mod, gain, mu, w_in, w0, w1, w2, a0, a1, a2, *, seq_shift, tm):
    B, T, D = x.shape
    per_row = not seq_shift
    nt = T // tm
    tile = pl.BlockSpec((1, tm, D), lambda b, t: (b, t, 0))
    if seq_shift:
        sub = tm // V7X_SUBLANES
        prev_spec = pl.BlockSpec((1, V7X_SUBLANES, D),
                                 lambda b, t: (b, jnp.maximum(t * sub - 1, 0), 0))
        hlast_shape = jax.ShapeDtypeStruct((B, 1, D), F32)
        hlast_spec = pl.BlockSpec((1, 1, D), lambda b, t: (b, 0, 0))
        sem = ("parallel", "arbitrary")
    else:
        prev_spec = tile
        hlast_shape = jax.ShapeDtypeStruct((B, T, D), F32)
        hlast_spec = tile
        sem = ("parallel", "parallel")
    full = _const_spec
    lora = w1.shape[1]
    pipelined = 5 * _nbytes((tm, D), F32)
    resident = (8 * _nbytes((tm, D), F32) + _nbytes((4, D, D), BF16)
                + 4 * _nbytes((D, V7X_LANES), BF16))
    out_sds = lambda dt: jax.ShapeDtypeStruct((B, T, D), dt)
    return pl.pallas_call(
        functools.partial(_rwkv_proj_kernel, seq_shift=seq_shift),
        grid=(B, nt),
        in_specs=[tile, prev_spec, _mod_spec(per_row, tm, D, 0), _mod_spec(per_row, tm, D, 1),
                  full((1, D)), full((6, D)), full((4, D, D)),
                  full((1, D)), full((D, lora)), full((lora, D)),
                  full((1, D)), full((D, lora)), full((lora, D))],
        out_specs=[tile] * 6 + [hlast_spec],
        out_shape=[out_sds(BF16), out_sds(F32)] + [out_sds(BF16)] * 4 + [hlast_shape],
        compiler_params=_params(sem, pipelined, resident),
        name="rwkv_proj",
    )(x, prev, mod, mod, gain, mu, w_in, w0, w1, w2, a0, a1, a2)


def _cumsum_rows(cum, x):
    hi = x.astype(BF16)
    rest = x - hi.astype(F32)
    mid = rest.astype(BF16)
    lo = (rest - mid.astype(F32)).astype(BF16)
    dot = lambda t: jnp.dot(cum, t, preferred_element_type=F32)
    return dot(hi) + dot(mid) + dot(lo)


def _wkv_chunk_kernel(r_ref, lw_ref, k_ref, v_ref, a_ref, kk_ref, ka_ref, rk_ref, gw_ref, gb_ref,
                      y_ref, sfin_ref, s_scr, wr_scr, tinv_scr, av_scr, ar_scr, bkp_scr, v2_scr, pend_scr,
                      y2_scr, bias_scr, *, n_chunks, n_pairs):
    L = WKV_CHUNK
    L2 = 2 * L
    tc = pl.program_id(2)

    @pl.when(tc == 0)
    def _():
        s_scr[...] = jnp.zeros_like(s_scr)

    lane = lax.broadcasted_iota(jnp.int32, (L2, V7X_LANES), 1)
    srow = lax.broadcasted_iota(jnp.int32, (L2, V7X_LANES), 0)
    own = (srow < L) == (lane < RWKV_HEAD)
    first_head = lax.broadcasted_iota(jnp.int32, (L, V7X_LANES), 1) < RWKV_HEAD
    ti = lax.broadcasted_iota(jnp.int32, (L, L), 0)
    tj = lax.broadcasted_iota(jnp.int32, (L, L), 1)
    cum = (ti >= tj).astype(BF16)
    ri = lax.broadcasted_iota(jnp.int32, (L2, L2), 0)
    rj = lax.broadcasted_iota(jnp.int32, (L2, L2), 1)
    strict = ri > rj
    incl = ri >= rj
    eye = (ri == rj).astype(F32)
    n_rounds = int(math.log2(L)) - 1

    def stack(x):
        return jnp.where(own, jnp.concatenate([x, x], axis=0), 0.0)

    def fold(x2):
        return x2[:L] + x2[L:]

    def precompute(c, carry):
        t0 = pl.multiple_of(c * L, L)
        pairs = range(n_pairs)
        cols = [slice(p * V7X_LANES, (p + 1) * V7X_LANES) for p in pairs]
        load = lambda ref: [ref[0, pl.ds(t0, L), cols[p]].astype(F32) for p in pairs]
        r, lw, k, v, a = load(r_ref), load(lw_ref), load(k_ref), load(v_ref), load(a_ref)

        def head_sum(t):
            s0 = jnp.sum(jnp.where(first_head, t, 0.0), axis=-1, keepdims=True)
            s1 = jnp.sum(t, axis=-1, keepdims=True) - s0
            return jnp.where(first_head, s0, s1)

        kkr = [k[p] * kk_ref[:, cols[p]] for p in pairs]
        nrm = [jnp.sqrt(head_sum(kkr[p] * kkr[p])) for p in pairs]
        kk = [kkr[p] / jnp.maximum(nrm[p], 1e-12) for p in pairs]
        b = [kk[p] * a[p] for p in pairs]
        km = [k[p] * (1.0 + (a[p] - 1.0) * ka_ref[:, cols[p]]) for p in pairs]
        v2 = [stack(v[p]) for p in pairs]

        cs = [_cumsum_rows(cum, lw[p]) for p in pairs]
        cs_end = [cs[p][L - 1:L, :] for p in pairs]
        wr = [jnp.concatenate([stack(kk[p] * jnp.exp(cs[p] - lw[p])), stack(r[p] * jnp.exp(cs[p]))],
                              axis=0).astype(BF16) for p in pairs]
        e_neg = [jnp.exp(-cs[p]) for p in pairs]
        bk = [jnp.concatenate([stack(b[p] * e_neg[p]), stack(km[p] * e_neg[p])], axis=0).astype(BF16)
              for p in pairs]
        aa = [lax.dot_general(wr[p], bk[p], _NT, preferred_element_type=F32) for p in pairs]

        m0 = [jnp.where(strict, -aa[p][:L2, :L2], 0.0) for p in pairs]
        m = m0
        x = [eye for p in pairs]
        for it in range(n_rounds):
            if it + 1 < n_rounds:
                mx = [_dot(m[p], jnp.concatenate([m[p], x[p]], axis=1)) for p in pairs]
                x = [x[p] + mx[p][:, L2:] for p in pairs]
                m = [mx[p][:, :L2] for p in pairs]
            else:
                x = [x[p] + _dot(m[p], x[p]) for p in pairs]
        res = [eye - x[p] + _dot(m0[p], x[p]) for p in pairs]
        x = [x[p] + _dot(x[p], res[p]) for p in pairs]
        av = [_dot(jnp.where(strict, aa[p][:L2, L2:], 0.0), v2[p]) for p in pairs]

        for p in pairs:
            e_end = jnp.exp(cs_end[p] - cs[p])
            wr_scr[c, p] = wr[p]
            tinv_scr[c, p] = x[p].astype(BF16)
            av_scr[c, p] = av[p]
            ar_scr[c, p] = jnp.where(jnp.concatenate([incl, incl], axis=1), aa[p][L2:, :], 0.0).astype(BF16)
            bkp_scr[c, p] = jnp.concatenate([stack(b[p] * e_end), stack(km[p] * e_end)],
                                            axis=0).astype(BF16)
            v2_scr[c, p] = v2[p].astype(BF16)
            pend_scr[c, p] = jnp.broadcast_to(jnp.exp(cs_end[p]), (V7X_SUBLANES, V7X_LANES))
            bonus = head_sum(r[p] * km[p] * rk_ref[:, cols[p]]) * v[p]
            bias_scr[c, p] = gb_ref[:, cols[p]] + bonus
        return carry

    def recur(c, carry):
        pairs = range(n_pairs)
        s0 = [s_scr[p] for p in pairs]
        g = [lax.dot_general(wr_scr[c, p], s0[p].astype(BF16), _NT, preferred_element_type=F32)
             for p in pairs]
        u = [_dot(tinv_scr[c, p], -(g[p][:L2] + av_scr[c, p])) for p in pairs]
        uv = [jnp.concatenate([u[p].astype(BF16), v2_scr[c, p]], axis=0) for p in pairs]
        for p in pairs:
            s_scr[p] = (s0[p] * pend_scr[c, p][0:1, :]
                        + lax.dot_general(uv[p], bkp_scr[c, p], _TN, preferred_element_type=F32))
        for p in pairs:
            y2_scr[c, p] = g[p][L2:] + jnp.dot(ar_scr[c, p], uv[p], preferred_element_type=F32)
        return carry

    def normalise(c, carry):
        t0 = c * L if isinstance(c, int) else pl.multiple_of(c * L, L)
        pairs = range(n_pairs)
        y2 = [y2_scr[c, p] for p in pairs]
        mean = [jnp.sum(y2[p], axis=-1, keepdims=True) * (1.0 / RWKV_HEAD) for p in pairs]
        cen = [jnp.where(own, y2[p] - mean[p], 0.0) for p in pairs]
        var = [jnp.sum(cen[p] * cen[p], axis=-1, keepdims=True) * (1.0 / RWKV_HEAD) for p in pairs]
        for p in pairs:
            cols = slice(p * V7X_LANES, (p + 1) * V7X_LANES)
            yn = fold(cen[p] * lax.rsqrt(var[p] + GN_EPS))
            y_ref[0, pl.ds(t0, L), cols] = (bias_scr[c, p] + yn * gw_ref[:, cols]).astype(y_ref.dtype)
        return carry

    def recur_and_normalise(c, carry):
        normalise(c - 1, carry)
        return recur(c, carry)

    lax.fori_loop(0, n_chunks, precompute, 0)
    recur(0, 0)
    lax.fori_loop(1, n_chunks, recur_and_normalise, 0)
    normalise(n_chunks - 1, 0)

    @pl.when(tc == pl.num_programs(2) - 1)
    def _():
        for p in range(n_pairs):
            s = s_scr[p]
            sfin_ref[0, 2 * p] = s[:RWKV_HEAD, :RWKV_HEAD]
            sfin_ref[0, 2 * p + 1] = s[RWKV_HEAD:, RWKV_HEAD:]


def _wkv_chunk_call(r, lw, k, v, a, k_k, k_a, r_k, gn_w, gn_b, *, t_block, n_pairs):
    B, T, D = r.shape
    L2 = 2 * WKV_CHUNK
    n_chunks = t_block // WKV_CHUNK
    wcol = n_pairs * V7X_LANES
    n_col = D // wcol
    tile = pl.BlockSpec((1, t_block, wcol), lambda b, p, t: (b, t, p))
    vec = pl.BlockSpec((1, wcol), lambda b, p, t: (0, p))
    per_chunk = lambda shape, dt: ((n_chunks, n_pairs) + shape, dt)
    per_chunk_scratch = [
        per_chunk((2 * L2, V7X_LANES), BF16),
        per_chunk((L2, L2), BF16),
        per_chunk((L2, V7X_LANES), F32),
        per_chunk((L2, 2 * L2), BF16),
        per_chunk((2 * L2, V7X_LANES), BF16),
        per_chunk((L2, V7X_LANES), BF16),
        per_chunk((V7X_SUBLANES, V7X_LANES), F32),
        per_chunk((L2, V7X_LANES), F32),
        per_chunk((WKV_CHUNK, V7X_LANES), F32)]
    scratch = ([pltpu.VMEM((n_pairs, V7X_LANES, V7X_LANES), F32)]
               + [pltpu.VMEM(shape, dt) for shape, dt in per_chunk_scratch])
    scratch_bytes = sum(_nbytes(shape, dt) for shape, dt in per_chunk_scratch)
    pipelined = 6 * _nbytes((t_block, wcol), F32) + _nbytes((n_pairs, 128, 128), F32)
    resident = scratch_bytes + _nbytes((n_pairs, 128, 128), F32) + 24 * _nbytes((256, 256), F32)
    return pl.pallas_call(
        functools.partial(_wkv_chunk_kernel, n_chunks=n_chunks, n_pairs=n_pairs),
        grid=(B, n_col, T // t_block),
        in_specs=[tile] * 5 + [vec] * 5,
        out_specs=[tile, pl.BlockSpec((1, 2 * n_pairs, RWKV_HEAD, RWKV_HEAD), lambda b, p, t: (b, p, 0, 0))],
        out_shape=[jax.ShapeDtypeStruct((B, T, D), BF16),
                   jax.ShapeDtypeStruct((B, D // RWKV_HEAD, RWKV_HEAD, RWKV_HEAD), F32)],
        scratch_shapes=scratch,
        compiler_params=_params(("parallel", "parallel", "arbitrary"), pipelined, resident),
        name="wkv_chunked",
    )(r, lw, k, v, a, k_k, k_a, r_k, gn_w, gn_b)


def _wkv_step_kernel(s_ref, r_ref, lw_ref, k_ref, v_ref, a_ref, kk_ref, ka_ref, rk_ref, gw_ref, gb_ref,
                     snew_ref, y_ref):
    S = s_ref[0]
    r, lw, k, v, a = (t[...].astype(F32) for t in (r_ref, lw_ref, k_ref, v_ref, a_ref))
    kkr = k * kk_ref[...]
    nrm = jnp.sqrt(jnp.sum(kkr * kkr, axis=0, keepdims=True))
    kk = kkr / jnp.maximum(nrm, 1e-12)
    b = kk * a
    km = k * (1.0 + (a - 1.0) * ka_ref[...])
    w = jnp.exp(lw)
    sa = -jnp.sum(S * kk[None], axis=1)
    s_new = S * w[None] + sa[:, None, :] * b[None] + v[:, None, :] * km[None]
    snew_ref[0] = s_new
    y = jnp.sum(s_new * r[None], axis=1)
    mean = jnp.mean(y, axis=0, keepdims=True)
    cen = y - mean
    var = jnp.mean(cen * cen, axis=0, keepdims=True)
    yn = cen * lax.rsqrt(var + GN_EPS) * gw_ref[...] + gb_ref[...]
    bonus = jnp.sum(r * km * rk_ref[...], axis=0, keepdims=True) * v
    y_ref[...] = yn + bonus


def _wkv_step_call(s0, r, lw, k, v, a, k_k, k_a, r_k, gn_w, gn_b):
    H, K, _, B = s0.shape
    st = pl.BlockSpec((1, K, K, B), lambda h: (h, 0, 0, 0))
    vec = pl.BlockSpec((K, B), lambda h: (h, 0))
    pipelined = 2 * _nbytes((K, K, B), F32) + 11 * _nbytes((K, B), F32)
    resident = 6 * _nbytes((K, K, B), F32)
    return pl.pallas_call(
        _wkv_step_kernel,
        grid=(H,),
        in_specs=[st] + [vec] * 10,
        out_specs=[st, vec],
        out_shape=[jax.ShapeDtypeStruct(s0.shape, F32), jax.ShapeDtypeStruct((H * K, B), F32)],
        compiler_params=_params(("parallel",), pipelined, resident),
        name="wkv_step",
    )(s0, r, lw, k, v, a, k_k, k_a, r_k, gn_w, gn_b)


def _mla_proj_kernel(y_ref, g0_ref, x0_ref, gate0_ref, post0_ref, wout0_ref,
                     shift_ref, scale_ref, kshift_ref, kscale_ref, gain_ref, kgain_ref,
                     cos_ref, sin_ref, wqa_ref, wgate_ref, wkv_ref, qnorm_ref, kvnorm_ref,
                     wqn_ref, wqp_ref, wqs_ref, wuk_ref,
                     x_ref, q_ref, kcat_ref, ckv_ref, kpe_ref, sg_ref):
    g0 = g0_ref[0].astype(F32)
    z = _dot(y_ref[0].astype(F32) * (g0 * _sigmoid(g0)), wout0_ref[...])
    x = x0_ref[0] + gate0_ref[0] * _rmsnorm(z, post0_ref[...])
    x_ref[0] = x
    ms = jnp.mean(x * x, axis=-1, keepdims=True)
    xn = x * lax.rsqrt(ms + EPS)
    h = xn * gain_ref[...] * (1.0 + scale_ref[0]) + shift_ref[0]
    hk = xn * kgain_ref[...] * (1.0 + kscale_ref[0]) + kshift_ref[0]
    cos = cos_ref[...]
    sin = sin_ref[...]

    kv = _dot(hk, wkv_ref[...])
    ckv = _rmsnorm(kv[:, :KV_LORA], kvnorm_ref[...])
    kpe = kv[:, KV_LORA:KV_LORA + 128] * cos + kv[:, KV_LORA + 128:] * sin
    ckv_ref[0] = ckv
    kpe_ref[0] = kpe[:, :QK_ROPE]
    kcat_ref[0, :, :KV_LORA] = ckv.astype(BF16)
    kcat_ref[0, :, KV_LORA:] = kpe.astype(BF16)

    g = _dot(h, wgate_ref[...])
    sg_ref[0] = (g * _sigmoid(g)).astype(sg_ref.dtype)

    qn = _rmsnorm(_dot(h, wqa_ref[...]), qnorm_ref[...]).astype(BF16)
    q_nope = _dot(qn, wqn_ref[...])
    q_pe = _dot(qn, wqp_ref[...])
    q_ps = _dot(qn, wqs_ref[...])
    for hd in range(MLA_H):
        cols = slice(hd * 128, (hd + 1) * 128)
        q_lat = _dot(q_nope[:, cols], wuk_ref[hd])
        q_ref[0, hd, :, :KV_LORA] = (q_lat * Q_SCALE).astype(BF16)
        q_ref[0, hd, :, KV_LORA:] = ((q_pe[:, cols] * cos + q_ps[:, cols] * sin) * Q_SCALE).astype(BF16)


def _mla_proj_call(y, g0, x0, mod0, post0, wout0, mod, kvmod, gain, kgain, cos, sin,
                   wqa, wgate, wkv, qnorm, kvnorm, wqn, wqp, wqs, wuk, *, per_row, tm):
    B, T, D = x0.shape
    tile = lambda w: pl.BlockSpec((1, tm, w), lambda b, t: (b, t, 0))
    full = _const_spec
    tab = pl.BlockSpec((tm, 128), lambda b, t: (t, 0))
    weights = (wout0, wqa, wgate, wkv, wqn, wqp, wqs, wuk)
    pipelined = (4 * _nbytes((tm, D), F32) + _nbytes((MLA_H, tm, KCAT), BF16)
                 + 2 * _nbytes((tm, KCAT), F32))
    weight_bytes = sum(_nbytes(w.shape, BF16) for w in weights)
    return pl.pallas_call(
        _mla_proj_kernel,
        grid=(B, T // tm),
        in_specs=[tile(D), tile(D), tile(D), _mod_spec(per_row, tm, D, 2), full((1, D)), full(wout0.shape),
                  _mod_spec(per_row, tm, D, 0), _mod_spec(per_row, tm, D, 1),
                  _mod_spec(per_row, tm, D, 0), _mod_spec(per_row, tm, D, 1),
                  full((1, D)), full((1, D)), tab, tab,
                  full(wqa.shape), full(wgate.shape), full(wkv.shape), full((1, Q_LORA)), full((1, KV_LORA)),
                  full(wqn.shape), full(wqp.shape), full(wqs.shape), full(wuk.shape)],
        out_specs=[tile(D), pl.BlockSpec((1, MLA_H, tm, KCAT), lambda b, t: (b, 0, t, 0)),
                   tile(KCAT), tile(KV_LORA), tile(QK_ROPE), tile(MLA_H * V_HEAD)],
        out_shape=[jax.ShapeDtypeStruct((B, T, D), F32),
                   jax.ShapeDtypeStruct((B, MLA_H, T, KCAT), BF16),
                   jax.ShapeDtypeStruct((B, T, KCAT), BF16),
                   jax.ShapeDtypeStruct((B, T, KV_LORA), F32),
                   jax.ShapeDtypeStruct((B, T, QK_ROPE), F32),
                   jax.ShapeDtypeStruct((B, T, MLA_H * V_HEAD), BF16)],
        compiler_params=_params(("parallel", "parallel"), pipelined,
                                12 * _nbytes((tm, D), F32) + weight_bytes),
        name="mla_proj",
    )(y, g0, x0, mod0, post0, wout0, mod, mod, kvmod, kvmod, gain, kgain, cos, sin,
      wqa, wgate, wkv, qnorm, kvnorm, wqn, wqp, wqs, wuk)


def _lane_tile(t, width):
    return jnp.concatenate([t] * (width // V7X_LANES), axis=1)


def _flash_kernel(q_ref, k_ref, o_ref, m_scr, l_scr, acc_scr, *, tq):
    qi = pl.program_id(1)
    m_scr[...] = jnp.full_like(m_scr, -jnp.inf)
    l_scr[...] = jnp.zeros_like(l_scr)
    acc_scr[...] = jnp.zeros_like(acc_scr)

    def update(ki, masked):
        kc = k_ref[0, pl.ds(pl.multiple_of(ki * tq, tq), tq), :]
        vc = kc[:, :KV_LORA]
        scores = lambda hd: lax.dot_general(q_ref[0, hd], kc, _NT, preferred_element_type=F32)
        if masked:
            causal = (lax.broadcasted_iota(jnp.int32, (tq, tq), 1)
                      <= lax.broadcasted_iota(jnp.int32, (tq, tq), 0))
        groups = [range(g, g + FLASH_HEAD_GROUP) for g in range(0, MLA_H, FLASH_HEAD_GROUP)]
        s_next = [scores(hd) for hd in groups[0]]
        for gi, heads in enumerate(groups):
            s = s_next
            if gi + 1 < len(groups):
                s_next = [scores(hd) for hd in groups[gi + 1]]
            if masked:
                s = [jnp.where(causal, t, -jnp.inf) for t in s]
            m_prev = [m_scr[hd] for hd in heads]
            m_new = [jnp.maximum(mp, jnp.max(t, axis=-1, keepdims=True)) for mp, t in zip(m_prev, s)]
            alpha = [jnp.exp2(mp - mn) for mp, mn in zip(m_prev, m_new)]
            p = [jnp.exp2(t - _lane_tile(mn, tq)) for t, mn in zip(s, m_new)]
            pv = [jnp.dot(t.astype(BF16), vc, preferred_element_type=F32) for t in p]
            for i, hd in enumerate(heads):
                l_scr[hd] = alpha[i] * l_scr[hd] + jnp.sum(p[i], axis=-1, keepdims=True)
                acc_scr[hd] = _lane_tile(alpha[i], KV_LORA) * acc_scr[hd] + pv[i]
                m_scr[hd] = m_new[i]

    def below_diagonal(ki, carry):
        update(ki, False)
        return carry

    lax.fori_loop(0, qi, below_diagonal, 0)
    update(qi, True)
    for hd in range(MLA_H):
        o_ref[0, hd] = (acc_scr[hd] * _lane_tile(1.0 / l_scr[hd], KV_LORA)).astype(BF16)


def _flash_call(q, kcat, *, tq):
    B, H, T, _ = q.shape
    pipelined = (_nbytes((H, tq, KCAT), BF16) + _nbytes((T, KCAT), BF16)
                 + _nbytes((H, tq, KV_LORA), BF16))
    resident = _nbytes((H, tq, KV_LORA + 2 * V7X_LANES), F32) + 8 * _nbytes((tq, tq), F32)
    return pl.pallas_call(
        functools.partial(_flash_kernel, tq=tq),
        grid=(B, T // tq),
        in_specs=[pl.BlockSpec((1, H, tq, KCAT), lambda b, i: (b, 0, i, 0)),
                  pl.BlockSpec((1, T, KCAT), lambda b, i: (b, 0, 0))],
        out_specs=pl.BlockSpec((1, H, tq, KV_LORA), lambda b, i: (b, 0, i, 0)),
        out_shape=jax.ShapeDtypeStruct((B, H, T, KV_LORA), BF16),
        scratch_shapes=[pltpu.VMEM((H, tq, V7X_LANES), F32), pltpu.VMEM((H, tq, V7X_LANES), F32),
                        pltpu.VMEM((H, tq, KV_LORA), F32)],
        compiler_params=_params(("parallel", "parallel"), pipelined, resident),
        name="mla_flash",
    )(q, kcat)


def _decode_kernel(pt_ref, q_ref, cnew_ref, pnew_ref, ckv_hbm, kpe_hbm, o_ref,
                   ck_buf, kp_buf, sems, keys_buf, rope_buf):
    b = pl.program_id(0)
    n_seq = pl.num_programs(0)
    n_pages, ps = ck_buf.shape[1], ck_buf.shape[2]
    slot = lax.rem(b, 2)

    def page_copies(seq, slot_, i):
        page = pt_ref[seq, i]
        return (pltpu.make_async_copy(ckv_hbm.at[page], ck_buf.at[slot_, i], sems.at[slot_]),
                pltpu.make_async_copy(kpe_hbm.at[page], kp_buf.at[slot_, i], sems.at[slot_]))

    def start_gather(seq, slot_):
        for i in range(n_pages):
            for cp in page_copies(seq, slot_, i):
                cp.start()

    @pl.when(b == 0)
    def _():
        start_gather(0, 0)

    @pl.when(b + 1 < n_seq)
    def _():
        start_gather(b + 1, 1 - slot)

    for i in range(n_pages):
        for cp in page_copies(b, slot, i):
            cp.wait()

    for i in range(n_pages):
        keys_buf[i * ps:(i + 1) * ps, :] = ck_buf[slot, i].astype(BF16)
        rope_buf[:, i * ps:(i + 1) * ps] = kp_buf[slot, i].astype(BF16)
    q = q_ref[0]
    q_lat = q[:, :KV_LORA]
    q_pe = q[:, KV_LORA:KV_LORA + QK_ROPE]
    keys = keys_buf[...]
    s = (lax.dot_general(q_lat, keys, _NT, preferred_element_type=F32)
         + jnp.dot(q_pe, rope_buf[...], preferred_element_type=F32))
    cn = cnew_ref[0]
    s_new = (jnp.sum(q_lat.astype(F32) * cn, axis=-1, keepdims=True)
             + jnp.sum(q_pe.astype(F32) * pnew_ref[0], axis=-1, keepdims=True))
    m = jnp.maximum(jnp.max(s, axis=-1, keepdims=True), s_new)
    p = jnp.exp2(s - m)
    p_new = jnp.exp2(s_new - m)
    denom = jnp.sum(p, axis=-1, keepdims=True) + p_new
    acc = jnp.dot(p.astype(BF16), keys, preferred_element_type=F32) + p_new * cn
    o_ref[0] = acc / denom


def _decode_call(page_table, q, c_new, p_new, cache_ckv, cache_kpe):
    B, H, _ = q.shape
    n_pages = page_table.shape[1]
    ps = cache_ckv.shape[1]
    past = n_pages * ps
    scratch = [((2, n_pages, ps, KV_LORA), F32),
               ((2, n_pages, QK_ROPE, ps), F32),
               ((past, KV_LORA), BF16),
               ((QK_ROPE, past), BF16)]
    scratch_bytes = sum(_nbytes(shape, dt) for shape, dt in scratch)
    (ck, kp, kb, rb) = [pltpu.VMEM(shape, dt) for shape, dt in scratch]
    return pl.pallas_call(
        _decode_kernel,
        grid_spec=pltpu.PrefetchScalarGridSpec(
            num_scalar_prefetch=1,
            grid=(B,),
            in_specs=[pl.BlockSpec((1, H, KCAT), lambda b, pt: (b, 0, 0)),
                      pl.BlockSpec((1, 1, KV_LORA), lambda b, pt: (b, 0, 0)),
                      pl.BlockSpec((1, 1, QK_ROPE), lambda b, pt: (b, 0, 0)),
                      pl.BlockSpec(memory_space=pl.ANY),
                      pl.BlockSpec(memory_space=pl.ANY)],
            out_specs=pl.BlockSpec((1, H, KV_LORA), lambda b, pt: (b, 0, 0)),
            scratch_shapes=[ck, kp, pltpu.SemaphoreType.DMA((2,)), kb, rb]),
        out_shape=jax.ShapeDtypeStruct((B, H, KV_LORA), F32),
        compiler_params=_params(("arbitrary",), _nbytes((H, KCAT), F32),
                                scratch_bytes + 4 * _nbytes((H, past), F32)),
        name="mla_decode",
    )(page_table, q, c_new, p_new, cache_ckv, cache_kpe)


def _mla_out_kernel(o_ref, sg_ref, x_ref, gate_ref, gain_ref, wuv_ref, wout_ref, y_ref, og_scr):
    for hd in range(MLA_H):
        cols = slice(hd * V_HEAD, (hd + 1) * V_HEAD)
        o = _dot(o_ref[0, hd], wuv_ref[hd])
        og_scr[:, cols] = (o * sg_ref[0, :, cols].astype(F32)).astype(BF16)
    z = _dot(og_scr[...], wout_ref[...])
    y_ref[0] = x_ref[0] + gate_ref[0] * _rmsnorm(z, gain_ref[...])


def _mla_out_call(o_lat, sg, x, mod, gain, wuv, wout, *, per_row, tm):
    B, T, D = x.shape
    tile = lambda w: pl.BlockSpec((1, tm, w), lambda b, t: (b, t, 0))
    full = _const_spec
    pipelined = _nbytes((MLA_H, tm, KV_LORA), BF16) + 3 * _nbytes((tm, D), F32)
    weight_bytes = _nbytes(wuv.shape, BF16) + _nbytes(wout.shape, BF16)
    return pl.pallas_call(
        _mla_out_kernel,
        grid=(B, T // tm),
        in_specs=[pl.BlockSpec((1, MLA_H, tm, KV_LORA), lambda b, t: (b, 0, t, 0)),
                  tile(MLA_H * V_HEAD), tile(D), _mod_spec(per_row, tm, D, 2), full((1, D)),
                  full(wuv.shape), full(wout.shape)],
        out_specs=tile(D),
        out_shape=jax.ShapeDtypeStruct((B, T, D), F32),
        scratch_shapes=[pltpu.VMEM((tm, MLA_H * V_HEAD), BF16)],
        compiler_params=_params(("parallel", "parallel"), pipelined,
                                4 * _nbytes((tm, D), F32) + weight_bytes),
        name="mla_out",
    )(o_lat, sg, x, mod, gain, wuv, wout)


def _rope_tables(pos):
    half = QK_ROPE // 2
    inv = ROPE_THETA ** (-jnp.arange(half, dtype=F32) / half)
    ang = pos.astype(F32)[:, None] * inv[None, :]
    c, s = jnp.cos(ang), jnp.sin(ang)
    z = jnp.zeros((pos.shape[0], 128 - QK_ROPE), F32)
    return jnp.concatenate([c, c, z], axis=1), jnp.concatenate([-s, s, z], axis=1)


def _swap_halves(w):
    half = w.shape[-1] // 2
    return jnp.concatenate([w[..., half:], w[..., :half]], axis=-1)


def _pad_lanes(w):
    return jnp.concatenate([w, jnp.zeros(w.shape[:-1] + (128 - w.shape[-1],), w.dtype)], axis=-1)


def kernel(x_prompt, x_sample, c_prompt, c_sample, state_wkv, state_shift, cache_kv_latent, cache_k_rope, page_table, ada_w, ada_b, norm_pre, norm_post, a_mu, a_w_in, a_w0, a_w1, a_w2, a_a0, a_a1, a_a2, a_k_k, a_k_a, a_r_k, a_gn_w, a_gn_b, a_w_out, kv_ada_w, kv_ada_b, kv_norm, kv_w_a, kv_a_norm, kv_w_b, b_w_in, b_q_norm, b_w_q, b_w_out):
    B, T, D = x_prompt.shape
    DB = x_sample.shape[0]
    H = D // RWKV_HEAD
    assert ada_w.shape[0] == 2 and a_mu.shape[0] == 1 and b_w_in.shape[0] == 1
    assert x_sample.shape[1] == 1 and T % ROW_TILE == 0 and DB % V7X_SUBLANES == 0

    c_all = jnp.concatenate([c_prompt, c_sample], axis=0)
    mods = _ada_call(c_all, ada_w, ada_b)
    kvmods = _ada_call(c_all, kv_ada_w[None], kv_ada_b[None])
    mod_p = [mods[i, :B].reshape(B, 1, 3 * D) for i in range(2)]
    mod_s = [mods[i, B:].reshape(1, DB, 3 * D) for i in range(2)]
    kvmod_p = kvmods[0, :B].reshape(B, 1, 2 * D)
    kvmod_s = kvmods[0, B:].reshape(1, DB, 2 * D)

    row = lambda v: v.reshape(1, -1)
    w_in = a_w_in[0].astype(BF16)
    a_args = (row(norm_pre[0]), a_mu[0], w_in, row(a_w0[0]), a_w1[0].astype(BF16), a_w2[0].astype(BF16),
              row(a_a0[0]), a_a1[0].astype(BF16), a_a2[0].astype(BF16))
    w_out_a = a_w_out[0].astype(BF16)
    k_k, k_a, r_k = row(a_k_k[0]), row(a_k_a[0]), row(a_r_k[0])
    gn_w, gn_b = row(a_gn_w[0]), row(a_gn_b[0])

    w_bin = b_w_in[0]
    wqa = w_bin[:, :Q_LORA].astype(BF16)
    wgate = w_bin[:, Q_LORA:].astype(BF16)
    kv_pe = kv_w_a[:, KV_LORA:]
    wkv = jnp.concatenate([kv_w_a[:, :KV_LORA], _pad_lanes(kv_pe), _pad_lanes(_swap_halves(kv_pe))],
                          axis=1).astype(BF16)
    w_q = b_w_q[0]
    wqn = w_q[:, :, :QK_NOPE].reshape(Q_LORA, MLA_H * QK_NOPE).astype(BF16)
    wq_pe = w_q[:, :, QK_NOPE:]
    wqp = _pad_lanes(wq_pe).reshape(Q_LORA, MLA_H * 128).astype(BF16)
    wqs = _pad_lanes(_swap_halves(wq_pe)).reshape(Q_LORA, MLA_H * 128).astype(BF16)
    wuk = jnp.transpose(kv_w_b[:, :, :QK_NOPE], (1, 2, 0)).astype(BF16)
    wuv = jnp.transpose(kv_w_b[:, :, QK_NOPE:], (1, 0, 2)).astype(BF16)
    wout_b = b_w_out[0].astype(BF16)
    b_args = (wqa, wgate, wkv, row(b_q_norm[0]), row(kv_a_norm), wqn, wqp, wqs, wuk)

    tm = ROW_TILE
    r, lw, k, v, a, g, h_last = _rwkv_proj_call(x_prompt, x_prompt, mod_p[0], *a_args,
                                                seq_shift=True, tm=tm)
    yw, wkv_p = _wkv_chunk_call(r, lw, k, v, a, k_k, k_a, r_k, gn_w, gn_b,
                                t_block=min(T, 512), n_pairs=8)
    wkv_p = wkv_p[None]
    shift_p = h_last.reshape(1, B, D)

    cos_p, sin_p = _rope_tables(jnp.arange(T, dtype=jnp.int32))
    x1, q_p, kcat_p, ckv_p, kpe_p, sg_p = _mla_proj_call(
        yw, g, x_prompt, mod_p[0], row(norm_post[0]), w_out_a,
        mod_p[1], kvmod_p, row(norm_pre[1]), row(kv_norm), cos_p, sin_p, *b_args,
        per_row=False, tm=tm)
    o_p = _flash_call(q_p, kcat_p, tq=min(T, 256))
    y_prompt = _mla_out_call(o_p, sg_p, x1, mod_p[1], row(norm_post[1]), wuv, wout_b,
                             per_row=False, tm=tm)

    xs = x_sample.reshape(1, DB, D)
    rs, lws, ks, vs, as_, gs, hs = _rwkv_proj_call(xs, state_shift[0].reshape(1, DB, D), mod_s[0],
                                                   *a_args, seq_shift=False, tm=DB)
    bm = lambda t: jnp.swapaxes(t.reshape(DB, D), 0, 1)
    pb = lambda t: jnp.broadcast_to(t.reshape(D, 1), (D, DB))
    s_new, yws = _wkv_step_call(jnp.transpose(state_wkv[0], (1, 2, 3, 0)),
                                bm(rs), bm(lws), bm(ks), bm(vs), bm(as_),
                                pb(k_k), pb(k_a), pb(r_k), pb(gn_w), pb(gn_b))
    s_new = jnp.transpose(s_new, (3, 0, 1, 2))
    n_pages = page_table.shape[1]
    past_len = n_pages * cache_kv_latent.shape[1]
    cos_s, sin_s = _rope_tables(jnp.full((DB,), past_len, dtype=jnp.int32))
    x1s, q_s, _, ckv_s, kpe_s, sg_s = _mla_proj_call(
        jnp.swapaxes(yws, 0, 1).reshape(1, DB, D), gs, xs, mod_s[0], row(norm_post[0]), w_out_a,
        mod_s[1], kvmod_s, row(norm_pre[1]), row(kv_norm), cos_s, sin_s, *b_args,
        per_row=True, tm=DB)
    o_s = _decode_call(page_table, jnp.transpose(q_s[0], (1, 0, 2)),
                       ckv_s.reshape(DB, 1, KV_LORA), kpe_s.reshape(DB, 1, QK_ROPE),
                       cache_kv_latent, jnp.swapaxes(cache_k_rope, 1, 2))
    o_s = jnp.transpose(o_s, (1, 0, 2)).astype(BF16)[None]
    y_s = _mla_out_call(o_s, sg_s, x1s, mod_s[1], row(norm_post[1]), wuv, wout_b,
                        per_row=True, tm=DB)

    return (y_prompt, y_s.reshape(DB, 1, D), wkv_p, shift_p, ckv_p, kpe_p,
            s_new[None], hs.reshape(1, DB, D), ckv_s.reshape(DB, 1, KV_LORA),
            kpe_s.reshape(DB, 1, QK_ROPE))
```

```python
import functools
import math

import jax
import jax.numpy as jnp
from jax import lax
from jax.experimental import pallas as pl
from jax.experimental.pallas import tpu as pltpu

F32 = jnp.float32
BF16 = jnp.bfloat16
HIGHEST = lax.Precision.HIGHEST

RWKV_HEAD = 64
GN_EPS = 64e-5
EPS = 1e-6
MLA_H = 8
QK_NOPE = 128
QK_ROPE = 64
V_HEAD = 128
Q_LORA = 384
KV_LORA = 256
ROPE_THETA = 10000.0
ATTN_SCALE = (QK_NOPE + QK_ROPE) ** -0.5
Q_SCALE = ATTN_SCALE * math.log2(math.e)
KCAT = KV_LORA + 128

V7X_LANES = 128
V7X_SUBLANES = 8
V7X_VMEM_BYTES = 64 * 1024 * 1024
V7X_VMEM_REQUEST_CAP = V7X_VMEM_BYTES - 8 * 1024 * 1024

WKV_CHUNK = 64
ROW_TILE = 512
FLASH_HEAD_GROUP = 2


def _vmem_limit(pipelined_bytes, resident_bytes=0):
    est = 2 * pipelined_bytes + resident_bytes + 4 * 1024 * 1024
    return int(min(max(est, 16 * 1024 * 1024), V7X_VMEM_REQUEST_CAP))


def _nbytes(shape, dtype):
    return math.prod(shape) * jnp.dtype(dtype).itemsize


def _params(sem, pipelined_bytes, resident_bytes=0):
    return pltpu.CompilerParams(
        dimension_semantics=sem,
        vmem_limit_bytes=_vmem_limit(pipelined_bytes, resident_bytes))


def _dot(a, b):
    return jnp.dot(a.astype(BF16), b.astype(BF16), preferred_element_type=F32)


def _dot_hi(a, b, dims=(((1,), (0,)), ((), ()))):
    return lax.dot_general(a, b, dims, precision=HIGHEST, preferred_element_type=F32)


_NT = (((1,), (1,)), ((), ()))
_TN = (((0,), (0,)), ((), ()))


def _sigmoid(x):
    return 1.0 / (1.0 + jnp.exp(-x))


def _ada_kernel(c_ref, w_ref, b_ref, o_ref):
    o_ref[0] = _dot_hi(c_ref[...], w_ref[0]) + b_ref[0]


def _ada_call(c, w, b):
    G, D, N = w.shape
    M = c.shape[0]
    tn = 1024
    pipelined = _nbytes((D, tn), F32) + _nbytes((M, tn), F32) + _nbytes((M, D), F32)
    return pl.pallas_call(
        _ada_kernel,
        grid=(G, N // tn),
        in_specs=[
            pl.BlockSpec((M, D), lambda g, j: (0, 0)),
            pl.BlockSpec((1, D, tn), lambda g, j: (g, 0, j)),
            pl.BlockSpec((1, 1, tn), lambda g, j: (g, 0, j)),
        ],
        out_specs=pl.BlockSpec((1, M, tn), lambda g, j: (g, 0, j)),
        out_shape=jax.ShapeDtypeStruct((G, M, N), F32),
        compiler_params=_params(("parallel", "parallel"), pipelined),
        name="ada_modulation",
    )(c, w, b.reshape(G, 1, N))


def _const_spec(shape):
    return pl.BlockSpec(shape, lambda *_: (0,) * len(shape), pipeline_mode=pl.Buffered(1))


def _mod_spec(per_row, tm, D, col):
    if per_row:
        return pl.BlockSpec((1, tm, D), lambda b, t: (b, t, col))
    return pl.BlockSpec((1, 1, D), lambda b, t: (b, 0, col))


def _modnorm(x, gain, scale, shift):
    ms = jnp.mean(x * x, axis=-1, keepdims=True)
    return x * lax.rsqrt(ms + EPS) * gain * (1.0 + scale) + shift


def _rmsnorm(x, gain):
    ms = jnp.mean(x * x, axis=-1, keepdims=True)
    return x * lax.rsqrt(ms + EPS) * gain


def _rwkv_proj_kernel(x_ref, prev_ref, shift_ref, scale_ref, gain_ref, mu_ref, win_ref,
                      w0_ref, w1_ref, w2_ref, a0_ref, a1_ref, a2_ref,
                      r_ref, lw_ref, k_ref, v_ref, a_ref, g_ref, hlast_ref, *, seq_shift):
    x = x_ref[0]
    gain = gain_ref[...]
    scale = scale_ref[0]
    shift = shift_ref[0]
    h = _modnorm(x, gain, scale, shift)
    tm = h.shape[0]
    if seq_shift:
        hp = _modnorm(prev_ref[0][V7X_SUBLANES - 1:V7X_SUBLANES, :], gain, scale, shift)
        hp = jnp.where(pl.program_id(1) == 0, 0.0, hp)
        row = lax.broadcasted_iota(jnp.int32, (tm, 1), 0)
        hs = jnp.where(row == 0, hp, pltpu.roll(h, 1, axis=0))
        hlast_ref[0] = h[tm - 1:tm, :]
    else:
        hs = prev_ref[0]
        hlast_ref[0] = h
    xx = hs - h
    mu = mu_ref[...]
    outs = (r_ref, k_ref, v_ref, g_ref)
    for m in range(4):
        xm = h + xx * mu[m:m + 1, :]
        outs[m][0] = _dot(xm, win_ref[m]).astype(outs[m].dtype)
    xw = h + xx * mu[4:5, :]
    xa = h + xx * mu[5:6, :]
    wl = w0_ref[...] + _dot(jnp.tanh(_dot(xw, w1_ref[...])), w2_ref[...])
    z = -wl
    softplus = jnp.maximum(z, 0.0) + jnp.log(1.0 + jnp.exp(-jnp.abs(z)))
    lw_ref[0] = -jnp.exp(-softplus - 0.5)
    al = a0_ref[...] + _dot(_dot(xa, a1_ref[...]), a2_ref[...])
    a_ref[0] = _sigmoid(al).astype(a_ref.dtype)


def _rwkv_proj_call(x, prev, mod, gain, mu, w_in, w0, w1, w2, a0, a1, a2, *, seq_shift, tm):
    B, T, D = x.shape
    per_row = not seq_shift
    nt = T // tm
    tile = pl.BlockSpec((1, tm, D), lambda b, t: (b, t, 0))
    if seq_shift:
        sub = tm // V7X_SUBLANES
        prev_spec = pl.BlockSpec((1, V7X_SUBLANES, D),
                                 lambda b, t: (b, jnp.maximum(t * sub - 1, 0), 0))
        hlast_shape = jax.ShapeDtypeStruct((B, 1, D), F32)
        hlast_spec = pl.BlockSpec((1, 1, D), lambda b, t: (b, 0, 0))
        sem = ("parallel", "arbitrary")
    else:
        prev_spec = tile
        hlast_shape = jax.ShapeDtypeStruct((B, T, D), F32)
        hlast_spec = tile
        sem = ("parallel", "parallel")
    full = _const_spec
    lora = w1.shape[1]
    pipelined = 5 * _nbytes((tm, D), F32)
    resident = (8 * _nbytes((tm, D), F32) + _nbytes((4, D, D), BF16)
                + 4 * _nbytes((D, V7X_LANES), BF16))
    out_sds = lambda dt: jax.ShapeDtypeStruct((B, T, D), dt)
    return pl.pallas_call(
        functools.partial(_rwkv_proj_kernel, seq_shift=seq_shift),
        grid=(B, nt),
        in_specs=[tile, prev_spec, _mod_spec(per_row, tm, D, 0), _mod_spec(per_row, tm, D, 1),
                  full((1, D)), full((6, D)), full((4, D, D)),
                  full((1, D)), full((D, lora)), full((lora, D)),
                  full((1, D)), full((D, lora)), full((lora, D))],
        out_specs=[tile] * 6 + [hlast_spec],
        out_shape=[out_sds(BF16), out_sds(F32)] + [out_sds(BF16)] * 4 + [hlast_shape],
        compiler_params=_params(sem, pipelined, resident),
        name="rwkv_proj",
    )(x, prev, mod, mod, gain, mu, w_in, w0, w1, w2, a0, a1, a2)


def _cumsum_rows(cum, x):
    hi = x.astype(BF16)
    rest = x - hi.astype(F32)
    mid = rest.astype(BF16)
    lo = (rest - mid.astype(F32)).astype(BF16)
    dot = lambda t: jnp.dot(cum, t, preferred_element_type=F32)
    return dot(hi) + dot(mid) + dot(lo)


def _wkv_chunk_kernel(r_ref, lw_ref, k_ref, v_ref, a_ref, kk_ref, ka_ref, rk_ref, gw_ref, gb_ref,
                      y_ref, sfin_ref, s_scr, wr_scr, tinv_scr, av_scr, ar_scr, bkp_scr, v2_scr, pend_scr,
                      y2_scr, bias_scr, *, n_chunks, n_pairs):
    L = WKV_CHUNK
    L2 = 2 * L
    tc = pl.program_id(2)

    @pl.when(tc == 0)
    def _():
        s_scr[...] = jnp.zeros_like(s_scr)

    lane = lax.broadcasted_iota(jnp.int32, (L2, V7X_LANES), 1)
    srow = lax.broadcasted_iota(jnp.int32, (L2, V7X_LANES), 0)
    own = (srow < L) == (lane < RWKV_HEAD)
    first_head = lax.broadcasted_iota(jnp.int32, (L, V7X_LANES), 1) < RWKV_HEAD
    ti = lax.broadcasted_iota(jnp.int32, (L, L), 0)
    tj = lax.broadcasted_iota(jnp.int32, (L, L), 1)
    cum = (ti >= tj).astype(BF16)
    ri = lax.broadcasted_iota(jnp.int32, (L2, L2), 0)
    rj = lax.broadcasted_iota(jnp.int32, (L2, L2), 1)
    strict = ri > rj
    incl = ri >= rj
    eye = (ri == rj).astype(F32)
    n_rounds = int(math.log2(L)) - 1

    def stack(x):
        return jnp.where(own, jnp.concatenate([x, x], axis=0), 0.0)

    def fold(x2):
        return x2[:L] + x2[L:]

    def precompute(c, carry):
        t0 = pl.multiple_of(c * L, L)
        pairs = range(n_pairs)
        cols = [slice(p * V7X_LANES, (p + 1) * V7X_LANES) for p in pairs]
        load = lambda ref: [ref[0, pl.ds(t0, L), cols[p]].astype(F32) for p in pairs]
        r, lw, k, v, a = load(r_ref), load(lw_ref), load(k_ref), load(v_ref), load(a_ref)

        def head_sum(t):
            s0 = jnp.sum(jnp.where(first_head, t, 0.0), axis=-1, keepdims=True)
            s1 = jnp.sum(t, axis=-1, keepdims=True) - s0
            return jnp.where(first_head, s0, s1)

        kkr = [k[p] * kk_ref[:, cols[p]] for p in pairs]
        nrm = [jnp.sqrt(head_sum(kkr[p] * kkr[p])) for p in pairs]
        kk = [kkr[p] / jnp.maximum(nrm[p], 1e-12) for p in pairs]
        b = [kk[p] * a[p] for p in pairs]
        km = [k[p] * (1.0 + (a[p] - 1.0) * ka_ref[:, cols[p]]) for p in pairs]
        v2 = [stack(v[p]) for p in pairs]

        cs = [_cumsum_rows(cum, lw[p]) for p in pairs]
        cs_end = [cs[p][L - 1:L, :] for p in pairs]
        wr = [jnp.concatenate([stack(kk[p] * jnp.exp(cs[p] - lw[p])), stack(r[p] * jnp.exp(cs[p]))],
                              axis=0).astype(BF16) for p in pairs]
        e_neg = [jnp.exp(-cs[p]) for p in pairs]
        bk = [jnp.concatenate([stack(b[p] * e_neg[p]), stack(km[p] * e_neg[p])], axis=0).astype(BF16)
              for p in pairs]
        aa = [lax.dot_general(wr[p], bk[p], _NT, preferred_element_type=F32) for p in pairs]

        m0 = [jnp.where(strict, -aa[p][:L2, :L2], 0.0) for p in pairs]
        m = m0
        x = [eye for p in pairs]
        for it in range(n_rounds):
            if it + 1 < n_rounds:
                mx = [_dot(m[p], jnp.concatenate([m[p], x[p]], axis=1)) for p in pairs]
                x = [x[p] + mx[p][:, L2:] for p in pairs]
                m = [mx[p][:, :L2] for p in pairs]
            else:
                x = [x[p] + _dot(m[p], x[p]) for p in pairs]
        res = [eye - x[p] + _dot(m0[p], x[p]) for p in pairs]
        x = [x[p] + _dot(x[p], res[p]) for p in pairs]
        av = [_dot(jnp.where(strict, aa[p][:L2, L2:], 0.0), v2[p]) for p in pairs]

        for p in pairs:
            e_end = jnp.exp(cs_end[p] - cs[p])
            wr_scr[c, p] = wr[p]
            tinv_scr[c, p] = x[p].astype(BF16)
            av_scr[c, p] = av[p]
            ar_scr[c, p] = jnp.where(jnp.concatenate([incl, incl], axis=1), aa[p][L2:, :], 0.0).astype(BF16)
            bkp_scr[c, p] = jnp.concatenate([stack(b[p] * e_end), stack(km[p] * e_end)],
                                            axis=0).astype(BF16)
            v2_scr[c, p] = v2[p].astype(BF16)
            pend_scr[c, p] = jnp.broadcast_to(jnp.exp(cs_end[p]), (V7X_SUBLANES, V7X_LANES))
            bonus = head_sum(r[p] * km[p] * rk_ref[:, cols[p]]) * v[p]
            bias_scr[c, p] = gb_ref[:, cols[p]] + bonus
        return carry

    def recur(c, carry):
        pairs = range(n_pairs)
        s0 = [s_scr[p] for p in pairs]
        g = [lax.dot_general(wr_scr[c, p], s0[p].astype(BF16), _NT, preferred_element_type=F32)
             for p in pairs]
        u = [_dot(tinv_scr[c, p], -(g[p][:L2] + av_scr[c, p])) for p in pairs]
        uv = [jnp.concatenate([u[p].astype(BF16), v2_scr[c, p]], axis=0) for p in pairs]
        for p in pairs:
            s_scr[p] = (s0[p] * pend_scr[c, p][0:1, :]
                        + lax.dot_general(uv[p], bkp_scr[c, p], _TN, preferred_element_type=F32))
        for p in pairs:
            y2_scr[c, p] = g[p][L2:] + jnp.dot(ar_scr[c, p], uv[p], preferred_element_type=F32)
        return carry

    def normalise(c, carry):
        t0 = c * L if isinstance(c, int) else pl.multiple_of(c * L, L)
        pairs = range(n_pairs)
        y2 = [y2_scr[c, p] for p in pairs]
        mean = [jnp.sum(y2[p], axis=-1, keepdims=True) * (1.0 / RWKV_HEAD) for p in pairs]
        cen = [jnp.where(own, y2[p] - mean[p], 0.0) for p in pairs]
        var = [jnp.sum(cen[p] * cen[p], axis=-1, keepdims=True) * (1.0 / RWKV_HEAD) for p in pairs]
        for p in pairs:
            cols = slice(p * V7X_LANES, (p + 1) * V7X_LANES)
            yn = fold(cen[p] * lax.rsqrt(var[p] + GN_EPS))
            y_ref[0, pl.ds(t0, L), cols] = (bias_scr[c, p] + yn * gw_ref[:, cols]).astype(y_ref.dtype)
        return carry

    def recur_and_normalise(c, carry):
        normalise(c - 1, carry)
        return recur(c, carry)

    lax.fori_loop(0, n_chunks, precompute, 0)
    recur(0, 0)
    lax.fori_loop(1, n_chunks, recur_and_normalise, 0)
    normalise(n_chunks - 1, 0)

    @pl.when(tc == pl.num_programs(2) - 1)
    def _():
        for p in range(n_pairs):
            s = s_scr[p]
            sfin_ref[0, 2 * p] = s[:RWKV_HEAD, :RWKV_HEAD]
            sfin_ref[0, 2 * p + 1] = s[RWKV_HEAD:, RWKV_HEAD:]


def _wkv_chunk_call(r, lw, k, v, a, k_k, k_a, r_k, gn_w, gn_b, *, t_block, n_pairs):
    B, T, D = r.shape
    L2 = 2 * WKV_CHUNK
    n_chunks = t_block // WKV_CHUNK
    wcol = n_pairs * V7X_LANES
    n_col = D // wcol
    tile = pl.BlockSpec((1, t_block, wcol), lambda b, p, t: (b, t, p))
    vec = pl.BlockSpec((1, wcol), lambda b, p, t: (0, p))
    per_chunk = lambda shape, dt: ((n_chunks, n_pairs) + shape, dt)
    per_chunk_scratch = [
        per_chunk((2 * L2, V7X_LANES), BF16),
        per_chunk((L2, L2), BF16),
        per_chunk((L2, V7X_LANES), F32),
        per_chunk((L2, 2 * L2), BF16),
        per_chunk((2 * L2, V7X_LANES), BF16),
        per_chunk((L2, V7X_LANES), BF16),
        per_chunk((V7X_SUBLANES, V7X_LANES), F32),
        per_chunk((L2, V7X_LANES), F32),
        per_chunk((WKV_CHUNK, V7X_LANES), F32)]
    scratch = ([pltpu.VMEM((n_pairs, V7X_LANES, V7X_LANES), F32)]
               + [pltpu.VMEM(shape, dt) for shape, dt in per_chunk_scratch])
    scratch_bytes = sum(_nbytes(shape, dt) for shape, dt in per_chunk_scratch)
    pipelined = 6 * _nbytes((t_block, wcol), F32) + _nbytes((n_pairs, 128, 128), F32)
    resident = scratch_bytes + _nbytes((n_pairs, 128, 128), F32) + 24 * _nbytes((256, 256), F32)
    return pl.pallas_call(
        functools.partial(_wkv_chunk_kernel, n_chunks=n_chunks, n_pairs=n_pairs),
        grid=(B, n_col, T // t_block),
        in_specs=[tile] * 5 + [vec] * 5,
        out_specs=[tile, pl.BlockSpec((1, 2 * n_pairs, RWKV_HEAD, RWKV_HEAD), lambda b, p, t: (b, p, 0, 0))],
        out_shape=[jax.ShapeDtypeStruct((B, T, D), BF16),
                   jax.ShapeDtypeStruct((B, D // RWKV_HEAD, RWKV_HEAD, RWKV_HEAD), F32)],
        scratch_shapes=scratch,
        compiler_params=_params(("parallel", "parallel", "arbitrary"), pipelined, resident),
        name="wkv_chunked",
    )(r, lw, k, v, a, k_k, k_a, r_k, gn_w, gn_b)


def _wkv_step_kernel(s_ref, r_ref, lw_ref, k_ref, v_ref, a_ref, kk_ref, ka_ref, rk_ref, gw_ref, gb_ref,
                     snew_ref, y_ref):
    S = s_ref[0]
    r, lw, k, v, a = (t[...].astype(F32) for t in (r_ref, lw_ref, k_ref, v_ref, a_ref))
    kkr = k * kk_ref[...]
    nrm = jnp.sqrt(jnp.sum(kkr * kkr, axis=0, keepdims=True))
    kk = kkr / jnp.maximum(nrm, 1e-12)
    b = kk * a
    km = k * (1.0 + (a - 1.0) * ka_ref[...])
    w = jnp.exp(lw)
    sa = -jnp.sum(S * kk[None], axis=1)
    s_new = S * w[None] + sa[:, None, :] * b[None] + v[:, None, :] * km[None]
    snew_ref[0] = s_new
    y = jnp.sum(s_new * r[None], axis=1)
    mean = jnp.mean(y, axis=0, keepdims=True)
    cen = y - mean
    var = jnp.mean(cen * cen, axis=0, keepdims=True)
    yn = cen * lax.rsqrt(var + GN_EPS) * gw_ref[...] + gb_ref[...]
    bonus = jnp.sum(r * km * rk_ref[...], axis=0, keepdims=True) * v
    y_ref[...] = yn + bonus


def _wkv_step_call(s0, r, lw, k, v, a, k_k, k_a, r_k, gn_w, gn_b):
    H, K, _, B = s0.shape
    st = pl.BlockSpec((1, K, K, B), lambda h: (h, 0, 0, 0))
    vec = pl.BlockSpec((K, B), lambda h: (h, 0))
    pipelined = 2 * _nbytes((K, K, B), F32) + 11 * _nbytes((K, B), F32)
    resident = 6 * _nbytes((K, K, B), F32)
    return pl.pallas_call(
        _wkv_step_kernel,
        grid=(H,),
        in_specs=[st] + [vec] * 10,
        out_specs=[st, vec],
        out_shape=[jax.ShapeDtypeStruct(s0.shape, F32), jax.ShapeDtypeStruct((H * K, B), F32)],
        compiler_params=_params(("parallel",), pipelined, resident),
        name="wkv_step",
    )(s0, r, lw, k, v, a, k_k, k_a, r_k, gn_w, gn_b)


def _mla_proj_kernel(y_ref, g0_ref, x0_ref, gate0_ref, post0_ref, wout0_ref,
                     shift_ref, scale_ref, kshift_ref, kscale_ref, gain_ref, kgain_ref,
                     cos_ref, sin_ref, wqa_ref, wgate_ref, wkv_ref, qnorm_ref, kvnorm_ref,
                     wqn_ref, wqp_ref, wqs_ref, wuk_ref,
                     x_ref, q_ref, kcat_ref, ckv_ref, kpe_ref, sg_ref):
    g0 = g0_ref[0].astype(F32)
    z = _dot(y_ref[0].astype(F32) * (g0 * _sigmoid(g0)), wout0_ref[...])
    x = x0_ref[0] + gate0_ref[0] * _rmsnorm(z, post0_ref[...])
    x_ref[0] = x
    ms = jnp.mean(x * x, axis=-1, keepdims=True)
    xn = x * lax.rsqrt(ms + EPS)
    h = xn * gain_ref[...] * (1.0 + scale_ref[0]) + shift_ref[0]
    hk = xn * kgain_ref[...] * (1.0 + kscale_ref[0]) + kshift_ref[0]
    cos = cos_ref[...]
    sin = sin_ref[...]

    kv = _dot(hk, wkv_ref[...])
    ckv = _rmsnorm(kv[:, :KV_LORA], kvnorm_ref[...])
    kpe = kv[:, KV_LORA:KV_LORA + 128] * cos + kv[:, KV_LORA + 128:] * sin
    ckv_ref[0] = ckv
    kpe_ref[0] = kpe[:, :QK_ROPE]
    kcat_ref[0, :, :KV_LORA] = ckv.astype(BF16)
    kcat_ref[0, :, KV_LORA:] = kpe.astype(BF16)

    g = _dot(h, wgate_ref[...])
    sg_ref[0] = (g * _sigmoid(g)).astype(sg_ref.dtype)

    qn = _rmsnorm(_dot(h, wqa_ref[...]), qnorm_ref[...]).astype(BF16)
    q_nope = _dot(qn, wqn_ref[...])
    q_pe = _dot(qn, wqp_ref[...])
    q_ps = _dot(qn, wqs_ref[...])
    second_half = lax.broadcasted_iota(jnp.int32, (x.shape[0], 128), 1) >= QK_ROPE
    for hd in range(MLA_H):
        cols = slice(hd * 128, (hd + 1) * 128)
        q_lat = _dot(q_nope[:, cols], wuk_ref[hd])
        q_ref[0, hd, :, :KV_LORA] = (q_lat * Q_SCALE).astype(BF16)
        pair = slice((hd // 2) * 128, (hd // 2 + 1) * 128)
        roped = (q_pe[:, pair] * cos + q_ps[:, pair] * sin) * Q_SCALE
        keep = second_half if hd % 2 else jnp.logical_not(second_half)
        q_ref[0, hd, :, KV_LORA:] = jnp.where(keep, roped, 0.0).astype(BF16)


def _mla_proj_call(y, g0, x0, mod0, post0, wout0, mod, kvmod, gain, kgain, cos, sin,
                   wqa, wgate, wkv, qnorm, kvnorm, wqn, wqp, wqs, wuk, *, per_row, tm):
    B, T, D = x0.shape
    tile = lambda w: pl.BlockSpec((1, tm, w), lambda b, t: (b, t, 0))
    full = _const_spec
    tab = pl.BlockSpec((tm, 128), lambda b, t: (t, 0))
    weights = (wout0, wqa, wgate, wkv, wqn, wqp, wqs, wuk)
    pipelined = (4 * _nbytes((tm, D), F32) + _nbytes((MLA_H, tm, KCAT), BF16)
                 + 2 * _nbytes((tm, KCAT), F32))
    weight_bytes = sum(_nbytes(w.shape, BF16) for w in weights)
    return pl.pallas_call(
        _mla_proj_kernel,
        grid=(B, T // tm),
        in_specs=[tile(D), tile(D), tile(D), _mod_spec(per_row, tm, D, 2), full((1, D)), full(wout0.shape),
                  _mod_spec(per_row, tm, D, 0), _mod_spec(per_row, tm, D, 1),
                  _mod_spec(per_row, tm, D, 0), _mod_spec(per_row, tm, D, 1),
                  full((1, D)), full((1, D)), tab, tab,
                  full(wqa.shape), full(wgate.shape), full(wkv.shape), full((1, Q_LORA)), full((1, KV_LORA)),
                  full(wqn.shape), full(wqp.shape), full(wqs.shape), full(wuk.shape)],
        out_specs=[tile(D), pl.BlockSpec((1, MLA_H, tm, KCAT), lambda b, t: (b, 0, t, 0)),
                   tile(KCAT), tile(KV_LORA), tile(QK_ROPE), tile(MLA_H * V_HEAD)],
        out_shape=[jax.ShapeDtypeStruct((B, T, D), F32),
                   jax.ShapeDtypeStruct((B, MLA_H, T, KCAT), BF16),
                   jax.ShapeDtypeStruct((B, T, KCAT), BF16),
                   jax.ShapeDtypeStruct((B, T, KV_LORA), F32),
                   jax.ShapeDtypeStruct((B, T, QK_ROPE), F32),
                   jax.ShapeDtypeStruct((B, T, MLA_H * V_HEAD), BF16)],
        compiler_params=_params(("parallel", "parallel"), pipelined,
                                12 * _nbytes((tm, D), F32) + weight_bytes),
        name="mla_proj",
    )(y, g0, x0, mod0, post0, wout0, mod, mod, kvmod, kvmod, gain, kgain, cos, sin,
      wqa, wgate, wkv, qnorm, kvnorm, wqn, wqp, wqs, wuk)


def _lane_tile(t, width):
    return jnp.concatenate([t] * (width // V7X_LANES), axis=1)


def _flash_kernel(q_ref, k_ref, o_ref, m_scr, l_scr, acc_scr, *, tq):
    qi = pl.program_id(1)
    m_scr[...] = jnp.full_like(m_scr, -jnp.inf)
    l_scr[...] = jnp.zeros_like(l_scr)
    acc_scr[...] = jnp.zeros_like(acc_scr)

    def update(ki, masked):
        kc = k_ref[0, pl.ds(pl.multiple_of(ki * tq, tq), tq), :]
        vc = kc[:, :KV_LORA]
        scores = lambda hd: lax.dot_general(q_ref[0, hd], kc, _NT, preferred_element_type=F32)
        if masked:
            causal = (lax.broadcasted_iota(jnp.int32, (tq, tq), 1)
                      <= lax.broadcasted_iota(jnp.int32, (tq, tq), 0))
        groups = [range(g, g + FLASH_HEAD_GROUP) for g in range(0, MLA_H, FLASH_HEAD_GROUP)]
        s_next = [scores(hd) for hd in groups[0]]
        for gi, heads in enumerate(groups):
            s = s_next
            if gi + 1 < len(groups):
                s_next = [scores(hd) for hd in groups[gi + 1]]
            if masked:
                s = [jnp.where(causal, t, -jnp.inf) for t in s]
            m_prev = [m_scr[hd] for hd in heads]
            m_new = [jnp.maximum(mp, jnp.max(t, axis=-1, keepdims=True)) for mp, t in zip(m_prev, s)]
            alpha = [jnp.exp2(mp - mn) for mp, mn in zip(m_prev, m_new)]
            p = [jnp.exp2(t - _lane_tile(mn, tq)) for t, mn in zip(s, m_new)]
            pv = [jnp.dot(t.astype(BF16), vc, preferred_element_type=F32) for t in p]
            for i, hd in enumerate(heads):
                l_scr[hd] = alpha[i] * l_scr[hd] + jnp.sum(p[i], axis=-1, keepdims=True)
                acc_scr[hd] = _lane_tile(alpha[i], KV_LORA) * acc_scr[hd] + pv[i]
                m_scr[hd] = m_new[i]

    def below_diagonal(ki, carry):
        update(ki, False)
        return carry

    lax.fori_loop(0, qi, below_diagonal, 0)
    update(qi, True)
    for hd in range(MLA_H):
        o_ref[0, hd] = (acc_scr[hd] * _lane_tile(1.0 / l_scr[hd], KV_LORA)).astype(BF16)


def _flash_call(q, kcat, *, tq):
    B, H, T, _ = q.shape
    pipelined = (_nbytes((H, tq, KCAT), BF16) + _nbytes((T, KCAT), BF16)
                 + _nbytes((H, tq, KV_LORA), BF16))
    resident = _nbytes((H, tq, KV_LORA + 2 * V7X_LANES), F32) + 8 * _nbytes((tq, tq), F32)
    return pl.pallas_call(
        functools.partial(_flash_kernel, tq=tq),
        grid=(B, T // tq),
        in_specs=[pl.BlockSpec((1, H, tq, KCAT), lambda b, i: (b, 0, i, 0)),
                  pl.BlockSpec((1, T, KCAT), lambda b, i: (b, 0, 0))],
        out_specs=pl.BlockSpec((1, H, tq, KV_LORA), lambda b, i: (b, 0, i, 0)),
        out_shape=jax.ShapeDtypeStruct((B, H, T, KV_LORA), BF16),
        scratch_shapes=[pltpu.VMEM((H, tq, V7X_LANES), F32), pltpu.VMEM((H, tq, V7X_LANES), F32),
                        pltpu.VMEM((H, tq, KV_LORA), F32)],
        compiler_params=_params(("parallel", "parallel"), pipelined, resident),
        name="mla_flash",
    )(q, kcat)


def _decode_kernel(pt_ref, q_ref, cnew_ref, pnew_ref, ckv_hbm, kpe_hbm, o_ref,
                   ck_buf, kp_buf, sems, keys_buf, rope_buf):
    b = pl.program_id(0)
    n_seq = pl.num_programs(0)
    n_pages, ps = ck_buf.shape[1], ck_buf.shape[2]
    slot = lax.rem(b, 2)

    def page_copies(seq, slot_, i):
        page = pt_ref[seq, i]
        return (pltpu.make_async_copy(ckv_hbm.at[page], ck_buf.at[slot_, i], sems.at[slot_]),
                pltpu.make_async_copy(kpe_hbm.at[page], kp_buf.at[slot_, i], sems.at[slot_]))

    def start_gather(seq, slot_):
        for i in range(n_pages):
            for cp in page_copies(seq, slot_, i):
                cp.start()

    @pl.when(b == 0)
    def _():
        start_gather(0, 0)

    @pl.when(b + 1 < n_seq)
    def _():
        start_gather(b + 1, 1 - slot)

    for i in range(n_pages):
        for cp in page_copies(b, slot, i):
            cp.wait()

    for i in range(n_pages):
        keys_buf[i * ps:(i + 1) * ps, :] = ck_buf[slot, i].astype(BF16)
        rope_buf[:, i * ps:(i + 1) * ps] = kp_buf[slot, i].astype(BF16)
    q = q_ref[0]
    q_lat = q[:, :KV_LORA]
    q_pe = q[:, KV_LORA:KV_LORA + QK_ROPE] + q[:, KV_LORA + QK_ROPE:]
    keys = keys_buf[...]
    s = (lax.dot_general(q_lat, keys, _NT, preferred_element_type=F32)
         + jnp.dot(q_pe, rope_buf[...], preferred_element_type=F32))
    cn = cnew_ref[0]
    s_new = (jnp.sum(q_lat.astype(F32) * cn, axis=-1, keepdims=True)
             + jnp.sum(q_pe.astype(F32) * pnew_ref[0], axis=-1, keepdims=True))
    m = jnp.maximum(jnp.max(s, axis=-1, keepdims=True), s_new)
    p = jnp.exp2(s - m)
    p_new = jnp.exp2(s_new - m)
    denom = jnp.sum(p, axis=-1, keepdims=True) + p_new
    acc = jnp.dot(p.astype(BF16), keys, preferred_element_type=F32) + p_new * cn
    o_ref[0] = acc / denom


def _decode_call(page_table, q, c_new, p_new, cache_ckv, cache_kpe):
    B, H, _ = q.shape
    n_pages = page_table.shape[1]
    ps = cache_ckv.shape[1]
    past = n_pages * ps
    scratch = [((2, n_pages, ps, KV_LORA), F32),
               ((2, n_pages, QK_ROPE, ps), F32),
               ((past, KV_LORA), BF16),
               ((QK_ROPE, past), BF16)]
    scratch_bytes = sum(_nbytes(shape, dt) for shape, dt in scratch)
    (ck, kp, kb, rb) = [pltpu.VMEM(shape, dt) for shape, dt in scratch]
    return pl.pallas_call(
        _decode_kernel,
        grid_spec=pltpu.PrefetchScalarGridSpec(
            num_scalar_prefetch=1,
            grid=(B,),
            in_specs=[pl.BlockSpec((1, H, KCAT), lambda b, pt: (b, 0, 0)),
                      pl.BlockSpec((1, 1, KV_LORA), lambda b, pt: (b, 0, 0)),
                      pl.BlockSpec((1, 1, QK_ROPE), lambda b, pt: (b, 0, 0)),
                      pl.BlockSpec(memory_space=pl.ANY),
                      pl.BlockSpec(memory_space=pl.ANY)],
            out_specs=pl.BlockSpec((1, H, KV_LORA), lambda b, pt: (b, 0, 0)),
            scratch_shapes=[ck, kp, pltpu.SemaphoreType.DMA((2,)), kb, rb]),
        out_shape=jax.ShapeDtypeStruct((B, H, KV_LORA), F32),
        compiler_params=_params(("arbitrary",), _nbytes((H, KCAT), F32),
                                scratch_bytes + 4 * _nbytes((H, past), F32)),
        name="mla_decode",
    )(page_table, q, c_new, p_new, cache_ckv, cache_kpe)


def _mla_out_kernel(o_ref, sg_ref, x_ref, gate_ref, gain_ref, wuv_ref, wout_ref, y_ref, og_scr):
    for hd in range(MLA_H):
        cols = slice(hd * V_HEAD, (hd + 1) * V_HEAD)
        o = _dot(o_ref[0, hd], wuv_ref[hd])
        og_scr[:, cols] = (o * sg_ref[0, :, cols].astype(F32)).astype(BF16)
    z = _dot(og_scr[...], wout_ref[...])
    y_ref[0] = x_ref[0] + gate_ref[0] * _rmsnorm(z, gain_ref[...])


def _mla_out_call(o_lat, sg, x, mod, gain, wuv, wout, *, per_row, tm):
    B, T, D = x.shape
    tile = lambda w: pl.BlockSpec((1, tm, w), lambda b, t: (b, t, 0))
    full = _const_spec
    pipelined = _nbytes((MLA_H, tm, KV_LORA), BF16) + 3 * _nbytes((tm, D), F32)
    weight_bytes = _nbytes(wuv.shape, BF16) + _nbytes(wout.shape, BF16)
    return pl.pallas_call(
        _mla_out_kernel,
        grid=(B, T // tm),
        in_specs=[pl.BlockSpec((1, MLA_H, tm, KV_LORA), lambda b, t: (b, 0, t, 0)),
                  tile(MLA_H * V_HEAD), tile(D), _mod_spec(per_row, tm, D, 2), full((1, D)),
                  full(wuv.shape), full(wout.shape)],
        out_specs=tile(D),
        out_shape=jax.ShapeDtypeStruct((B, T, D), F32),
        scratch_shapes=[pltpu.VMEM((tm, MLA_H * V_HEAD), BF16)],
        compiler_params=_params(("parallel", "parallel"), pipelined,
                                4 * _nbytes((tm, D), F32) + weight_bytes),
        name="mla_out",
    )(o_lat, sg, x, mod, gain, wuv, wout)


def _rope_tables(pos):
    half = QK_ROPE // 2
    inv = ROPE_THETA ** (-jnp.arange(half, dtype=F32) / half)
    ang = pos.astype(F32)[:, None] * inv[None, :]
    c, s = jnp.cos(ang), jnp.sin(ang)
    return jnp.concatenate([c, c, c, c], axis=1), jnp.concatenate([-s, s, -s, s], axis=1)


def _swap_halves(w):
    half = w.shape[-1] // 2
    return jnp.concatenate([w[..., half:], w[..., :half]], axis=-1)


def _twice(w):
    return jnp.concatenate([w, w], axis=-1)


def kernel(x_prompt, x_sample, c_prompt, c_sample, state_wkv, state_shift, cache_kv_latent, cache_k_rope, page_table, ada_w, ada_b, norm_pre, norm_post, a_mu, a_w_in, a_w0, a_w1, a_w2, a_a0, a_a1, a_a2, a_k_k, a_k_a, a_r_k, a_gn_w, a_gn_b, a_w_out, kv_ada_w, kv_ada_b, kv_norm, kv_w_a, kv_a_norm, kv_w_b, b_w_in, b_q_norm, b_w_q, b_w_out):
    B, T, D = x_prompt.shape
    DB = x_sample.shape[0]
    H = D // RWKV_HEAD
    assert ada_w.shape[0] == 2 and a_mu.shape[0] == 1 and b_w_in.shape[0] == 1
    assert x_sample.shape[1] == 1 and T % ROW_TILE == 0 and DB % V7X_SUBLANES == 0

    c_all = jnp.concatenate([c_prompt, c_sample], axis=0)
    mods = _ada_call(c_all, ada_w, ada_b)
    kvmods = _ada_call(c_all, kv_ada_w[None], kv_ada_b[None])
    mod_p = [mods[i, :B].reshape(B, 1, 3 * D) for i in range(2)]
    mod_s = [mods[i, B:].reshape(1, DB, 3 * D) for i in range(2)]
    kvmod_p = kvmods[0, :B].reshape(B, 1, 2 * D)
    kvmod_s = kvmods[0, B:].reshape(1, DB, 2 * D)

    row = lambda v: v.reshape(1, -1)
    w_in = a_w_in[0].astype(BF16)
    a_args = (row(norm_pre[0]), a_mu[0], w_in, row(a_w0[0]), a_w1[0].astype(BF16), a_w2[0].astype(BF16),
              row(a_a0[0]), a_a1[0].astype(BF16), a_a2[0].astype(BF16))
    w_out_a = a_w_out[0].astype(BF16)
    k_k, k_a, r_k = row(a_k_k[0]), row(a_k_a[0]), row(a_r_k[0])
    gn_w, gn_b = row(a_gn_w[0]), row(a_gn_b[0])

    w_bin = b_w_in[0]
    wqa = w_bin[:, :Q_LORA].astype(BF16)
    wgate = w_bin[:, Q_LORA:].astype(BF16)
    kv_pe = kv_w_a[:, KV_LORA:]
    wkv = jnp.concatenate([kv_w_a[:, :KV_LORA], _twice(kv_pe), _twice(_swap_halves(kv_pe))],
                          axis=1).astype(BF16)
    w_q = b_w_q[0]
    wqn = w_q[:, :, :QK_NOPE].reshape(Q_LORA, MLA_H * QK_NOPE).astype(BF16)
    wq_pe = w_q[:, :, QK_NOPE:]
    wqp = wq_pe.reshape(Q_LORA, MLA_H * QK_ROPE).astype(BF16)
    wqs = _swap_halves(wq_pe).reshape(Q_LORA, MLA_H * QK_ROPE).astype(BF16)
    wuk = jnp.transpose(kv_w_b[:, :, :QK_NOPE], (1, 2, 0)).astype(BF16)
    wuv = jnp.transpose(kv_w_b[:, :, QK_NOPE:], (1, 0, 2)).astype(BF16)
    wout_b = b_w_out[0].astype(BF16)
    b_args = (wqa, wgate, wkv, row(b_q_norm[0]), row(kv_a_norm), wqn, wqp, wqs, wuk)

    tm = ROW_TILE
    r, lw, k, v, a, g, h_last = _rwkv_proj_call(x_prompt, x_prompt, mod_p[0], *a_args,
                                                seq_shift=True, tm=tm)
    yw, wkv_p = _wkv_chunk_call(r, lw, k, v, a, k_k, k_a, r_k, gn_w, gn_b,
                                t_block=min(T, 512), n_pairs=8)
    wkv_p = wkv_p[None]
    shift_p = h_last.reshape(1, B, D)

    cos_p, sin_p = _rope_tables(jnp.arange(T, dtype=jnp.int32))
    x1, q_p, kcat_p, ckv_p, kpe_p, sg_p = _mla_proj_call(
        yw, g, x_prompt, mod_p[0], row(norm_post[0]), w_out_a,
        mod_p[1], kvmod_p, row(norm_pre[1]), row(kv_norm), cos_p, sin_p, *b_args,
        per_row=False, tm=tm)
    o_p = _flash_call(q_p, kcat_p, tq=min(T, 256))
    y_prompt = _mla_out_call(o_p, sg_p, x1, mod_p[1], row(norm_post[1]), wuv, wout_b,
                             per_row=False, tm=tm)

    xs = x_sample.reshape(1, DB, D)
    rs, lws, ks, vs, as_, gs, hs = _rwkv_proj_call(xs, state_shift[0].reshape(1, DB, D), mod_s[0],
                                                   *a_args, seq_shift=False, tm=DB)
    bm = lambda t: jnp.swapaxes(t.reshape(DB, D), 0, 1)
    pb = lambda t: jnp.broadcast_to(t.reshape(D, 1), (D, DB))
    s_new, yws = _wkv_step_call(jnp.transpose(state_wkv[0], (1, 2, 3, 0)),
                                bm(rs), bm(lws), bm(ks), bm(vs), bm(as_),
                                pb(k_k), pb(k_a), pb(r_k), pb(gn_w), pb(gn_b))
    s_new = jnp.transpose(s_new, (3, 0, 1, 2))
    n_pages = page_table.shape[1]
    past_len = n_pages * cache_kv_latent.shape[1]
    cos_s, sin_s = _rope_tables(jnp.full((DB,), past_len, dtype=jnp.int32))
    x1s, q_s, _, ckv_s, kpe_s, sg_s = _mla_proj_call(
        jnp.swapaxes(yws, 0, 1).reshape(1, DB, D), gs, xs, mod_s[0], row(norm_post[0]), w_out_a,
        mod_s[1], kvmod_s, row(norm_pre[1]), row(kv_norm), cos_s, sin_s, *b_args,
        per_row=True, tm=DB)
    o_s = _decode_call(page_table, jnp.transpose(q_s[0], (1, 0, 2)),
                       ckv_s.reshape(DB, 1, KV_LORA), kpe_s.reshape(DB, 1, QK_ROPE),
                       cache_kv_latent, jnp.swapaxes(cache_k_rope, 1, 2))
    o_s = jnp.transpose(o_s, (1, 0, 2)).astype(BF16)[None]
    y_s = _mla_out_call(o_s, sg_s, x1s, mod_s[1], row(norm_post[1]), wuv, wout_b,
                        per_row=True, tm=DB)

    return (y_prompt, y_s.reshape(DB, 1, D), wkv_p, shift_p, ckv_p, kpe_p,
            s_new[None], hs.reshape(1, DB, D), ckv_s.reshape(DB, 1, KV_LORA),
            kpe_s.reshape(DB, 1, QK_ROPE))
```
